```python
import math
import jax, jax.numpy as jnp
from jax import lax
import numpy as np

D_MODEL = 1024
BATCH = 2
SEQ = 8192
DEPTH = 2

D_MIX = D_MODEL
HG_HEADS = 4
HG_WIDTH = D_MIX // 4
HG_HEAD_DIM = HG_WIDTH // HG_HEADS
HG_KEY_DIM = 64
HG_CHUNK = 16
LRU_WIDTH = D_MIX // 4
LRU_BLOCKS = 4
LRU_BLOCK_DIM = LRU_WIDTH // LRU_BLOCKS
CONV_WIDTH = 4
LRU_C = 8.0
DA_WIDTH = D_MIX - HG_WIDTH - LRU_WIDTH
DA_HEADS = 4
DA_HEAD_DIM = DA_WIDTH // (2 * DA_HEADS)
Q_BLOCK = 128
D_FF = 3584
N_EXPERTS = 8
TOP_K = 2
N_DENSE = (DEPTH + 1) // 2
N_MOE = DEPTH // 2
DN_ALPHA = (2.0 * DEPTH) ** 0.25
DN_BETA = (8.0 * DEPTH) ** -0.25
EPS = 1e-5
PROJ_SIZES = (HG_HEADS * HG_KEY_DIM, HG_HEADS * HG_KEY_DIM, HG_WIDTH, HG_WIDTH,
              LRU_WIDTH, LRU_WIDTH, DA_WIDTH, DA_WIDTH, DA_WIDTH)
PROJ_SPLITS = tuple(sum(PROJ_SIZES[:i + 1]) for i in range(len(PROJ_SIZES) - 1))
D_PROJ = sum(PROJ_SIZES)

kernel_name = 'hybrid_hgrn2_rglru_diffattn_moe_deepnorm'


def _layer_norm(x, g, b):
    xf = x.astype(jnp.float32)
    mu = jnp.mean(xf, axis=-1, keepdims=True)
    var = jnp.mean(jnp.square(xf - mu), axis=-1, keepdims=True)
    y = (xf - mu) * lax.rsqrt(var + EPS) * g.astype(jnp.float32) + b.astype(jnp.float32)
    return y.astype(x.dtype)


def _rms_heads(x, g, n_heads):
    B, S, W = x.shape
    xf = x.astype(jnp.float32).reshape(B, S, n_heads, W // n_heads)
    xf = xf * lax.rsqrt(jnp.mean(jnp.square(xf), axis=-1, keepdims=True) + EPS)
    return xf.reshape(B, S, W) * g.astype(jnp.float32)


def _hgrn2(q_raw, f_raw, i_raw, g_raw, lb, norm_g):
    B, S, _ = q_raw.shape
    H, DK, DV, C = HG_HEADS, HG_KEY_DIM, HG_HEAD_DIM, HG_CHUNK
    N = S // C
    z = f_raw.astype(jnp.float32)
    lbv = lb.astype(jnp.float32)
    logf = jnp.log(lbv + (1.0 - lbv) * jax.nn.sigmoid(z))
    k = (1.0 - lbv) * jax.nn.sigmoid(-z)
    q = jax.nn.silu(q_raw.astype(jnp.float32))
    v = i_raw.astype(jnp.float32)

    def to_chunks(t, d):
        return t.reshape(B, N, C, H, d).transpose(0, 3, 1, 2, 4)

    q, logf, k, v = to_chunks(q, DK), to_chunks(logf, DK), to_chunks(k, DK), to_chunks(v, DV)
    b = jnp.cumsum(logf, axis=3)
    causal = jnp.tril(jnp.ones((C, C), dtype=bool))[:, :, None]
    rel = b[:, :, :, :, None, :] - b[:, :, :, None, :, :]
    decay = jnp.exp(jnp.where(causal, rel, -jnp.inf))
    scores = jnp.einsum('bhntk,bhnsk,bhntsk->bhnts', q, k, decay)
    o_intra = jnp.einsum('bhnts,bhnsv->bhntv', scores, v)
    b_last = b[:, :, :, -1:, :]
    u = jnp.einsum('bhnsk,bhnsv->bhnkv', k * jnp.exp(b_last - b), v)
    chunk_decay = jnp.exp(b_last[:, :, :, 0, :])

    def step(state, inp):
        d_n, u_n = inp
        return d_n[..., None] * state + u_n, state

    _, s_in = lax.scan(step, jnp.zeros((B, H, DK, DV), jnp.float32),
                       (jnp.moveaxis(chunk_decay, 2, 0), jnp.moveaxis(u, 2, 0)))
    s_in = jnp.moveaxis(s_in, 0, 2)
    o_inter = jnp.einsum('bhntk,bhnkv->bhntv', q * jnp.exp(b), s_in)
    o = (o_intra + o_inter).transpose(0, 2, 3, 1, 4).reshape(B, S, H * DV)
    return _rms_heads(o, norm_g, H) * jax.nn.silu(g_raw.astype(jnp.float32))


def _linear_combine(c1, c2):
    a1, b1 = c1
    a2, b2 = c2
    return a1 * a2, a2 * b1 + b2


def _rglru_block(x_raw, gate_raw, conv_w, conv_b, wa, ba, wx, bx, lam):
    B, S, W = x_raw.shape
    xc = lax.conv_general_dilated(x_raw, conv_w[:, None, :], window_strides=(1,),
                                  padding=[(CONV_WIDTH - 1, 0)],
                                  dimension_numbers=('NWC', 'WIO', 'NWC'),
                                  feature_group_count=W) + conv_b
    xb = xc.reshape(B, S, LRU_BLOCKS, LRU_BLOCK_DIM)
    r = jax.nn.sigmoid((jnp.einsum('bsgi,gij->bsgj', xb, wa).reshape(B, S, W) + ba).astype(jnp.float32))
    i = jax.nn.sigmoid((jnp.einsum('bsgi,gij->bsgj', xb, wx).reshape(B, S, W) + bx).astype(jnp.float32))
    log_a = LRU_C * r * jax.nn.log_sigmoid(lam.astype(jnp.float32))
    a = jnp.exp(log_a)
    u = jnp.sqrt(-jnp.expm1(2.0 * log_a)) * i * xc.astype(jnp.float32)
    _, h = lax.associative_scan(_linear_combine, (a, u), axis=1)
    return h * jax.nn.gelu(gate_raw.astype(jnp.float32))


def _diff_attention(q_raw, k_raw, v_raw, lq1, lk1, lq2, lk2, norm_g, layer):
    B, S, _ = q_raw.shape
    H, d = DA_HEADS, DA_HEAD_DIM
    lam_init = 0.8 - 0.6 * math.exp(-0.3 * layer)
    lam = (jnp.exp(jnp.sum(lq1.astype(jnp.float32) * lk1.astype(jnp.float32)))
           - jnp.exp(jnp.sum(lq2.astype(jnp.float32) * lk2.astype(jnp.float32))) + lam_init)
    q = q_raw.reshape(B, S, H, 2, d).transpose(0, 2, 3, 1, 4)
    k = k_raw.reshape(B, S, H, 2, d).transpose(0, 2, 3, 1, 4)
    v = v_raw.reshape(B, S, H, 2 * d).transpose(0, 2, 1, 3)
    scale = d ** -0.5
    outs = []
    for blk in range(S // Q_BLOCK):
        s0 = blk * Q_BLOCK
        s1 = s0 + Q_BLOCK
        sc = jnp.einsum('bhcqd,bhckd->bhcqk', q[:, :, :, s0:s1], k[:, :, :, :s1]).astype(jnp.float32) * scale
        mask = jnp.arange(s1)[None, :] <= (s0 + jnp.arange(Q_BLOCK))[:, None]
        p = jax.nn.softmax(jnp.where(mask, sc, -jnp.inf), axis=-1)
        w = p[:, :, 0] - lam * p[:, :, 1]
        outs.append(jnp.einsum('bhqk,bhkv->bhqv', w.astype(v.dtype), v[:, :, :s1]))
    o = jnp.concatenate(outs, axis=2).transpose(0, 2, 1, 3).reshape(B, S, H * 2 * d)
    return _rms_heads(o, norm_g, H) * (1.0 - lam_init)


def _mixer(x, w_in, w_out, lb, hg_norm_g, conv_w, conv_b, wa, ba, wx, bx, lru_lambda,
           lq1, lk1, lq2, lk2, da_norm_g, layer):
    proj = x @ w_in
    hq, hf, hi, hg, lx, lg, dq, dk, dv = jnp.split(proj, PROJ_SPLITS, axis=-1)
    o_hg = _hgrn2(hq, hf, hi, hg, lb, hg_norm_g)
    o_lru = _rglru_block(lx, lg, conv_w, conv_b, wa, ba, wx, bx, lru_lambda)
    o_da = _diff_attention(dq, dk, dv, lq1, lk1, lq2, lk2, da_norm_g, layer)
    o = jnp.concatenate([o_hg, o_lru, o_da], axis=-1).astype(x.dtype)
    return o @ w_out


def _swiglu(x, wg, wu, wd):
    return (jax.nn.silu(x @ wg) * (x @ wu)) @ wd


def _moe(x, router_w, wg, wu, wd):
    B, S, D = x.shape
    xf = x.reshape(B * S, D)
    logits = (xf @ router_w).astype(jnp.float32)
    top_v, top_i = lax.top_k(logits, TOP_K)
    gates = jax.nn.softmax(top_v, axis=-1)
    combine = jnp.sum(jax.nn.one_hot(top_i, N_EXPERTS, dtype=jnp.float32) * gates[..., None], axis=1)
    y = jnp.zeros_like(xf)
    for e in range(N_EXPERTS):
        y = y + combine[:, e:e + 1].astype(x.dtype) * _swiglu(xf, wg[e], wu[e], wd[e])
    return y.reshape(B, S, D)


def setup_inputs(seed: int = 0) -> dict:
    key = jax.random.key(seed)
    ks = jax.random.split(key, 28)

    def nrm(k, shape, scale):
        return jax.random.normal(k, shape, jnp.float32) * scale

    u = jax.random.uniform(ks[11], (DEPTH, LRU_WIDTH), jnp.float32, 0.9, 0.999)
    s = u ** (1.0 / LRU_C)
    return {
        'x': nrm(ks[0], (BATCH, SEQ, D_MODEL), 1.0),
        'w_in': nrm(ks[1], (DEPTH, D_MODEL, D_PROJ), D_MODEL ** -0.5),
        'w_out': nrm(ks[2], (DEPTH, D_MIX, D_MODEL), DN_BETA * D_MIX ** -0.5),
        'hg_lb_logits': nrm(ks[3], (DEPTH, HG_HEADS * HG_KEY_DIM), 0.5),
        'hg_norm_g': 1.0 + nrm(ks[4], (DEPTH, HG_WIDTH), 0.02),
        'lru_conv_w': nrm(ks[5], (DEPTH, CONV_WIDTH, LRU_WIDTH), CONV_WIDTH ** -0.5),
        'lru_conv_b': nrm(ks[6], (DEPTH, LRU_WIDTH), 0.01),
        'lru_wa': nrm(ks[7], (DEPTH, LRU_BLOCKS, LRU_BLOCK_DIM, LRU_BLOCK_DIM), LRU_BLOCK_DIM ** -0.5),
        'lru_ba': nrm(ks[8], (DEPTH, LRU_WIDTH), 0.01),
        'lru_wx': nrm(ks[9], (DEPTH, LRU_BLOCKS, LRU_BLOCK_DIM, LRU_BLOCK_DIM), LRU_BLOCK_DIM ** -0.5),
        'lru_bx': nrm(ks[10], (DEPTH, LRU_WIDTH), 0.01),
        'lru_lambda': jnp.log(s) - jnp.log1p(-s),
        'da_lq1': nrm(ks[12], (DEPTH, DA_HEAD_DIM), 0.1),
        'da_lk1': nrm(ks[13], (DEPTH, DA_HEAD_DIM), 0.1),
        'da_lq2': nrm(ks[14], (DEPTH, DA_HEAD_DIM), 0.1),
        'da_lk2': nrm(ks[15], (DEPTH, DA_HEAD_DIM), 0.1),
        'da_norm_g': 1.0 + nrm(ks[16], (DEPTH, DA_WIDTH), 0.02),
        'ln1_g': 1.0 + nrm(ks[17], (DEPTH, D_MODEL), 0.02),
        'ln1_b': nrm(ks[18], (DEPTH, D_MODEL), 0.02),
        'ln2_g': 1.0 + nrm(ks[19], (DEPTH, D_MODEL), 0.02),
        'ln2_b': nrm(ks[20], (DEPTH, D_MODEL), 0.02),
        'ffn_wg': nrm(ks[21], (N_DENSE, D_MODEL, D_FF), D_MODEL ** -0.5),
        'ffn_wu': nrm(ks[22], (N_DENSE, D_MODEL, D_FF), D_MODEL ** -0.5),
        'ffn_wd': nrm(ks[23], (N_DENSE, D_FF, D_MODEL), DN_BETA * D_FF ** -0.5),
        'router_w': nrm(ks[24], (N_MOE, D_MODEL, N_EXPERTS), D_MODEL ** -0.5),
        'moe_wg': nrm(ks[25], (N_MOE, N_EXPERTS, D_MODEL, D_FF), D_MODEL ** -0.5),
        'moe_wu': nrm(ks[26], (N_MOE, N_EXPERTS, D_MODEL, D_FF), D_MODEL ** -0.5),
        'moe_wd': nrm(ks[27], (N_MOE, N_EXPERTS, D_FF, D_MODEL), DN_BETA * D_FF ** -0.5),
    }


def reference(x, w_in, w_out, hg_lb_logits, hg_norm_g, lru_conv_w, lru_conv_b, lru_wa, lru_ba,
              lru_wx, lru_bx, lru_lambda, da_lq1, da_lk1, da_lq2, da_lk2, da_norm_g,
              ln1_g, ln1_b, ln2_g, ln2_b, ffn_wg, ffn_wu, ffn_wd, router_w, moe_wg, moe_wu, moe_wd):
    lb_p = jax.nn.softmax(hg_lb_logits.astype(jnp.float32), axis=0)
    lb_all = jnp.cumsum(lb_p, axis=0) - lb_p[0:1]
    for l in range(DEPTH):
        h = _mixer(x, w_in[l], w_out[l], lb_all[l], hg_norm_g[l], lru_conv_w[l], lru_conv_b[l],
                   lru_wa[l], lru_ba[l], lru_wx[l], lru_bx[l], lru_lambda[l],
                   da_lq1[l], da_lk1[l], da_lq2[l], da_lk2[l], da_norm_g[l], l)
        x = _layer_norm(DN_ALPHA * x + h, ln1_g[l], ln1_b[l])
        j = l // 2
        if l % 2 == 0:
            f = _swiglu(x, ffn_wg[j], ffn_wu[j], ffn_wd[j])
        else:
            f = _moe(x, router_w[j], moe_wg[j], moe_wu[j], moe_wd[j])
        x = _layer_norm(DN_ALPHA * x + f, ln2_g[l], ln2_b[l])
    return x
```

```python
import functools
import math

import jax
import jax.numpy as jnp
from jax import lax
from jax.experimental import pallas as pl
from jax.experimental.pallas import tpu as pltpu

D_MODEL = 1024
DEPTH = 2
HG_HEADS = 4
HG_KEY_DIM = 64
HG_WIDTH = 256
HG_CHUNK = 16
LRU_WIDTH = 256
LRU_BLOCKS = 4
CONV_WIDTH = 4
LRU_C = 8.0
DA_WIDTH = 512
DA_HEADS = 4
DA_HEAD_DIM = 64
D_FF = 3584
N_EXPERTS = 8
DN_ALPHA = (2.0 * DEPTH) ** 0.25
EPS = 1e-5
REC_WIDTH = 4 * HG_WIDTH + 2 * LRU_WIDTH
ATT_WIDTH = 3 * DA_WIDTH

V7X_VMEM_BYTES = 64 * 1024 * 1024
VMEM_LIMIT = 48 * 1024 * 1024

BF16 = jnp.bfloat16
F32 = jnp.float32


def _params(*semantics):
    return pltpu.CompilerParams(dimension_semantics=semantics, vmem_limit_bytes=VMEM_LIMIT)


def _layer_norm_rows(y, g, b):
    mu = jnp.mean(y, axis=-1, keepdims=True)
    yc = y - mu
    var = jnp.mean(yc * yc, axis=-1, keepdims=True)
    return yc * lax.rsqrt(var + EPS) * g + b


def _in_proj_kernel(x_ref, w_ref, rec_ref, att_ref):
    xb = x_ref[...].astype(BF16)
    rec_ref[...] = jnp.dot(xb, w_ref[:, :REC_WIDTH], preferred_element_type=F32)
    att_ref[...] = jnp.dot(xb, w_ref[:, REC_WIDTH:], preferred_element_type=F32).astype(BF16)


def _in_proj(x, w_bf16, tm=512):
    T = x.shape[0]
    return pl.pallas_call(
        _in_proj_kernel,
        grid=(T // tm,),
        in_specs=[pl.BlockSpec((tm, D_MODEL), lambda i: (i, 0)),
                  pl.BlockSpec((D_MODEL, REC_WIDTH + ATT_WIDTH), lambda i: (0, 0))],
        out_specs=[pl.BlockSpec((tm, REC_WIDTH), lambda i: (i, 0)),
                   pl.BlockSpec((tm, ATT_WIDTH), lambda i: (i, 0))],
        out_shape=[jax.ShapeDtypeStruct((T, REC_WIDTH), F32),
                   jax.ShapeDtypeStruct((T, ATT_WIDTH), BF16)],
        compiler_params=_params("parallel"),
        name="in_proj",
    )(x, w_bf16)


def _hgrn_kernel(rec_ref, lb_ref, ng_ref, gmat_ref, sel_ref, bmask_ref, o_ref, st_ref, *, rows):
    @pl.when(pl.program_id(1) == 0)
    def _():
        st_ref[...] = jnp.zeros_like(st_ref)

    C = HG_CHUNK
    W = HG_WIDTH
    lb = lb_ref[...]
    ng = ng_ref[...]
    gmat = gmat_ref[...]
    sel = sel_ref[...]
    row = lax.broadcasted_iota(jnp.int32, (C, W), 0)

    def chunk(c, carry):
        r0 = pl.multiple_of(c * C, C)
        qr = rec_ref[pl.ds(r0, C), 0:W]
        z = rec_ref[pl.ds(r0, C), W:2 * W]
        v = rec_ref[pl.ds(r0, C), 2 * W:3 * W]
        g = rec_ref[pl.ds(r0, C), 3 * W:4 * W]
        logf = jnp.log(lb + (1.0 - lb) * jax.nn.sigmoid(z))
        kk = (1.0 - lb) * jax.nn.sigmoid(-z)
        q = qr * jax.nn.sigmoid(qr)
        b = logf
        for s in (1, 2, 4, 8):
            b = b + jnp.where(row >= s, pltpu.roll(b, s, 0), 0.0)
        v_b = v.astype(BF16)
        w_rows = []
        for t in range(C):
            rel = b[t:t + 1, :] - b
            dec = jnp.exp(jnp.where(row <= t, rel, -jnp.inf))
            w_rows.append(((q[t:t + 1, :] * kk) * dec).astype(BF16))
        w2 = jnp.concatenate(w_rows, axis=0)
        a = jnp.dot(w2, gmat, preferred_element_type=F32)
        p = a * jnp.concatenate([v] * C, axis=0)
        o_intra = jnp.dot(sel, p.astype(BF16), preferred_element_type=F32)
        st = st_ref[...]
        qd = (q * jnp.exp(b)).astype(BF16)
        o_inter = lax.dot_general(qd, st.astype(BF16), (((1,), (1,)), ((), ())),
                                  preferred_element_type=F32)
        b_last = b[C - 1:C, :]
        kd = (kk * jnp.exp(b_last - b)).astype(BF16)
        u_t = lax.dot_general(v_b, kd, (((0,), (0,)), ((), ())), preferred_element_type=F32)
        st_ref[...] = st * jnp.exp(b_last) + u_t * bmask_ref[...]
        o = o_intra + o_inter
        ms = jnp.dot(o * o, gmat.astype(F32), preferred_element_type=F32,
                     precision=lax.Precision.HIGHEST) * (1.0 / HG_KEY_DIM)
        out = o * lax.rsqrt(ms + EPS) * ng * (g * jax.nn.sigmoid(g))
        o_ref[pl.ds(r0, C), :] = out.astype(o_ref.dtype)
        return carry

    lax.fori_loop(0, rows // C, chunk, 0)


def _hgrn(rec, lb, norm_g, batch, seq, rows=256):
    T = rec.shape[0]
    nblk = seq // rows
    head = jnp.arange(HG_WIDTH) // HG_KEY_DIM
    same_head = head[:, None] == head[None, :]
    gmat = same_head.astype(BF16)
    bmask = same_head.astype(F32)
    sel = (jnp.arange(HG_CHUNK)[:, None] == (jnp.arange(HG_CHUNK * HG_CHUNK) // HG_CHUNK)[None, :]).astype(BF16)
    const = lambda shape: pl.BlockSpec(shape, lambda b, i: (0, 0))
    return pl.pallas_call(
        functools.partial(_hgrn_kernel, rows=rows),
        grid=(batch, nblk),
        in_specs=[pl.BlockSpec((rows, 4 * HG_WIDTH), lambda b, i: (b * nblk + i, 0)),
                  const((1, HG_WIDTH)), const((1, HG_WIDTH)),
                  const((HG_WIDTH, HG_WIDTH)), const((HG_CHUNK, HG_CHUNK * HG_CHUNK)),
                  const((HG_WIDTH, HG_WIDTH))],
        out_specs=pl.BlockSpec((rows, HG_WIDTH), lambda b, i: (b * nblk + i, 0)),
        out_shape=jax.ShapeDtypeStruct((T, HG_WIDTH), BF16),
        scratch_shapes=[pltpu.VMEM((HG_WIDTH, HG_WIDTH), F32)],
        compiler_params=_params("parallel", "arbitrary"),
        name="hgrn2",
    )(rec, lb.reshape(1, -1), norm_g.reshape(1, -1), gmat, sel, bmask)


def _lru_kernel(rec_ref, cw_ref, cb_ref, wa_ref, wx_ref, ba_ref, bx_ref, lam_ref, o_ref,
                xprev_ref, h_ref, *, rows):
    @pl.when(pl.program_id(1) == 0)
    def _():
        xprev_ref[...] = jnp.zeros_like(xprev_ref)
        h_ref[...] = jnp.zeros_like(h_ref)

    W = LRU_WIDTH
    x = rec_ref[:, 0:W]
    gate = rec_ref[:, W:2 * W]
    row = lax.broadcasted_iota(jnp.int32, (rows, W), 0)
    xp = xprev_ref[...]
    tail = jnp.zeros((rows - 8, W), F32)
    xc = cb_ref[...] + cw_ref[CONV_WIDTH - 1:CONV_WIDTH, :] * x
    for j in range(1, CONV_WIDTH):
        prev = jnp.concatenate([pltpu.roll(xp, j, 0), tail], axis=0)
        xs = jnp.where(row >= j, pltpu.roll(x, j, 0), prev)
        xc = xc + cw_ref[CONV_WIDTH - 1 - j:CONV_WIDTH - j, :] * xs
    xprev_ref[...] = x[rows - 8:rows, :]

    xcb = xc.astype(BF16)
    r = jax.nn.sigmoid(jnp.dot(xcb, wa_ref[...], preferred_element_type=F32) + ba_ref[...])
    i = jax.nn.sigmoid(jnp.dot(xcb, wx_ref[...], preferred_element_type=F32) + bx_ref[...])
    lam = lam_ref[...]
    log_sig = jnp.minimum(lam, 0.0) - jnp.log1p(jnp.exp(-jnp.abs(lam)))
    log_a = LRU_C * r * log_sig
    a = jnp.exp(log_a)
    th = jnp.tanh(log_a)
    one_minus_a2 = -2.0 * th / (1.0 - th)
    u = jnp.sqrt(one_minus_a2) * i * xc

    for s in (1, 2, 4):
        a_s = jnp.where(row >= s, pltpu.roll(a, s, 0), 1.0)
        u_s = jnp.where(row >= s, pltpu.roll(u, s, 0), 0.0)
        u = a * u_s + u
        a = a * a_s
    s = 8
    while s < rows:
        a_s = jnp.concatenate([jnp.ones((s, W), F32), a[:rows - s, :]], axis=0)
        u_s = jnp.concatenate([jnp.zeros((s, W), F32), u[:rows - s, :]], axis=0)
        u = a * u_s + u
        a = a * a_s
        s *= 2
    h = a * h_ref[0:1, :] + u
    h_ref[...] = jnp.broadcast_to(h[rows - 1:rows, :], h_ref.shape)
    o_ref[...] = (h * jax.nn.gelu(gate)).astype(o_ref.dtype)


def _lru(rec, conv_w, conv_b, wa_bd, wx_bd, ba, bx, lam, batch, seq, rows=256):
    T = rec.shape[0]
    nblk = seq // rows
    W = LRU_WIDTH
    const = lambda shape: pl.BlockSpec(shape, lambda b, i: (0, 0))
    return pl.pallas_call(
        functools.partial(_lru_kernel, rows=rows),
        grid=(batch, nblk),
        in_specs=[pl.BlockSpec((rows, 2 * W), lambda b, i: (b * nblk + i, 2)),
                  const((CONV_WIDTH, W)), const((1, W)), const((W, W)), const((W, W)),
                  const((1, W)), const((1, W)), const((1, W))],
        out_specs=pl.BlockSpec((rows, W), lambda b, i: (b * nblk + i, 0)),
        out_shape=jax.ShapeDtypeStruct((T, W), BF16),
        scratch_shapes=[pltpu.VMEM((8, W), F32), pltpu.VMEM((8, W), F32)],
        compiler_params=_params("parallel", "arbitrary"),
        name="rglru",
    )(rec, conv_w, conv_b.reshape(1, -1), wa_bd, wx_bd, ba.reshape(1, -1), bx.reshape(1, -1),
      lam.reshape(1, -1))


def _attn_kernel(q_ref, k_ref, v_ref, lq1_ref, lk1_ref, lq2_ref, lk2_ref, ng_ref, o_ref,
                 m_ref, l_ref, acc_ref, *, tq, tk, lam_init):
    d = DA_HEAD_DIM
    qi = pl.program_id(2)
    q = q_ref[...] * (d ** -0.5)
    qs = (q[:, 0:d], q[:, d:2 * d])
    m_ref[...] = jnp.full_like(m_ref, -jnp.inf)
    l_ref[...] = jnp.zeros_like(l_ref)
    acc_ref[...] = jnp.zeros_like(acc_ref)
    kpq = tq // tk

    def block(j, masked):
        k0 = pl.multiple_of(j * tk, tk)
        kb = k_ref[pl.ds(k0, tk), :]
        vb = v_ref[pl.ds(k0, tk), :]
        for c in range(2):
            s = lax.dot_general(qs[c], kb[:, c * d:(c + 1) * d], (((1,), (1,)), ((), ())),
                                preferred_element_type=F32)
            if masked:
                qpos = qi * tq + lax.broadcasted_iota(jnp.int32, (tq, tk), 0)
                kpos = k0 + lax.broadcasted_iota(jnp.int32, (tq, tk), 1)
                s = jnp.where(kpos <= qpos, s, -jnp.inf)
            m_old = m_ref[c]
            m_new = jnp.maximum(m_old, jnp.max(s, axis=-1, keepdims=True))
            p = jnp.exp(s - m_new)
            alpha = jnp.exp(m_old - m_new)
            l_ref[c] = alpha * l_ref[c] + jnp.sum(p, axis=-1, keepdims=True)
            acc_ref[c] = alpha * acc_ref[c] + jnp.dot(p.astype(BF16), vb, preferred_element_type=F32)
            m_ref[c] = m_new

    def full_block(j, carry):
        block(j, False)
        return carry

    lax.fori_loop(0, qi * kpq, full_block, 0)
    for jj in range(kpq):
        block(qi * kpq + jj, True)

    lam = (jnp.exp(jnp.sum(lq1_ref[...] * lk1_ref[...], axis=-1, keepdims=True))
           - jnp.exp(jnp.sum(lq2_ref[...] * lk2_ref[...], axis=-1, keepdims=True)) + lam_init)
    o = acc_ref[0] / l_ref[0] - lam * (acc_ref[1] / l_ref[1])
    ms = jnp.mean(o * o, axis=-1, keepdims=True)
    o_ref[...] = (o * lax.rsqrt(ms + EPS) * ng_ref[...] * (1.0 - lam_init)).astype(o_ref.dtype)


def _attention(att, lq1, lk1, lq2, lk2, norm_g, layer, batch, seq, tq=256, tk=256):
    T = att.shape[0]
    nq = seq // tq
    hw = 2 * DA_HEAD_DIM
    lam_init = 0.8 - 0.6 * math.exp(-0.3 * layer)
    vec = lambda: pl.BlockSpec((1, DA_HEAD_DIM), lambda b, h, i: (0, 0))
    return pl.pallas_call(
        functools.partial(_attn_kernel, tq=tq, tk=tk, lam_init=lam_init),
        grid=(batch, DA_HEADS, nq),
        in_specs=[pl.BlockSpec((tq, hw), lambda b, h, i: (b * nq + i, h)),
                  pl.BlockSpec((seq, hw), lambda b, h, i: (b, DA_HEADS + h)),
                  pl.BlockSpec((seq, hw), lambda b, h, i: (b, 2 * DA_HEADS + h)),
                  vec(), vec(), vec(), vec(),
                  pl.BlockSpec((1, hw), lambda b, h, i: (0, h))],
        out_specs=pl.BlockSpec((tq, hw), lambda b, h, i: (b * nq + i, h)),
        out_shape=jax.ShapeDtypeStruct((T, DA_WIDTH), BF16),
        scratch_shapes=[pltpu.VMEM((2, tq, 1), F32), pltpu.VMEM((2, tq, 1), F32),
                        pltpu.VMEM((2, tq, hw), F32)],
        compiler_params=_params("parallel", "parallel", "arbitrary"),
        name="diff_attn",
    )(att, att, att, lq1.reshape(1, -1), lk1.reshape(1, -1), lq2.reshape(1, -1),
      lk2.reshape(1, -1), norm_g.reshape(1, -1))


def _out_proj_kernel(hg_ref, lru_ref, da_ref, w_ref, x_ref, g_ref, b_ref, o_ref):
    h = jnp.dot(hg_ref[...], w_ref[0:HG_WIDTH, :], preferred_element_type=F32)
    h = h + jnp.dot(lru_ref[...], w_ref[HG_WIDTH:HG_WIDTH + LRU_WIDTH, :], preferred_element_type=F32)
    h = h + jnp.dot(da_ref[...], w_ref[HG_WIDTH + LRU_WIDTH:, :], preferred_element_type=F32)
    o_ref[...] = _layer_norm_rows(DN_ALPHA * x_ref[...] + h, g_ref[...], b_ref[...])


def _out_proj_ln(o_hg, o_lru, o_da, w_bf16, x, g, b, tm=512):
    T = x.shape[0]
    rows = lambda w: pl.BlockSpec((tm, w), lambda i: (i, 0))
    const = lambda shape: pl.BlockSpec(shape, lambda i: (0, 0))
    return pl.pallas_call(
        _out_proj_kernel,
        grid=(T // tm,),
        in_specs=[rows(HG_WIDTH), rows(LRU_WIDTH), rows(DA_WIDTH), const((D_MODEL, D_MODEL)),
                  rows(D_MODEL), const((1, D_MODEL)), const((1, D_MODEL))],
        out_specs=rows(D_MODEL),
        out_shape=jax.ShapeDtypeStruct((T, D_MODEL), F32),
        compiler_params=_params("parallel"),
        name="out_proj_ln",
    )(o_hg, o_lru, o_da, w_bf16, x, g.reshape(1, -1), b.reshape(1, -1))


def _ffn_kernel(x_ref, wg_ref, wu_ref, wd_ref, g_ref, b_ref, o_ref, xb_ref, acc_ref):
    f = pl.program_id(1)

    @pl.when(f == 0)
    def _():
        xb_ref[...] = x_ref[...].astype(BF16)
        acc_ref[...] = jnp.zeros_like(acc_ref)

    xb = xb_ref[...]
    gate = jnp.dot(xb, wg_ref[...], preferred_element_type=F32)
    up = jnp.dot(xb, wu_ref[...], preferred_element_type=F32)
    hmid = (gate * jax.nn.sigmoid(gate) * up).astype(BF16)
    acc_ref[...] += jnp.dot(hmid, wd_ref[...], preferred_element_type=F32)

    @pl.when(f == pl.num_programs(1) - 1)
    def _():
        o_ref[...] = _layer_norm_rows(DN_ALPHA * x_ref[...] + acc_ref[...], g_ref[...], b_ref[...])


def _ffn_ln(x, wg, wu, wd, g, b, tm=1024, tf=512):
    T = x.shape[0]
    return pl.pallas_call(
        _ffn_kernel,
        grid=(T // tm, D_FF // tf),
        in_specs=[pl.BlockSpec((tm, D_MODEL), lambda i, f: (i, 0)),
                  pl.BlockSpec((D_MODEL, tf), lambda i, f: (0, f)),
                  pl.BlockSpec((D_MODEL, tf), lambda i, f: (0, f)),
                  pl.BlockSpec((tf, D_MODEL), lambda i, f: (f, 0)),
                  pl.BlockSpec((1, D_MODEL), lambda i, f: (0, 0)),
                  pl.BlockSpec((1, D_MODEL), lambda i, f: (0, 0))],
        out_specs=pl.BlockSpec((tm, D_MODEL), lambda i, f: (i, 0)),
        out_shape=jax.ShapeDtypeStruct((T, D_MODEL), F32),
        scratch_shapes=[pltpu.VMEM((tm, D_MODEL), BF16), pltpu.VMEM((tm, D_MODEL), F32)],
        compiler_params=_params("parallel", "arbitrary"),
        name="ffn_ln",
    )(x, wg, wu, wd, g.reshape(1, -1), b.reshape(1, -1))


def _moe_kernel(x_ref, rw_ref, wg_ref, wu_ref, wd_ref, g_ref, b_ref, o_ref, xb_ref, comb_ref, acc_ref):
    e = pl.program_id(1)
    f = pl.program_id(2)

    @pl.when((e == 0) & (f == 0))
    def _():
        x = x_ref[...]
        xb_ref[...] = x.astype(BF16)
        acc_ref[...] = jnp.zeros_like(acc_ref)
        logits = jnp.dot(x, rw_ref[...], preferred_element_type=F32, precision=lax.Precision.HIGHEST)
        lane = lax.broadcasted_iota(jnp.int32, logits.shape, 1)
        valid = lane < N_EXPERTS
        neg = -jnp.inf
        l1 = jnp.where(valid, logits, neg)
        v1 = jnp.max(l1, axis=-1, keepdims=True)
        i1 = jnp.min(jnp.where(l1 == v1, lane, logits.shape[1]), axis=-1, keepdims=True)
        l2 = jnp.where(lane == i1, neg, l1)
        v2 = jnp.max(l2, axis=-1, keepdims=True)
        i2 = jnp.min(jnp.where(l2 == v2, lane, logits.shape[1]), axis=-1, keepdims=True)
        e2 = jnp.exp(v2 - v1)
        g1 = 1.0 / (1.0 + e2)
        g2 = e2 / (1.0 + e2)
        comb_ref[...] = jnp.where(lane == i1, g1, 0.0) + jnp.where(lane == i2, g2, 0.0)

    lane = lax.broadcasted_iota(jnp.int32, comb_ref.shape, 1)
    ce = jnp.sum(jnp.where(lane == e, comb_ref[...], 0.0), axis=-1, keepdims=True)
    xb = xb_ref[...]
    gate = jnp.dot(xb, wg_ref[0], preferred_element_type=F32)
    up = jnp.dot(xb, wu_ref[0], preferred_element_type=F32)
    hmid = (gate * jax.nn.sigmoid(gate) * up * ce).astype(BF16)
    acc_ref[...] += jnp.dot(hmid, wd_ref[0], preferred_element_type=F32)

    @pl.when((e == pl.num_programs(1) - 1) & (f == pl.num_programs(2) - 1))
    def _():
        o_ref[...] = _layer_norm_rows(DN_ALPHA * x_ref[...] + acc_ref[...], g_ref[...], b_ref[...])


def _moe_ln(x, router_pad, wg, wu, wd, g, b, tm=1024, tf=512):
    T = x.shape[0]
    lanes = router_pad.shape[1]
    return pl.pallas_call(
        _moe_kernel,
        grid=(T // tm, N_EXPERTS, D_FF // tf),
        in_specs=[pl.BlockSpec((tm, D_MODEL), lambda i, e, f: (i, 0)),
                  pl.BlockSpec((D_MODEL, lanes), lambda i, e, f: (0, 0)),
                  pl.BlockSpec((1, D_MODEL, tf), lambda i, e, f: (e, 0, f)),
                  pl.BlockSpec((1, D_MODEL, tf), lambda i, e, f: (e, 0, f)),
                  pl.BlockSpec((1, tf, D_MODEL), lambda i, e, f: (e, f, 0)),
                  pl.BlockSpec((1, D_MODEL), lambda i, e, f: (0, 0)),
                  pl.BlockSpec((1, D_MODEL), lambda i, e, f: (0, 0))],
        out_specs=pl.BlockSpec((tm, D_MODEL), lambda i, e, f: (i, 0)),
        out_shape=jax.ShapeDtypeStruct((T, D_MODEL), F32),
        scratch_shapes=[pltpu.VMEM((tm, D_MODEL), BF16), pltpu.VMEM((tm, lanes), F32),
                        pltpu.VMEM((tm, D_MODEL), F32)],
        compiler_params=_params("parallel", "arbitrary", "arbitrary"),
        name="moe_ln",
    )(x, router_pad, wg, wu, wd, g.reshape(1, -1), b.reshape(1, -1))


def _block_diag(w):
    nb, n, _ = w.shape
    eye = jnp.eye(nb, dtype=w.dtype)
    return (eye[:, None, :, None] * w[:, :, None, :]).reshape(nb * n, nb * n)


def kernel(x, w_in, w_out, hg_lb_logits, hg_norm_g, lru_conv_w, lru_conv_b, lru_wa, lru_ba, lru_wx, lru_bx, lru_lambda, da_lq1, da_lk1, da_lq2, da_lk2, da_norm_g, ln1_g, ln1_b, ln2_g, ln2_b, ffn_wg, ffn_wu, ffn_wd, router_w, moe_wg, moe_wu, moe_wd):
    batch, seq, _ = x.shape
    depth = w_in.shape[0]
    xf = x.reshape(batch * seq, D_MODEL)
    lb_p = jax.nn.softmax(hg_lb_logits.astype(F32), axis=0)
    lb_all = jnp.cumsum(lb_p, axis=0) - lb_p[0:1]
    for l in range(depth):
        rec, att = _in_proj(xf, w_in[l].astype(BF16))
        o_hg = _hgrn(rec, lb_all[l], hg_norm_g[l], batch, seq)
        o_lru = _lru(rec, lru_conv_w[l], lru_conv_b[l], _block_diag(lru_wa[l]).astype(BF16),
                     _block_diag(lru_wx[l]).astype(BF16), lru_ba[l], lru_bx[l], lru_lambda[l],
                     batch, seq)
        o_da = _attention(att, da_lq1[l], da_lk1[l], da_lq2[l], da_lk2[l], da_norm_g[l], l,
                          batch, seq)
        xf = _out_proj_ln(o_hg, o_lru, o_da, w_out[l].astype(BF16), xf, ln1_g[l], ln1_b[l])
        j = l // 2
        if l % 2 == 0:
            xf = _ffn_ln(xf, ffn_wg[j].astype(BF16), ffn_wu[j].astype(BF16), ffn_wd[j].astype(BF16),
                         ln2_g[l], ln2_b[l])
        else:
            router_pad = jnp.pad(router_w[j], ((0, 0), (0, 128 - N_EXPERTS)))
            xf = _moe_ln(xf, router_pad, moe_wg[j].astype(BF16), moe_wu[j].astype(BF16),
                         moe_wd[j].astype(BF16), ln2_g[l], ln2_b[l])
    return xf.reshape(batch, seq, D_MODEL)
```

```python
import functools
import math

import jax
import jax.numpy as jnp
from jax import lax
from jax.experimental import pallas as pl
from jax.experimental.pallas import tpu as pltpu

D_MODEL = 1024
DEPTH = 2
HG_HEADS = 4
HG_KEY_DIM = 64
HG_WIDTH = 256
HG_CHUNK = 16
LRU_WIDTH = 256
LRU_BLOCKS = 4
CONV_WIDTH = 4
LRU_C = 8.0
DA_WIDTH = 512
DA_HEADS = 4
DA_HEAD_DIM = 64
D_FF = 3584
N_EXPERTS = 8
DN_ALPHA = (2.0 * DEPTH) ** 0.25
EPS = 1e-5
REC_WIDTH = 4 * HG_WIDTH + 2 * LRU_WIDTH
ATT_WIDTH = 3 * DA_WIDTH

V7X_VMEM_BYTES = 64 * 1024 * 1024
VMEM_LIMIT = 48 * 1024 * 1024

BF16 = jnp.bfloat16
F32 = jnp.float32


def _params(*semantics):
    return pltpu.CompilerParams(dimension_semantics=semantics, vmem_limit_bytes=VMEM_LIMIT)


def _layer_norm_rows(y, g, b):
    mu = jnp.mean(y, axis=-1, keepdims=True)
    yc = y - mu
    var = jnp.mean(yc * yc, axis=-1, keepdims=True)
    return yc * lax.rsqrt(var + EPS) * g + b


def _in_proj_kernel(x_ref, w_ref, rec_ref, att_ref):
    xb = x_ref[...].astype(BF16)
    rec_ref[...] = jnp.dot(xb, w_ref[:, :REC_WIDTH], preferred_element_type=F32)
    att_ref[...] = jnp.dot(xb, w_ref[:, REC_WIDTH:], preferred_element_type=F32).astype(BF16)


def _in_proj(x, w_bf16, tm=512):
    T = x.shape[0]
    return pl.pallas_call(
        _in_proj_kernel,
        grid=(T // tm,),
        in_specs=[pl.BlockSpec((tm, D_MODEL), lambda i: (i, 0)),
                  pl.BlockSpec((D_MODEL, REC_WIDTH + ATT_WIDTH), lambda i: (0, 0))],
        out_specs=[pl.BlockSpec((tm, REC_WIDTH), lambda i: (i, 0)),
                   pl.BlockSpec((tm, ATT_WIDTH), lambda i: (i, 0))],
        out_shape=[jax.ShapeDtypeStruct((T, REC_WIDTH), F32),
                   jax.ShapeDtypeStruct((T, ATT_WIDTH), BF16)],
        compiler_params=_params("parallel"),
        name="in_proj",
    )(x, w_bf16)


def _hgrn_kernel(rec_ref, lb_ref, ng_ref, gmat_ref, sel_ref, bmask_ref, o_ref, st_ref, *, rows):
    @pl.when(pl.program_id(1) == 0)
    def _():
        st_ref[...] = jnp.zeros_like(st_ref)

    C = HG_CHUNK
    W = HG_WIDTH
    lb = lb_ref[...]
    ng = ng_ref[...]
    gmat = gmat_ref[...]
    sel = sel_ref[...]
    row = lax.broadcasted_iota(jnp.int32, (C, W), 0)

    def chunk(c, carry):
        r0 = pl.multiple_of(c * C, C)
        qr = rec_ref[pl.ds(r0, C), 0:W]
        z = rec_ref[pl.ds(r0, C), W:2 * W]
        v = rec_ref[pl.ds(r0, C), 2 * W:3 * W]
        g = rec_ref[pl.ds(r0, C), 3 * W:4 * W]
        logf = jnp.log(lb + (1.0 - lb) * jax.nn.sigmoid(z))
        kk = (1.0 - lb) * jax.nn.sigmoid(-z)
        q = qr * jax.nn.sigmoid(qr)
        b = logf
        for s in (1, 2, 4, 8):
            b = b + jnp.where(row >= s, pltpu.roll(b, s, 0), 0.0)
        v_b = v.astype(BF16)
        w_rows = []
        for t in range(C):
            rel = b[t:t + 1, :] - b
            dec = jnp.exp(jnp.where(row <= t, rel, -jnp.inf))
            w_rows.append(((q[t:t + 1, :] * kk) * dec).astype(BF16))
        w2 = jnp.concatenate(w_rows, axis=0)
        a = jnp.dot(w2, gmat, preferred_element_type=F32)
        p = a * jnp.concatenate([v] * C, axis=0)
        o_intra = jnp.dot(sel, p.astype(BF16), preferred_element_type=F32)
        st = st_ref[...]
        qd = (q * jnp.exp(b)).astype(BF16)
        o_inter = lax.dot_general(qd, st.astype(BF16), (((1,), (1,)), ((), ())),
                                  preferred_element_type=F32)
        b_last = b[C - 1:C, :]
        kd = (kk * jnp.exp(b_last - b)).astype(BF16)
        u_t = lax.dot_general(v_b, kd, (((0,), (0,)), ((), ())), preferred_element_type=F32)
        st_ref[...] = st * jnp.exp(b_last) + u_t * bmask_ref[...]
        o = o_intra + o_inter
        ms = jnp.dot(o * o, gmat.astype(F32), preferred_element_type=F32,
                     precision=lax.Precision.HIGHEST) * (1.0 / HG_KEY_DIM)
        out = o * lax.rsqrt(ms + EPS) * ng * (g * jax.nn.sigmoid(g))
        o_ref[pl.ds(r0, C), :] = out.astype(o_ref.dtype)
        return carry

    lax.fori_loop(0, rows // C, chunk, 0)


def _hgrn(rec, lb, norm_g, batch, seq, rows=256):
    T = rec.shape[0]
    nblk = seq // rows
    head = jnp.arange(HG_WIDTH) // HG_KEY_DIM
    same_head = head[:, None] == head[None, :]
    gmat = same_head.astype(BF16)
    bmask = same_head.astype(F32)
    sel = (jnp.arange(HG_CHUNK)[:, None] == (jnp.arange(HG_CHUNK * HG_CHUNK) // HG_CHUNK)[None, :]).astype(BF16)
    const = lambda shape: pl.BlockSpec(shape, lambda b, i: (0, 0))
    return pl.pallas_call(
        functools.partial(_hgrn_kernel, rows=rows),
        grid=(batch, nblk),
        in_specs=[pl.BlockSpec((rows, 4 * HG_WIDTH), lambda b, i: (b * nblk + i, 0)),
                  const((1, HG_WIDTH)), const((1, HG_WIDTH)),
                  const((HG_WIDTH, HG_WIDTH)), const((HG_CHUNK, HG_CHUNK * HG_CHUNK)),
                  const((HG_WIDTH, HG_WIDTH))],
        out_specs=pl.BlockSpec((rows, HG_WIDTH), lambda b, i: (b * nblk + i, 0)),
        out_shape=jax.ShapeDtypeStruct((T, HG_WIDTH), BF16),
        scratch_shapes=[pltpu.VMEM((HG_WIDTH, HG_WIDTH), F32)],
        compiler_params=_params("parallel", "arbitrary"),
        name="hgrn2",
    )(rec, lb.reshape(1, -1), norm_g.reshape(1, -1), gmat, sel, bmask)


def _lru_kernel(rec_ref, cw_ref, cb_ref, wa_ref, wx_ref, ba_ref, bx_ref, lam_ref, o_ref,
                xprev_ref, h_ref, *, rows):
    @pl.when(pl.program_id(1) == 0)
    def _():
        xprev_ref[...] = jnp.zeros_like(xprev_ref)
        h_ref[...] = jnp.zeros_like(h_ref)

    W = LRU_WIDTH
    x = rec_ref[:, 0:W]
    gate = rec_ref[:, W:2 * W]
    row = lax.broadcasted_iota(jnp.int32, (rows, W), 0)
    xp = xprev_ref[...]
    tail = jnp.zeros((rows - 8, W), F32)
    xc = cb_ref[...] + cw_ref[CONV_WIDTH - 1:CONV_WIDTH, :] * x
    for j in range(1, CONV_WIDTH):
        prev = jnp.concatenate([pltpu.roll(xp, j, 0), tail], axis=0)
        xs = jnp.where(row >= j, pltpu.roll(x, j, 0), prev)
        xc = xc + cw_ref[CONV_WIDTH - 1 - j:CONV_WIDTH - j, :] * xs
    xprev_ref[...] = x[rows - 8:rows, :]

    xcb = xc.astype(BF16)
    r = jax.nn.sigmoid(jnp.dot(xcb, wa_ref[...], preferred_element_type=F32) + ba_ref[...])
    i = jax.nn.sigmoid(jnp.dot(xcb, wx_ref[...], preferred_element_type=F32) + bx_ref[...])
    lam = lam_ref[...]
    log_sig = jnp.minimum(lam, 0.0) - jnp.log1p(jnp.exp(-jnp.abs(lam)))
    log_a = LRU_C * r * log_sig
    a = jnp.exp(log_a)
    th = jnp.tanh(log_a)
    one_minus_a2 = -2.0 * th / (1.0 - th)
    u = jnp.sqrt(one_minus_a2) * i * xc

    for s in (1, 2, 4):
        a_s = jnp.where(row >= s, pltpu.roll(a, s, 0), 1.0)
        u_s = jnp.where(row >= s, pltpu.roll(u, s, 0), 0.0)
        u = a * u_s + u
        a = a * a_s
    s = 8
    while s < rows:
        a_s = jnp.concatenate([jnp.ones((s, W), F32), a[:rows - s, :]], axis=0)
        u_s = jnp.concatenate([jnp.zeros((s, W), F32), u[:rows - s, :]], axis=0)
        u = a * u_s + u
        a = a * a_s
        s *= 2
    h = a * h_ref[0:1, :] + u
    h_ref[...] = jnp.broadcast_to(h[rows - 1:rows, :], h_ref.shape)
    o_ref[...] = (h * jax.nn.gelu(gate)).astype(o_ref.dtype)


def _lru(rec, conv_w, conv_b, wa_bd, wx_bd, ba, bx, lam, batch, seq, rows=256):
    T = rec.shape[0]
    nblk = seq // rows
    W = LRU_WIDTH
    const = lambda shape: pl.BlockSpec(shape, lambda b, i: (0, 0))
    return pl.pallas_call(
        functools.partial(_lru_kernel, rows=rows),
        grid=(batch, nblk),
        in_specs=[pl.BlockSpec((rows, 2 * W), lambda b, i: (b * nblk + i, 2)),
                  const((CONV_WIDTH, W)), const((1, W)), const((W, W)), const((W, W)),
                  const((1, W)), const((1, W)), const((1, W))],
        out_specs=pl.BlockSpec((rows, W), lambda b, i: (b * nblk + i, 0)),
        out_shape=jax.ShapeDtypeStruct((T, W), BF16),
        scratch_shapes=[pltpu.VMEM((8, W), F32), pltpu.VMEM((8, W), F32)],
        compiler_params=_params("parallel", "arbitrary"),
        name="rglru",
    )(rec, conv_w, conv_b.reshape(1, -1), wa_bd, wx_bd, ba.reshape(1, -1), bx.reshape(1, -1),
      lam.reshape(1, -1))


def _attn_kernel(q_ref, k_ref, v_ref, lq1_ref, lk1_ref, lq2_ref, lk2_ref, ng_ref, o_ref,
                 vt_ref, m_ref, l_ref, acc_ref, *, tq, tk, lam_init, seq):
    d = DA_HEAD_DIM
    qi = pl.program_id(2)

    @pl.when(qi == 0)
    def _():
        def transpose_values(j, carry):
            r0 = pl.multiple_of(j * tk, tk)
            vt_ref[j] = v_ref[pl.ds(r0, tk), :].astype(F32).T.astype(BF16)
            return carry
        lax.fori_loop(0, seq // tk, transpose_values, 0)

    q = q_ref[...] * (d ** -0.5)
    qs = (q[:, 0:d], q[:, d:2 * d])
    m_ref[...] = jnp.full_like(m_ref, -jnp.inf)
    l_ref[...] = jnp.zeros_like(l_ref)
    acc_ref[...] = jnp.zeros_like(acc_ref)

    def block(j, masked):
        k0 = pl.multiple_of(j * tk, tk)
        kb = k_ref[pl.ds(k0, tk), :]
        vt = vt_ref[j]
        for c in range(2):
            s = lax.dot_general(kb[:, c * d:(c + 1) * d], qs[c], (((1,), (1,)), ((), ())),
                                preferred_element_type=F32)
            if masked:
                kpos = k0 + lax.broadcasted_iota(jnp.int32, (tk, tq), 0)
                qpos = qi * tq + lax.broadcasted_iota(jnp.int32, (tk, tq), 1)
                s = jnp.where(kpos <= qpos, s, -jnp.inf)
            m_old = m_ref[c]
            m_new = jnp.maximum(m_old, jnp.max(s, axis=0, keepdims=True))
            p = jnp.exp(s - m_new)
            alpha = jnp.exp(m_old - m_new)
            l_ref[c] = alpha * l_ref[c] + jnp.sum(p, axis=0, keepdims=True)
            acc_ref[c] = alpha * acc_ref[c] + jnp.dot(vt, p.astype(BF16), preferred_element_type=F32)
            m_ref[c] = m_new

    def full_block(j, carry):
        block(j, False)
        return carry

    kpq = tq // tk
    lax.fori_loop(0, qi * kpq, full_block, 0)
    for jj in range(kpq):
        block(qi * kpq + jj, True)

    lam = (jnp.exp(jnp.sum(lq1_ref[...] * lk1_ref[...], axis=-1, keepdims=True))
           - jnp.exp(jnp.sum(lq2_ref[...] * lk2_ref[...], axis=-1, keepdims=True)) + lam_init)
    o_t = acc_ref[0] * (1.0 / l_ref[0]) - lam * (acc_ref[1] * (1.0 / l_ref[1]))
    ms = jnp.mean(o_t * o_t, axis=0, keepdims=True)
    o_t = o_t * lax.rsqrt(ms + EPS) * ng_ref[...] * (1.0 - lam_init)
    o_ref[...] = o_t.T.astype(o_ref.dtype)


def _attention(att, lq1, lk1, lq2, lk2, norm_g, layer, batch, seq, tq=512, tk=512):
    T = att.shape[0]
    nq = seq // tq
    hw = 2 * DA_HEAD_DIM
    lam_init = 0.8 - 0.6 * math.exp(-0.3 * layer)
    vec = lambda: pl.BlockSpec((1, DA_HEAD_DIM), lambda b, h, i: (0, 0))
    return pl.pallas_call(
        functools.partial(_attn_kernel, tq=tq, tk=tk, lam_init=lam_init, seq=seq),
        grid=(batch, DA_HEADS, nq),
        in_specs=[pl.BlockSpec((tq, hw), lambda b, h, i: (b * nq + i, h)),
                  pl.BlockSpec((seq, hw), lambda b, h, i: (b, DA_HEADS + h)),
                  pl.BlockSpec((seq, hw), lambda b, h, i: (b, 2 * DA_HEADS + h)),
                  vec(), vec(), vec(), vec(),
                  pl.BlockSpec((hw, 1), lambda b, h, i: (h, 0))],
        out_specs=pl.BlockSpec((tq, hw), lambda b, h, i: (b * nq + i, h)),
        out_shape=jax.ShapeDtypeStruct((T, DA_WIDTH), BF16),
        scratch_shapes=[pltpu.VMEM((seq // tk, hw, tk), BF16),
                        pltpu.VMEM((2, 1, tq), F32), pltpu.VMEM((2, 1, tq), F32),
                        pltpu.VMEM((2, hw, tq), F32)],
        compiler_params=_params("arbitrary", "arbitrary", "arbitrary"),
        name="diff_attn",
    )(att, att, att, lq1.reshape(1, -1), lk1.reshape(1, -1), lq2.reshape(1, -1),
      lk2.reshape(1, -1), norm_g.reshape(-1, 1))


def _out_proj_kernel(hg_ref, lru_ref, da_ref, w_ref, x_ref, g_ref, b_ref, o_ref):
    h = jnp.dot(hg_ref[...], w_ref[0:HG_WIDTH, :], preferred_element_type=F32)
    h = h + jnp.dot(lru_ref[...], w_ref[HG_WIDTH:HG_WIDTH + LRU_WIDTH, :], preferred_element_type=F32)
    h = h + jnp.dot(da_ref[...], w_ref[HG_WIDTH + LRU_WIDTH:, :], preferred_element_type=F32)
    o_ref[...] = _layer_norm_rows(DN_ALPHA * x_ref[...] + h, g_ref[...], b_ref[...])


def _out_proj_ln(o_hg, o_lru, o_da, w_bf16, x, g, b, tm=512):
    T = x.shape[0]
    rows = lambda w: pl.BlockSpec((tm, w), lambda i: (i, 0))
    const = lambda shape: pl.BlockSpec(shape, lambda i: (0, 0))
    return pl.pallas_call(
        _out_proj_kernel,
        grid=(T // tm,),
        in_specs=[rows(HG_WIDTH), rows(LRU_WIDTH), rows(DA_WIDTH), const((D_MODEL, D_MODEL)),
                  rows(D_MODEL), const((1, D_MODEL)), const((1, D_MODEL))],
        out_specs=rows(D_MODEL),
        out_shape=jax.ShapeDtypeStruct((T, D_MODEL), F32),
        compiler_params=_params("parallel"),
        name="out_proj_ln",
    )(o_hg, o_lru, o_da, w_bf16, x, g.reshape(1, -1), b.reshape(1, -1))


def _ffn_kernel(x_ref, wg_ref, wu_ref, wd_ref, g_ref, b_ref, o_ref, xb_ref, acc_ref):
    f = pl.program_id(1)

    @pl.when(f == 0)
    def _():
        xb_ref[...] = x_ref[...].astype(BF16)
        acc_ref[...] = jnp.zeros_like(acc_ref)

    xb = xb_ref[...]
    gate = jnp.dot(xb, wg_ref[...], preferred_element_type=F32)
    up = jnp.dot(xb, wu_ref[...], preferred_element_type=F32)
    hmid = (gate * jax.nn.sigmoid(gate) * up).astype(BF16)
    acc_ref[...] += jnp.dot(hmid, wd_ref[...], preferred_element_type=F32)

    @pl.when(f == pl.num_programs(1) - 1)
    def _():
        o_ref[...] = _layer_norm_rows(DN_ALPHA * x_ref[...] + acc_ref[...], g_ref[...], b_ref[...])


def _ffn_ln(x, wg, wu, wd, g, b, tm=1024, tf=512):
    T = x.shape[0]
    return pl.pallas_call(
        _ffn_kernel,
        grid=(T // tm, D_FF // tf),
        in_specs=[pl.BlockSpec((tm, D_MODEL), lambda i, f: (i, 0)),
                  pl.BlockSpec((D_MODEL, tf), lambda i, f: (0, f)),
                  pl.BlockSpec((D_MODEL, tf), lambda i, f: (0, f)),
                  pl.BlockSpec((tf, D_MODEL), lambda i, f: (f, 0)),
                  pl.BlockSpec((1, D_MODEL), lambda i, f: (0, 0)),
                  pl.BlockSpec((1, D_MODEL), lambda i, f: (0, 0))],
        out_specs=pl.BlockSpec((tm, D_MODEL), lambda i, f: (i, 0)),
        out_shape=jax.ShapeDtypeStruct((T, D_MODEL), F32),
        scratch_shapes=[pltpu.VMEM((tm, D_MODEL), BF16), pltpu.VMEM((tm, D_MODEL), F32)],
        compiler_params=_params("parallel", "arbitrary"),
        name="ffn_ln",
    )(x, wg, wu, wd, g.reshape(1, -1), b.reshape(1, -1))


def _moe_kernel(x_ref, rw_ref, wg_ref, wu_ref, wd_ref, g_ref, b_ref, o_ref, xb_ref, comb_ref, acc_ref):
    e = pl.program_id(1)
    f = pl.program_id(2)

    @pl.when((e == 0) & (f == 0))
    def _():
        x = x_ref[...]
        xb_ref[...] = x.astype(BF16)
        acc_ref[...] = jnp.zeros_like(acc_ref)
        logits = jnp.dot(x, rw_ref[...], preferred_element_type=F32, precision=lax.Precision.HIGHEST)
        lane = lax.broadcasted_iota(jnp.int32, logits.shape, 1)
        valid = lane < N_EXPERTS
        neg = -jnp.inf
        l1 = jnp.where(valid, logits, neg)
        v1 = jnp.max(l1, axis=-1, keepdims=True)
        i1 = jnp.min(jnp.where(l1 == v1, lane, logits.shape[1]), axis=-1, keepdims=True)
        l2 = jnp.where(lane == i1, neg, l1)
        v2 = jnp.max(l2, axis=-1, keepdims=True)
        i2 = jnp.min(jnp.where(l2 == v2, lane, logits.shape[1]), axis=-1, keepdims=True)
        e2 = jnp.exp(v2 - v1)
        g1 = 1.0 / (1.0 + e2)
        g2 = e2 / (1.0 + e2)
        comb_ref[...] = jnp.where(lane == i1, g1, 0.0) + jnp.where(lane == i2, g2, 0.0)

    lane = lax.broadcasted_iota(jnp.int32, comb_ref.shape, 1)
    ce = jnp.sum(jnp.where(lane == e, comb_ref[...], 0.0), axis=-1, keepdims=True)
    xb = xb_ref[...]
    gate = jnp.dot(xb, wg_ref[0], preferred_element_type=F32)
    up = jnp.dot(xb, wu_ref[0], preferred_element_type=F32)
    hmid = (gate * jax.nn.sigmoid(gate) * up * ce).astype(BF16)
    acc_ref[...] += jnp.dot(hmid, wd_ref[0], preferred_element_type=F32)

    @pl.when((e == pl.num_programs(1) - 1) & (f == pl.num_programs(2) - 1))
    def _():
        o_ref[...] = _layer_norm_rows(DN_ALPHA * x_ref[...] + acc_ref[...], g_ref[...], b_ref[...])


def _moe_ln(x, router_pad, wg, wu, wd, g, b, tm=1024, tf=512):
    T = x.shape[0]
    lanes = router_pad.shape[1]
    return pl.pallas_call(
        _moe_kernel,
        grid=(T // tm, N_EXPERTS, D_FF // tf),
        in_specs=[pl.BlockSpec((tm, D_MODEL), lambda i, e, f: (i, 0)),
                  pl.BlockSpec((D_MODEL, lanes), lambda i, e, f: (0, 0)),
                  pl.BlockSpec((1, D_MODEL, tf), lambda i, e, f: (e, 0, f)),
                  pl.BlockSpec((1, D_MODEL, tf), lambda i, e, f: (e, 0, f)),
                  pl.BlockSpec((1, tf, D_MODEL), lambda i, e, f: (e, f, 0)),
                  pl.BlockSpec((1, D_MODEL), lambda i, e, f: (0, 0)),
                  pl.BlockSpec((1, D_MODEL), lambda i, e, f: (0, 0))],
        out_specs=pl.BlockSpec((tm, D_MODEL), lambda i, e, f: (i, 0)),
        out_shape=jax.ShapeDtypeStruct((T, D_MODEL), F32),
        scratch_shapes=[pltpu.VMEM((tm, D_MODEL), BF16), pltpu.VMEM((tm, lanes), F32),
                        pltpu.VMEM((tm, D_MODEL), F32)],
        compiler_params=_params("parallel", "arbitrary", "arbitrary"),
        name="moe_ln",
    )(x, router_pad, wg, wu, wd, g.reshape(1, -1), b.reshape(1, -1))


def _block_diag(w):
    nb, n, _ = w.shape
    eye = jnp.eye(nb, dtype=w.dtype)
    return (eye[:, None, :, None] * w[:, :, None, :]).reshape(nb * n, nb * n)


def kernel(x, w_in, w_out, hg_lb_logits, hg_norm_g, lru_conv_w, lru_conv_b, lru_wa, lru_ba, lru_wx, lru_bx, lru_lambda, da_lq1, da_lk1, da_lq2, da_lk2, da_norm_g, ln1_g, ln1_b, ln2_g, ln2_b, ffn_wg, ffn_wu, ffn_wd, router_w, moe_wg, moe_wu, moe_wd):
    batch, seq, _ = x.shape
    depth = w_in.shape[0]
    xf = x.reshape(batch * seq, D_MODEL)
    lb_p = jax.nn.softmax(hg_lb_logits.astype(F32), axis=0)
    lb_all = jnp.cumsum(lb_p, axis=0) - lb_p[0:1]
    for l in range(depth):
        rec, att = _in_proj(xf, w_in[l].astype(BF16))
        o_hg = _hgrn(rec, lb_all[l], hg_norm_g[l], batch, seq)
        o_lru = _lru(rec, lru_conv_w[l], lru_conv_b[l], _block_diag(lru_wa[l]).astype(BF16),
                     _block_diag(lru_wx[l]).astype(BF16), lru_ba[l], lru_bx[l], lru_lambda[l],
                     batch, seq)
        o_da = _attention(att, da_lq1[l], da_lk1[l], da_lq2[l], da_lk2[l], da_norm_g[l], l,
                          batch, seq)
        xf = _out_proj_ln(o_hg, o_lru, o_da, w_out[l].astype(BF16), xf, ln1_g[l], ln1_b[l])
        j = l // 2
        if l % 2 == 0:
            xf = _ffn_ln(xf, ffn_wg[j].astype(BF16), ffn_wu[j].astype(BF16), ffn_wd[j].astype(BF16),
                         ln2_g[l], ln2_b[l])
        else:
            router_pad = jnp.pad(router_w[j], ((0, 0), (0, 128 - N_EXPERTS)))
            xf = _moe_ln(xf, router_pad, moe_wg[j].astype(BF16), moe_wu[j].astype(BF16),
                         moe_wd[j].astype(BF16), ln2_g[l], ln2_b[l])
    return xf.reshape(batch, seq, D_MODEL)
```

```python
import functools
import math

import jax
import jax.numpy as jnp
from jax import lax
from jax.experimental import pallas as pl
from jax.experimental.pallas import tpu as pltpu

D_MODEL = 1024
DEPTH = 2
HG_HEADS = 4
HG_KEY_DIM = 64
HG_WIDTH = 256
HG_CHUNK = 16
LRU_WIDTH = 256
LRU_BLOCKS = 4
CONV_WIDTH = 4
LRU_C = 8.0
DA_WIDTH = 512
DA_HEADS = 4
DA_HEAD_DIM = 64
D_FF = 3584
N_EXPERTS = 8
DN_ALPHA = (2.0 * DEPTH) ** 0.25
EPS = 1e-5
REC_WIDTH = 4 * HG_WIDTH + 2 * LRU_WIDTH
ATT_WIDTH = 3 * DA_WIDTH

V7X_VMEM_BYTES = 64 * 1024 * 1024
VMEM_LIMIT = 48 * 1024 * 1024

BF16 = jnp.bfloat16
F32 = jnp.float32


def _params(*semantics):
    return pltpu.CompilerParams(dimension_semantics=semantics, vmem_limit_bytes=VMEM_LIMIT)


def _layer_norm_rows(y, g, b):
    mu = jnp.mean(y, axis=-1, keepdims=True)
    yc = y - mu
    var = jnp.mean(yc * yc, axis=-1, keepdims=True)
    return yc * lax.rsqrt(var + EPS) * g + b


def _in_proj_kernel(x_ref, w_ref, rec_ref, att_ref):
    xb = x_ref[...].astype(BF16)
    rec_ref[...] = jnp.dot(xb, w_ref[:, :REC_WIDTH], preferred_element_type=F32)
    att_ref[...] = jnp.dot(xb, w_ref[:, REC_WIDTH:], preferred_element_type=F32).astype(BF16)


def _in_proj(x, w_bf16, tm=512):
    T = x.shape[0]
    return pl.pallas_call(
        _in_proj_kernel,
        grid=(T // tm,),
        in_specs=[pl.BlockSpec((tm, D_MODEL), lambda i: (i, 0)),
                  pl.BlockSpec((D_MODEL, REC_WIDTH + ATT_WIDTH), lambda i: (0, 0))],
        out_specs=[pl.BlockSpec((tm, REC_WIDTH), lambda i: (i, 0)),
                   pl.BlockSpec((tm, ATT_WIDTH), lambda i: (i, 0))],
        out_shape=[jax.ShapeDtypeStruct((T, REC_WIDTH), F32),
                   jax.ShapeDtypeStruct((T, ATT_WIDTH), BF16)],
        compiler_params=_params("parallel"),
        name="in_proj",
    )(x, w_bf16)


def _hgrn_kernel(rec_ref, lb_ref, ng_ref, gmat_ref, sel_ref, bmask_ref, o_ref, st_ref, *, rows):
    @pl.when(pl.program_id(1) == 0)
    def _():
        st_ref[...] = jnp.zeros_like(st_ref)

    C = HG_CHUNK
    W = HG_WIDTH
    lb = lb_ref[...]
    ng = ng_ref[...]
    gmat = gmat_ref[...]
    sel = sel_ref[...]
    row = lax.broadcasted_iota(jnp.int32, (C, W), 0)

    def chunk(c, carry):
        r0 = pl.multiple_of(c * C, C)
        qr = rec_ref[pl.ds(r0, C), 0:W]
        z = rec_ref[pl.ds(r0, C), W:2 * W]
        v = rec_ref[pl.ds(r0, C), 2 * W:3 * W]
        g = rec_ref[pl.ds(r0, C), 3 * W:4 * W]
        logf = jnp.log(lb + (1.0 - lb) * jax.nn.sigmoid(z))
        kk = (1.0 - lb) * jax.nn.sigmoid(-z)
        q = qr * jax.nn.sigmoid(qr)
        b = logf
        for s in (1, 2, 4, 8):
            b = b + jnp.where(row >= s, pltpu.roll(b, s, 0), 0.0)
        v_b = v.astype(BF16)
        w_rows = []
        for t in range(C):
            rel = b[t:t + 1, :] - b
            dec = jnp.exp(jnp.where(row <= t, rel, -jnp.inf))
            w_rows.append(((q[t:t + 1, :] * kk) * dec).astype(BF16))
        w2 = jnp.concatenate(w_rows, axis=0)
        a = jnp.dot(w2, gmat, preferred_element_type=F32)
        p = a * jnp.concatenate([v] * C, axis=0)
        o_intra = jnp.dot(sel, p.astype(BF16), preferred_element_type=F32)
        st = st_ref[...]
        qd = (q * jnp.exp(b)).astype(BF16)
        o_inter = lax.dot_general(qd, st.astype(BF16), (((1,), (1,)), ((), ())),
                                  preferred_element_type=F32)
        b_last = b[C - 1:C, :]
        kd = (kk * jnp.exp(b_last - b)).astype(BF16)
        u_t = lax.dot_general(v_b, kd, (((0,), (0,)), ((), ())), preferred_element_type=F32)
        st_ref[...] = st * jnp.exp(b_last) + u_t * bmask_ref[...]
        o = o_intra + o_inter
        ms = jnp.dot(o * o, gmat.astype(F32), preferred_element_type=F32,
                     precision=lax.Precision.HIGHEST) * (1.0 / HG_KEY_DIM)
        out = o * lax.rsqrt(ms + EPS) * ng * (g * jax.nn.sigmoid(g))
        o_ref[pl.ds(r0, C), :] = out.astype(o_ref.dtype)
        return carry

    lax.fori_loop(0, rows // C, chunk, 0)


def _hgrn(rec, lb, norm_g, batch, seq, rows=256):
    T = rec.shape[0]
    nblk = seq // rows
    head = jnp.arange(HG_WIDTH) // HG_KEY_DIM
    same_head = head[:, None] == head[None, :]
    gmat = same_head.astype(BF16)
    bmask = same_head.astype(F32)
    sel = (jnp.arange(HG_CHUNK)[:, None] == (jnp.arange(HG_CHUNK * HG_CHUNK) // HG_CHUNK)[None, :]).astype(BF16)
    const = lambda shape: pl.BlockSpec(shape, lambda b, i: (0, 0))
    return pl.pallas_call(
        functools.partial(_hgrn_kernel, rows=rows),
        grid=(batch, nblk),
        in_specs=[pl.BlockSpec((rows, 4 * HG_WIDTH), lambda b, i: (b * nblk + i, 0)),
                  const((1, HG_WIDTH)), const((1, HG_WIDTH)),
                  const((HG_WIDTH, HG_WIDTH)), const((HG_CHUNK, HG_CHUNK * HG_CHUNK)),
                  const((HG_WIDTH, HG_WIDTH))],
        out_specs=pl.BlockSpec((rows, HG_WIDTH), lambda b, i: (b * nblk + i, 0)),
        out_shape=jax.ShapeDtypeStruct((T, HG_WIDTH), BF16),
        scratch_shapes=[pltpu.VMEM((HG_WIDTH, HG_WIDTH), F32)],
        compiler_params=_params("parallel", "arbitrary"),
        name="hgrn2",
    )(rec, lb.reshape(1, -1), norm_g.reshape(1, -1), gmat, sel, bmask)


def _lru_kernel(rec_ref, cw_ref, cb_ref, wa_ref, wx_ref, ba_ref, bx_ref, lam_ref, o_ref,
                xprev_ref, h_ref, *, rows):
    @pl.when(pl.program_id(1) == 0)
    def _():
        xprev_ref[...] = jnp.zeros_like(xprev_ref)
        h_ref[...] = jnp.zeros_like(h_ref)

    W = LRU_WIDTH
    x = rec_ref[:, 0:W]
    gate = rec_ref[:, W:2 * W]
    row = lax.broadcasted_iota(jnp.int32, (rows, W), 0)
    xp = xprev_ref[...]
    tail = jnp.zeros((rows - 8, W), F32)
    xc = cb_ref[...] + cw_ref[CONV_WIDTH - 1:CONV_WIDTH, :] * x
    for j in range(1, CONV_WIDTH):
        prev = jnp.concatenate([pltpu.roll(xp, j, 0), tail], axis=0)
        xs = jnp.where(row >= j, pltpu.roll(x, j, 0), prev)
        xc = xc + cw_ref[CONV_WIDTH - 1 - j:CONV_WIDTH - j, :] * xs
    xprev_ref[...] = x[rows - 8:rows, :]

    xcb = xc.astype(BF16)
    r = jax.nn.sigmoid(jnp.dot(xcb, wa_ref[...], preferred_element_type=F32) + ba_ref[...])
    i = jax.nn.sigmoid(jnp.dot(xcb, wx_ref[...], preferred_element_type=F32) + bx_ref[...])
    lam = lam_ref[...]
    log_sig = jnp.minimum(lam, 0.0) - jnp.log1p(jnp.exp(-jnp.abs(lam)))
    log_a = LRU_C * r * log_sig
    a = jnp.exp(log_a)
    th = jnp.tanh(log_a)
    one_minus_a2 = -2.0 * th / (1.0 - th)
    u = jnp.sqrt(one_minus_a2) * i * xc

    for s in (1, 2, 4):
        a_s = jnp.where(row >= s, pltpu.roll(a, s, 0), 1.0)
        u_s = jnp.where(row >= s, pltpu.roll(u, s, 0), 0.0)
        u = a * u_s + u
        a = a * a_s
    s = 8
    while s < rows:
        a_s = jnp.concatenate([jnp.ones((s, W), F32), a[:rows - s, :]], axis=0)
        u_s = jnp.concatenate([jnp.zeros((s, W), F32), u[:rows - s, :]], axis=0)
        u = a * u_s + u
        a = a * a_s
        s *= 2
    h = a * h_ref[0:1, :] + u
    h_ref[...] = jnp.broadcast_to(h[rows - 1:rows, :], h_ref.shape)
    o_ref[...] = (h * jax.nn.gelu(gate)).astype(o_ref.dtype)


def _lru(rec, conv_w, conv_b, wa_bd, wx_bd, ba, bx, lam, batch, seq, rows=256):
    T = rec.shape[0]
    nblk = seq // rows
    W = LRU_WIDTH
    const = lambda shape: pl.BlockSpec(shape, lambda b, i: (0, 0))
    return pl.pallas_call(
        functools.partial(_lru_kernel, rows=rows),
        grid=(batch, nblk),
        in_specs=[pl.BlockSpec((rows, 2 * W), lambda b, i: (b * nblk + i, 2)),
                  const((CONV_WIDTH, W)), const((1, W)), const((W, W)), const((W, W)),
                  const((1, W)), const((1, W)), const((1, W))],
        out_specs=pl.BlockSpec((rows, W), lambda b, i: (b * nblk + i, 0)),
        out_shape=jax.ShapeDtypeStruct((T, W), BF16),
        scratch_shapes=[pltpu.VMEM((8, W), F32), pltpu.VMEM((8, W), F32)],
        compiler_params=_params("parallel", "arbitrary"),
        name="rglru",
    )(rec, conv_w, conv_b.reshape(1, -1), wa_bd, wx_bd, ba.reshape(1, -1), bx.reshape(1, -1),
      lam.reshape(1, -1))


def _attn_kernel(q_ref, k_ref, v_ref, lq1_ref, lk1_ref, lq2_ref, lk2_ref, ng_ref, o_ref,
                 vt_ref, m_ref, l_ref, acc_ref, *, tq, tk, lam_init, seq):
    d = DA_HEAD_DIM
    qi = pl.program_id(2)

    @pl.when(qi == 0)
    def _():
        def transpose_values(j, carry):
            r0 = pl.multiple_of(j * tk, tk)
            vt_ref[j] = v_ref[pl.ds(r0, tk), :].astype(F32).T.astype(BF16)
            return carry
        lax.fori_loop(0, seq // tk, transpose_values, 0)

    q = q_ref[...] * (d ** -0.5)
    qs = (q[:, 0:d], q[:, d:2 * d])
    m_ref[...] = jnp.full_like(m_ref, -jnp.inf)
    l_ref[...] = jnp.zeros_like(l_ref)
    acc_ref[...] = jnp.zeros_like(acc_ref)

    def block(j, masked):
        k0 = pl.multiple_of(j * tk, tk)
        kb = k_ref[pl.ds(k0, tk), :]
        vt = vt_ref[j]
        for c in range(2):
            s = lax.dot_general(kb[:, c * d:(c + 1) * d], qs[c], (((1,), (1,)), ((), ())),
                                preferred_element_type=F32)
            if masked:
                kpos = k0 + lax.broadcasted_iota(jnp.int32, (tk, tq), 0)
                qpos = qi * tq + lax.broadcasted_iota(jnp.int32, (tk, tq), 1)
                s = jnp.where(kpos <= qpos, s, -jnp.inf)
            m_old = m_ref[c]
            m_new = jnp.maximum(m_old, jnp.max(s, axis=0, keepdims=True))
            p = jnp.exp(s - m_new)
            alpha = jnp.exp(m_old - m_new)
            l_ref[c] = alpha * l_ref[c] + jnp.sum(p, axis=0, keepdims=True)
            acc_ref[c] = alpha * acc_ref[c] + jnp.dot(vt, p.astype(BF16), preferred_element_type=F32)
            m_ref[c] = m_new

    def full_block(j, carry):
        block(j, False)
        return carry

    kpq = tq // tk
    lax.fori_loop(0, qi * kpq, full_block, 0)
    for jj in range(kpq):
        block(qi * kpq + jj, True)

    lam = (jnp.exp(jnp.sum(lq1_ref[...] * lk1_ref[...], axis=-1, keepdims=True))
           - jnp.exp(jnp.sum(lq2_ref[...] * lk2_ref[...], axis=-1, keepdims=True)) + lam_init)
    o_t = acc_ref[0] * (1.0 / l_ref[0]) - lam * (acc_ref[1] * (1.0 / l_ref[1]))
    ms = jnp.mean(o_t * o_t, axis=0, keepdims=True)
    o_t = o_t * lax.rsqrt(ms + EPS) * ng_ref[...] * (1.0 - lam_init)
    o_ref[...] = o_t.T.astype(o_ref.dtype)


def _attention(att, lq1, lk1, lq2, lk2, norm_g, layer, batch, seq, tq=512, tk=512):
    T = att.shape[0]
    nq = seq // tq
    hw = 2 * DA_HEAD_DIM
    lam_init = 0.8 - 0.6 * math.exp(-0.3 * layer)
    vec = lambda: pl.BlockSpec((1, DA_HEAD_DIM), lambda b, h, i: (0, 0))
    return pl.pallas_call(
        functools.partial(_attn_kernel, tq=tq, tk=tk, lam_init=lam_init, seq=seq),
        grid=(batch, DA_HEADS, nq),
        in_specs=[pl.BlockSpec((tq, hw), lambda b, h, i: (b * nq + i, h)),
                  pl.BlockSpec((seq, hw), lambda b, h, i: (b, DA_HEADS + h)),
                  pl.BlockSpec((seq, hw), lambda b, h, i: (b, 2 * DA_HEADS + h)),
                  vec(), vec(), vec(), vec(),
                  pl.BlockSpec((hw, 1), lambda b, h, i: (h, 0))],
        out_specs=pl.BlockSpec((tq, hw), lambda b, h, i: (b * nq + i, h)),
        out_shape=jax.ShapeDtypeStruct((T, DA_WIDTH), BF16),
        scratch_shapes=[pltpu.VMEM((seq // tk, hw, tk), BF16),
                        pltpu.VMEM((2, 1, tq), F32), pltpu.VMEM((2, 1, tq), F32),
                        pltpu.VMEM((2, hw, tq), F32)],
        compiler_params=_params("arbitrary", "arbitrary", "arbitrary"),
        name="diff_attn",
    )(att, att, att, lq1.reshape(1, -1), lk1.reshape(1, -1), lq2.reshape(1, -1),
      lk2.reshape(1, -1), norm_g.reshape(-1, 1))


def _out_proj_kernel(hg_ref, lru_ref, da_ref, w_ref, x_ref, g_ref, b_ref, o_ref):
    h = jnp.dot(hg_ref[...], w_ref[0:HG_WIDTH, :], preferred_element_type=F32)
    h = h + jnp.dot(lru_ref[...], w_ref[HG_WIDTH:HG_WIDTH + LRU_WIDTH, :], preferred_element_type=F32)
    h = h + jnp.dot(da_ref[...], w_ref[HG_WIDTH + LRU_WIDTH:, :], preferred_element_type=F32)
    o_ref[...] = _layer_norm_rows(DN_ALPHA * x_ref[...] + h, g_ref[...], b_ref[...])


def _out_proj_ln(o_hg, o_lru, o_da, w_bf16, x, g, b, tm=512):
    T = x.shape[0]
    rows = lambda w: pl.BlockSpec((tm, w), lambda i: (i, 0))
    const = lambda shape: pl.BlockSpec(shape, lambda i: (0, 0))
    return pl.pallas_call(
        _out_proj_kernel,
        grid=(T // tm,),
        in_specs=[rows(HG_WIDTH), rows(LRU_WIDTH), rows(DA_WIDTH), const((D_MODEL, D_MODEL)),
                  rows(D_MODEL), const((1, D_MODEL)), const((1, D_MODEL))],
        out_specs=rows(D_MODEL),
        out_shape=jax.ShapeDtypeStruct((T, D_MODEL), F32),
        compiler_params=_params("parallel"),
        name="out_proj_ln",
    )(o_hg, o_lru, o_da, w_bf16, x, g.reshape(1, -1), b.reshape(1, -1))


def _ffn_kernel(x_ref, wg_ref, wu_ref, wd_ref, g_ref, b_ref, o_ref, xb_ref, acc_ref):
    f = pl.program_id(1)

    @pl.when(f == 0)
    def _():
        xb_ref[...] = x_ref[...].astype(BF16)
        acc_ref[...] = jnp.zeros_like(acc_ref)

    xb = xb_ref[...]
    gate = jnp.dot(xb, wg_ref[...], preferred_element_type=F32)
    up = jnp.dot(xb, wu_ref[...], preferred_element_type=F32)
    hmid = (gate * jax.nn.sigmoid(gate) * up).astype(BF16)
    acc_ref[...] += jnp.dot(hmid, wd_ref[...], preferred_element_type=F32)

    @pl.when(f == pl.num_programs(1) - 1)
    def _():
        o_ref[...] = _layer_norm_rows(DN_ALPHA * x_ref[...] + acc_ref[...], g_ref[...], b_ref[...])


def _ffn_ln(x, wg, wu, wd, g, b, tm=1024, tf=512):
    T = x.shape[0]
    return pl.pallas_call(
        _ffn_kernel,
        grid=(T // tm, D_FF // tf),
        in_specs=[pl.BlockSpec((tm, D_MODEL), lambda i, f: (i, 0)),
                  pl.BlockSpec((D_MODEL, tf), lambda i, f: (0, f)),
                  pl.BlockSpec((D_MODEL, tf), lambda i, f: (0, f)),
                  pl.BlockSpec((tf, D_MODEL), lambda i, f: (f, 0)),
                  pl.BlockSpec((1, D_MODEL), lambda i, f: (0, 0)),
                  pl.BlockSpec((1, D_MODEL), lambda i, f: (0, 0))],
        out_specs=pl.BlockSpec((tm, D_MODEL), lambda i, f: (i, 0)),
        out_shape=jax.ShapeDtypeStruct((T, D_MODEL), F32),
        scratch_shapes=[pltpu.VMEM((tm, D_MODEL), BF16), pltpu.VMEM((tm, D_MODEL), F32)],
        compiler_params=_params("parallel", "arbitrary"),
        name="ffn_ln",
    )(x, wg, wu, wd, g.reshape(1, -1), b.reshape(1, -1))


ROUTER_LANES = 128
META_I1, META_I2, META_R1, META_R2 = 0, 1, 2, 3


def _router_kernel(x_ref, rw_ref, tri_ref, meta_ref, gate_ref, cnt_ref, carry_ref):
    @pl.when(pl.program_id(0) == 0)
    def _():
        carry_ref[...] = jnp.zeros_like(carry_ref)

    logits = jnp.dot(x_ref[...], rw_ref[...], preferred_element_type=F32,
                     precision=lax.Precision.HIGHEST)
    lane = lax.broadcasted_iota(jnp.int32, logits.shape, 1)
    neg = -jnp.inf
    l1 = jnp.where(lane < N_EXPERTS, logits, neg)
    v1 = jnp.max(l1, axis=-1, keepdims=True)
    i1 = jnp.min(jnp.where(l1 == v1, lane, ROUTER_LANES), axis=-1, keepdims=True)
    l2 = jnp.where(lane == i1, neg, l1)
    v2 = jnp.max(l2, axis=-1, keepdims=True)
    i2 = jnp.min(jnp.where(l2 == v2, lane, ROUTER_LANES), axis=-1, keepdims=True)
    e2 = jnp.exp(v2 - v1)
    g1 = 1.0 / (1.0 + e2)
    g2 = e2 / (1.0 + e2)
    member = jnp.where((lane == i1) | (lane == i2), 1.0, 0.0)
    rank = jnp.dot(tri_ref[...], member.astype(BF16), preferred_element_type=F32) + carry_ref[0:1, :]
    r1 = jnp.sum(jnp.where(lane == i1, rank, 0.0), axis=-1, keepdims=True).astype(jnp.int32)
    r2 = jnp.sum(jnp.where(lane == i2, rank, 0.0), axis=-1, keepdims=True).astype(jnp.int32)
    meta = jnp.where(lane == META_I1, i1, 0) + jnp.where(lane == META_I2, i2, 0)
    meta = meta + jnp.where(lane == META_R1, r1, 0) + jnp.where(lane == META_R2, r2, 0)
    meta_ref[...] = meta
    gate_ref[...] = jnp.where(lane == 0, g1, 0.0) + jnp.where(lane == 1, g2, 0.0)
    carry = carry_ref[...] + jnp.sum(member, axis=0, keepdims=True)
    carry_ref[...] = carry
    cnt_ref[...] = carry


def _router(x, router_pad, tm=1024):
    T = x.shape[0]
    tri = (jnp.arange(tm)[:, None] > jnp.arange(tm)[None, :]).astype(BF16)
    return pl.pallas_call(
        _router_kernel,
        grid=(T // tm,),
        in_specs=[pl.BlockSpec((tm, D_MODEL), lambda i: (i, 0)),
                  pl.BlockSpec((D_MODEL, ROUTER_LANES), lambda i: (0, 0)),
                  pl.BlockSpec((tm, tm), lambda i: (0, 0))],
        out_specs=[pl.BlockSpec((tm, ROUTER_LANES), lambda i: (i, 0)),
                   pl.BlockSpec((tm, ROUTER_LANES), lambda i: (i, 0)),
                   pl.BlockSpec((8, ROUTER_LANES), lambda i: (0, 0))],
        out_shape=[jax.ShapeDtypeStruct((T, ROUTER_LANES), jnp.int32),
                   jax.ShapeDtypeStruct((T, ROUTER_LANES), F32),
                   jax.ShapeDtypeStruct((8, ROUTER_LANES), F32)],
        scratch_shapes=[pltpu.VMEM((8, ROUTER_LANES), F32)],
        compiler_params=_params("arbitrary"),
        name="moe_router",
    )(x, router_pad, tri)


def _dispatch_tables(meta, gates, cnt, tm):
    T = meta.shape[0]
    counts = cnt[0, :N_EXPERTS].astype(jnp.int32)
    padded = ((counts + tm - 1) // tm) * tm
    ends = jnp.cumsum(padded)
    off = ends - padded
    pos1 = off[meta[:, META_I1]] + meta[:, META_R1]
    pos2 = off[meta[:, META_I2]] + meta[:, META_R2]
    n_rows = 2 * T + N_EXPERTS * tm
    tok = jnp.arange(T, dtype=jnp.int32)
    token_of = jnp.zeros((n_rows,), jnp.int32).at[pos1].set(tok).at[pos2].set(tok)
    gate_of = jnp.zeros((n_rows,), F32).at[pos1].set(gates[:, 0]).at[pos2].set(gates[:, 1])
    n_tiles = n_rows // tm
    tile_start = jnp.arange(n_tiles, dtype=jnp.int32) * tm
    tile_expert = jnp.minimum(jnp.sum((tile_start[:, None] >= ends[None, :]).astype(jnp.int32), axis=1),
                              N_EXPERTS - 1)
    n_used = (ends[-1] // tm).astype(jnp.int32).reshape(1)
    return pos1, pos2, token_of, gate_of, tile_expert, n_used


def _expert_kernel(te_ref, nu_ref, tok_ref, x_hbm, gate_ref, wg_ref, wu_ref, wd_ref, o_ref,
                   xbuf_ref, xb_ref, acc_ref, sem, *, tm):
    i = pl.program_id(0)
    f = pl.program_id(1)
    used = i < nu_ref[0]

    def row_copy(r):
        return pltpu.make_async_copy(x_hbm.at[pl.ds(tok_ref[0, 0, r], 1)],
                                     xbuf_ref.at[pl.ds(r, 1)], sem)

    @pl.when(used & (f == 0))
    def _():
        def issue(r, carry):
            row_copy(r).start()
            return carry

        def wait(r, carry):
            row_copy(r).wait()
            return carry

        lax.fori_loop(0, tm, issue, 0, unroll=8)
        lax.fori_loop(0, tm, wait, 0, unroll=8)
        xb_ref[...] = xbuf_ref[...].astype(BF16)
        acc_ref[...] = jnp.zeros_like(acc_ref)

    @pl.when(used)
    def _():
        xb = xb_ref[...]
        gate = jnp.dot(xb, wg_ref[0], preferred_element_type=F32)
        up = jnp.dot(xb, wu_ref[0], preferred_element_type=F32)
        hmid = (gate * jax.nn.sigmoid(gate) * up).astype(BF16)
        acc_ref[...] += jnp.dot(hmid, wd_ref[0], preferred_element_type=F32)

    last = f == pl.num_programs(1) - 1

    @pl.when(used & last)
    def _():
        o_ref[...] = acc_ref[...] * gate_ref[...]

    @pl.when(jnp.logical_not(used) & last)
    def _():
        o_ref[...] = jnp.zeros_like(o_ref)


def _experts(x, token_of, gate_of, tile_expert, n_used, wg, wu, wd, tm, tf=512):
    n_rows = token_of.shape[0]
    n_tiles = n_rows // tm
    nf = D_FF // tf
    fidx = lambda i, f, nu: jnp.where(i < nu[0], f, nf - 1)
    grid_spec = pltpu.PrefetchScalarGridSpec(
        num_scalar_prefetch=2,
        grid=(n_tiles, nf),
        in_specs=[pl.BlockSpec((1, 1, tm), lambda i, f, te, nu: (i, 0, 0), memory_space=pltpu.SMEM),
                  pl.BlockSpec(memory_space=pl.ANY),
                  pl.BlockSpec((tm, 1), lambda i, f, te, nu: (i, 0)),
                  pl.BlockSpec((1, D_MODEL, tf), lambda i, f, te, nu: (te[i], 0, fidx(i, f, nu))),
                  pl.BlockSpec((1, D_MODEL, tf), lambda i, f, te, nu: (te[i], 0, fidx(i, f, nu))),
                  pl.BlockSpec((1, tf, D_MODEL), lambda i, f, te, nu: (te[i], fidx(i, f, nu), 0))],
        out_specs=pl.BlockSpec((tm, D_MODEL), lambda i, f, te, nu: (i, 0)),
        scratch_shapes=[pltpu.VMEM((tm, D_MODEL), F32), pltpu.VMEM((tm, D_MODEL), BF16),
                        pltpu.VMEM((tm, D_MODEL), F32), pltpu.SemaphoreType.DMA(())],
    )
    return pl.pallas_call(
        functools.partial(_expert_kernel, tm=tm),
        grid_spec=grid_spec,
        out_shape=jax.ShapeDtypeStruct((n_rows, D_MODEL), F32),
        compiler_params=_params("arbitrary", "arbitrary"),
        name="moe_experts",
    )(tile_expert, n_used, token_of.reshape(n_tiles, 1, tm), x, gate_of.reshape(n_rows, 1),
      wg, wu, wd)


def _combine_kernel(p1_ref, p2_ref, x_ref, ys_hbm, g_ref, b_ref, o_ref, buf_ref, sem, *, tm):
    def row_copy(slot, p_ref, r):
        return pltpu.make_async_copy(ys_hbm.at[pl.ds(p_ref[0, 0, r], 1)],
                                     buf_ref.at[slot, pl.ds(r, 1)], sem.at[slot])

    def issue(r, carry):
        row_copy(0, p1_ref, r).start()
        row_copy(1, p2_ref, r).start()
        return carry

    def wait(r, carry):
        row_copy(0, p1_ref, r).wait()
        row_copy(1, p2_ref, r).wait()
        return carry

    lax.fori_loop(0, tm, issue, 0, unroll=8)
    lax.fori_loop(0, tm, wait, 0, unroll=8)
    y = buf_ref[0] + buf_ref[1]
    o_ref[...] = _layer_norm_rows(DN_ALPHA * x_ref[...] + y, g_ref[...], b_ref[...])


def _combine_ln(x, ys, pos1, pos2, g, b, tm=512):
    T = x.shape[0]
    nt = T // tm
    smem_rows = lambda: pl.BlockSpec((1, 1, tm), lambda i: (i, 0, 0), memory_space=pltpu.SMEM)
    return pl.pallas_call(
        functools.partial(_combine_kernel, tm=tm),
        grid=(nt,),
        in_specs=[smem_rows(), smem_rows(),
                  pl.BlockSpec((tm, D_MODEL), lambda i: (i, 0)),
                  pl.BlockSpec(memory_space=pl.ANY),
                  pl.BlockSpec((1, D_MODEL), lambda i: (0, 0)),
                  pl.BlockSpec((1, D_MODEL), lambda i: (0, 0))],
        out_specs=pl.BlockSpec((tm, D_MODEL), lambda i: (i, 0)),
        out_shape=jax.ShapeDtypeStruct((T, D_MODEL), F32),
        scratch_shapes=[pltpu.VMEM((2, tm, D_MODEL), F32), pltpu.SemaphoreType.DMA((2,))],
        compiler_params=_params("arbitrary"),
        name="moe_combine_ln",
    )(pos1.reshape(nt, 1, tm), pos2.reshape(nt, 1, tm), x, ys, g.reshape(1, -1), b.reshape(1, -1))


def _moe_ln(x, router_w, wg, wu, wd, g, b, tm=512):
    router_pad = jnp.pad(router_w, ((0, 0), (0, ROUTER_LANES - N_EXPERTS)))
    meta, gates, cnt = _router(x, router_pad)
    pos1, pos2, token_of, gate_of, tile_expert, n_used = _dispatch_tables(meta, gates, cnt, tm)
    ys = _experts(x, token_of, gate_of, tile_expert, n_used, wg, wu, wd, tm)
    return _combine_ln(x, ys, pos1, pos2, g, b)


def _block_diag(w):
    nb, n, _ = w.shape
    eye = jnp.eye(nb, dtype=w.dtype)
    return (eye[:, None, :, None] * w[:, :, None, :]).reshape(nb * n, nb * n)


def kernel(x, w_in, w_out, hg_lb_logits, hg_norm_g, lru_conv_w, lru_conv_b, lru_wa, lru_ba, lru_wx, lru_bx, lru_lambda, da_lq1, da_lk1, da_lq2, da_lk2, da_norm_g, ln1_g, ln1_b, ln2_g, ln2_b, ffn_wg, ffn_wu, ffn_wd, router_w, moe_wg, moe_wu, moe_wd):
    batch, seq, _ = x.shape
    depth = w_in.shape[0]
    xf = x.reshape(batch * seq, D_MODEL)
    lb_p = jax.nn.softmax(hg_lb_logits.astype(F32), axis=0)
    lb_all = jnp.cumsum(lb_p, axis=0) - lb_p[0:1]
    for l in range(depth):
        rec, att = _in_proj(xf, w_in[l].astype(BF16))
        o_hg = _hgrn(rec, lb_all[l], hg_norm_g[l], batch, seq)
        o_lru = _lru(rec, lru_conv_w[l], lru_conv_b[l], _block_diag(lru_wa[l]).astype(BF16),
                     _block_diag(lru_wx[l]).astype(BF16), lru_ba[l], lru_bx[l], lru_lambda[l],
                     batch, seq)
        o_da = _attention(att, da_lq1[l], da_lk1[l], da_lq2[l], da_lk2[l], da_norm_g[l], l,
                          batch, seq)
        xf = _out_proj_ln(o_hg, o_lru, o_da, w_out[l].astype(BF16), xf, ln1_g[l], ln1_b[l])
        j = l // 2
        if l % 2 == 0:
            xf = _ffn_ln(xf, ffn_wg[j].astype(BF16), ffn_wu[j].astype(BF16), ffn_wd[j].astype(BF16),
                         ln2_g[l], ln2_b[l])
        else:
            xf = _moe_ln(xf, router_w[j], moe_wg[j].astype(BF16), moe_wu[j].astype(BF16),
                         moe_wd[j].astype(BF16), ln2_g[l], ln2_b[l])
    return xf.reshape(batch, seq, D_MODEL)
```

```python
import functools
import math

import jax
import jax.numpy as jnp
from jax import lax
from jax.experimental import pallas as pl
from jax.experimental.pallas import tpu as pltpu

D_MODEL = 1024
DEPTH = 2
HG_HEADS = 4
HG_KEY_DIM = 64
HG_WIDTH = 256
HG_CHUNK = 16
LRU_WIDTH = 256
LRU_BLOCKS = 4
CONV_WIDTH = 4
LRU_C = 8.0
DA_WIDTH = 512
DA_HEADS = 4
DA_HEAD_DIM = 64
D_FF = 3584
N_EXPERTS = 8
DN_ALPHA = (2.0 * DEPTH) ** 0.25
EPS = 1e-5
REC_WIDTH = 4 * HG_WIDTH + 2 * LRU_WIDTH
ATT_WIDTH = 3 * DA_WIDTH

V7X_VMEM_BYTES = 64 * 1024 * 1024
VMEM_LIMIT = 48 * 1024 * 1024

BF16 = jnp.bfloat16
F32 = jnp.float32


def _params(*semantics):
    return pltpu.CompilerParams(dimension_semantics=semantics, vmem_limit_bytes=VMEM_LIMIT)


def _layer_norm_rows(y, g, b):
    mu = jnp.mean(y, axis=-1, keepdims=True)
    yc = y - mu
    var = jnp.mean(yc * yc, axis=-1, keepdims=True)
    return yc * lax.rsqrt(var + EPS) * g + b


Q_SCALE = DA_HEAD_DIM ** -0.5 * math.log2(math.e)


def _in_proj_kernel(x_ref, w_ref, rec_ref, att_ref):
    xb = x_ref[...].astype(BF16)
    rec_ref[...] = jnp.dot(xb, w_ref[:, :REC_WIDTH], preferred_element_type=F32)
    q = jnp.dot(xb, w_ref[:, REC_WIDTH:REC_WIDTH + DA_WIDTH], preferred_element_type=F32)
    att_ref[:, :DA_WIDTH] = (q * Q_SCALE).astype(BF16)
    att_ref[:, DA_WIDTH:] = jnp.dot(xb, w_ref[:, REC_WIDTH + DA_WIDTH:],
                                    preferred_element_type=F32).astype(BF16)


def _in_proj(x, w_bf16, tm=512):
    T = x.shape[0]
    return pl.pallas_call(
        _in_proj_kernel,
        grid=(T // tm,),
        in_specs=[pl.BlockSpec((tm, D_MODEL), lambda i: (i, 0)),
                  pl.BlockSpec((D_MODEL, REC_WIDTH + ATT_WIDTH), lambda i: (0, 0))],
        out_specs=[pl.BlockSpec((tm, REC_WIDTH), lambda i: (i, 0)),
                   pl.BlockSpec((tm, ATT_WIDTH), lambda i: (i, 0))],
        out_shape=[jax.ShapeDtypeStruct((T, REC_WIDTH), F32),
                   jax.ShapeDtypeStruct((T, ATT_WIDTH), BF16)],
        compiler_params=_params("parallel"),
        name="in_proj",
    )(x, w_bf16)


def _hgrn_kernel(rec_ref, lb_ref, ng_ref, gmat_ref, sel_ref, bmask_ref, o_ref, st_ref, *, rows):
    @pl.when(pl.program_id(1) == 0)
    def _():
        st_ref[...] = jnp.zeros_like(st_ref)

    C = HG_CHUNK
    W = HG_WIDTH
    lb = lb_ref[...]
    ng = ng_ref[...]
    gmat = gmat_ref[...]
    sel = sel_ref[...]
    row = lax.broadcasted_iota(jnp.int32, (C, W), 0)

    def chunk(c, carry):
        r0 = pl.multiple_of(c * C, C)
        qr = rec_ref[pl.ds(r0, C), 0:W]
        z = rec_ref[pl.ds(r0, C), W:2 * W]
        v = rec_ref[pl.ds(r0, C), 2 * W:3 * W]
        g = rec_ref[pl.ds(r0, C), 3 * W:4 * W]
        logf = jnp.log(lb + (1.0 - lb) * jax.nn.sigmoid(z))
        kk = (1.0 - lb) * jax.nn.sigmoid(-z)
        q = qr * jax.nn.sigmoid(qr)
        b = logf
        for s in (1, 2, 4, 8):
            b = b + jnp.where(row >= s, pltpu.roll(b, s, 0), 0.0)
        v_b = v.astype(BF16)
        w_rows = []
        for t in range(C):
            rel = b[t:t + 1, :] - b
            dec = jnp.exp(jnp.where(row <= t, rel, -jnp.inf))
            w_rows.append(((q[t:t + 1, :] * kk) * dec).astype(BF16))
        w2 = jnp.concatenate(w_rows, axis=0)
        a = jnp.dot(w2, gmat, preferred_element_type=F32)
        p = a * jnp.concatenate([v] * C, axis=0)
        o_intra = jnp.dot(sel, p.astype(BF16), preferred_element_type=F32)
        st = st_ref[...]
        qd = (q * jnp.exp(b)).astype(BF16)
        o_inter = lax.dot_general(qd, st.astype(BF16), (((1,), (1,)), ((), ())),
                                  preferred_element_type=F32)
        b_last = b[C - 1:C, :]
        kd = (kk * jnp.exp(b_last - b)).astype(BF16)
        u_t = lax.dot_general(v_b, kd, (((0,), (0,)), ((), ())), preferred_element_type=F32)
        st_ref[...] = st * jnp.exp(b_last) + u_t * bmask_ref[...]
        o = o_intra + o_inter
        ms = jnp.dot(o * o, gmat.astype(F32), preferred_element_type=F32,
                     precision=lax.Precision.HIGHEST) * (1.0 / HG_KEY_DIM)
        out = o * lax.rsqrt(ms + EPS) * ng * (g * jax.nn.sigmoid(g))
        o_ref[pl.ds(r0, C), :] = out.astype(o_ref.dtype)
        return carry

    lax.fori_loop(0, rows // C, chunk, 0, unroll=8)


def _hgrn(rec, lb, norm_g, batch, seq, rows=256):
    T = rec.shape[0]
    nblk = seq // rows
    head = jnp.arange(HG_WIDTH) // HG_KEY_DIM
    same_head = head[:, None] == head[None, :]
    gmat = same_head.astype(BF16)
    bmask = same_head.astype(F32)
    sel = (jnp.arange(HG_CHUNK)[:, None] == (jnp.arange(HG_CHUNK * HG_CHUNK) // HG_CHUNK)[None, :]).astype(BF16)
    const = lambda shape: pl.BlockSpec(shape, lambda b, i: (0, 0))
    return pl.pallas_call(
        functools.partial(_hgrn_kernel, rows=rows),
        grid=(batch, nblk),
        in_specs=[pl.BlockSpec((rows, 4 * HG_WIDTH), lambda b, i: (b * nblk + i, 0)),
                  const((1, HG_WIDTH)), const((1, HG_WIDTH)),
                  const((HG_WIDTH, HG_WIDTH)), const((HG_CHUNK, HG_CHUNK * HG_CHUNK)),
                  const((HG_WIDTH, HG_WIDTH))],
        out_specs=pl.BlockSpec((rows, HG_WIDTH), lambda b, i: (b * nblk + i, 0)),
        out_shape=jax.ShapeDtypeStruct((T, HG_WIDTH), BF16),
        scratch_shapes=[pltpu.VMEM((HG_WIDTH, HG_WIDTH), F32)],
        compiler_params=_params("parallel", "arbitrary"),
        name="hgrn2",
    )(rec, lb.reshape(1, -1), norm_g.reshape(1, -1), gmat, sel, bmask)


def _lru_kernel(rec_ref, cw_ref, cb_ref, wa_ref, wx_ref, ba_ref, bx_ref, lam_ref, o_ref,
                xprev_ref, h_ref, *, rows):
    @pl.when(pl.program_id(1) == 0)
    def _():
        xprev_ref[...] = jnp.zeros_like(xprev_ref)
        h_ref[...] = jnp.zeros_like(h_ref)

    W = LRU_WIDTH
    x = rec_ref[:, 0:W]
    gate = rec_ref[:, W:2 * W]
    row = lax.broadcasted_iota(jnp.int32, (rows, W), 0)
    xp = xprev_ref[...]
    tail = jnp.zeros((rows - 8, W), F32)
    xc = cb_ref[...] + cw_ref[CONV_WIDTH - 1:CONV_WIDTH, :] * x
    for j in range(1, CONV_WIDTH):
        prev = jnp.concatenate([pltpu.roll(xp, j, 0), tail], axis=0)
        xs = jnp.where(row >= j, pltpu.roll(x, j, 0), prev)
        xc = xc + cw_ref[CONV_WIDTH - 1 - j:CONV_WIDTH - j, :] * xs
    xprev_ref[...] = x[rows - 8:rows, :]

    xcb = xc.astype(BF16)
    r = jax.nn.sigmoid(jnp.dot(xcb, wa_ref[...], preferred_element_type=F32) + ba_ref[...])
    i = jax.nn.sigmoid(jnp.dot(xcb, wx_ref[...], preferred_element_type=F32) + bx_ref[...])
    lam = lam_ref[...]
    log_sig = jnp.minimum(lam, 0.0) - jnp.log1p(jnp.exp(-jnp.abs(lam)))
    log_a = LRU_C * r * log_sig
    a = jnp.exp(log_a)
    th = jnp.tanh(log_a)
    one_minus_a2 = -2.0 * th / (1.0 - th)
    u = jnp.sqrt(one_minus_a2) * i * xc

    for s in (1, 2, 4):
        a_s = jnp.where(row >= s, pltpu.roll(a, s, 0), 1.0)
        u_s = jnp.where(row >= s, pltpu.roll(u, s, 0), 0.0)
        u = a * u_s + u
        a = a * a_s
    s = 8
    while s < rows:
        a_s = jnp.concatenate([jnp.ones((s, W), F32), a[:rows - s, :]], axis=0)
        u_s = jnp.concatenate([jnp.zeros((s, W), F32), u[:rows - s, :]], axis=0)
        u = a * u_s + u
        a = a * a_s
        s *= 2
    h = a * h_ref[0:1, :] + u
    h_ref[...] = jnp.broadcast_to(h[rows - 1:rows, :], h_ref.shape)
    o_ref[...] = (h * jax.nn.gelu(gate)).astype(o_ref.dtype)


def _lru(rec, conv_w, conv_b, wa_bd, wx_bd, ba, bx, lam, batch, seq, rows=256):
    T = rec.shape[0]
    nblk = seq // rows
    W = LRU_WIDTH
    const = lambda shape: pl.BlockSpec(shape, lambda b, i: (0, 0))
    return pl.pallas_call(
        functools.partial(_lru_kernel, rows=rows),
        grid=(batch, nblk),
        in_specs=[pl.BlockSpec((rows, 2 * W), lambda b, i: (b * nblk + i, 2)),
                  const((CONV_WIDTH, W)), const((1, W)), const((W, W)), const((W, W)),
                  const((1, W)), const((1, W)), const((1, W))],
        out_specs=pl.BlockSpec((rows, W), lambda b, i: (b * nblk + i, 0)),
        out_shape=jax.ShapeDtypeStruct((T, W), BF16),
        scratch_shapes=[pltpu.VMEM((8, W), F32), pltpu.VMEM((8, W), F32)],
        compiler_params=_params("parallel", "arbitrary"),
        name="rglru",
    )(rec, conv_w, conv_b.reshape(1, -1), wa_bd, wx_bd, ba.reshape(1, -1), bx.reshape(1, -1),
      lam.reshape(1, -1))


def _attn_kernel(q_ref, k_ref, v_ref, lq1_ref, lk1_ref, lq2_ref, lk2_ref, ng_ref, o_ref,
                 vt_ref, qm_ref, s0_ref, s1_ref, m_ref, l_ref, acc_ref, *, tq, tk, qs, lam_init, seq):
    d = DA_HEAD_DIM
    qi = pl.program_id(2)

    @pl.when(qi == 0)
    def _():
        def transpose_values(j, carry):
            r0 = pl.multiple_of(j * tk, tk)
            vt_ref[j] = v_ref[pl.ds(r0, tk), :].astype(F32).T.astype(BF16)
            return carry
        lax.fori_loop(0, seq // tk, transpose_values, 0)

    q = q_ref[...]
    lane = lax.broadcasted_iota(jnp.int32, q.shape, 1)
    qm_ref[0] = jnp.where(lane < d, q, jnp.zeros_like(q))
    qm_ref[1] = jnp.where(lane >= d, q, jnp.zeros_like(q))
    m_ref[...] = jnp.full_like(m_ref, -jnp.inf)
    l_ref[...] = jnp.zeros_like(l_ref)
    acc_ref[...] = jnp.zeros_like(acc_ref)
    s_refs = (s0_ref, s1_ref)

    def scores(j, c):
        k0 = pl.multiple_of(j * tk, tk)
        s_refs[c][...] = lax.dot_general(k_ref[pl.ds(k0, tk), :], qm_ref[c], (((1,), (1,)), ((), ())),
                                         preferred_element_type=F32)

    def softmax_pv(j, c, masked):
        for st in range(tq // qs):
            cols = pl.ds(st * qs, qs)
            nk = min((st + 1) * qs, tk) if masked else tk
            vt = vt_ref[j, :, 0:nk]
            s = s_refs[c][0:nk, cols]
            if masked:
                kpos = lax.broadcasted_iota(jnp.int32, (nk, qs), 0)
                qpos = st * qs + lax.broadcasted_iota(jnp.int32, (nk, qs), 1)
                s = jnp.where(kpos <= qpos, s, -jnp.inf)
            m_old = m_ref[c, :, cols]
            m_new = jnp.maximum(m_old, jnp.max(s, axis=0, keepdims=True))
            p = jnp.exp2(s - m_new)
            alpha = jnp.exp2(m_old - m_new)
            l_ref[c, :, cols] = alpha * l_ref[c, :, cols] + jnp.sum(p, axis=0, keepdims=True)
            acc_ref[c, :, cols] = (alpha * acc_ref[c, :, cols]
                                   + jnp.dot(vt, p.astype(BF16), preferred_element_type=F32))
            m_ref[c, :, cols] = m_new

    def full_block(j, carry):
        scores(j, 1)
        softmax_pv(j, 0, False)
        scores(j + 1, 0)
        softmax_pv(j, 1, False)
        return carry

    scores(0, 0)
    lax.fori_loop(0, qi, full_block, 0)
    scores(qi, 1)
    softmax_pv(qi, 0, True)
    softmax_pv(qi, 1, True)

    lam =(jnp.exp(jnp.sum(lq1_ref[...] * lk1_ref[...], axis=-1, keepdims=True))
           - jnp.exp(jnp.sum(lq2_ref[...] * lk2_ref[...], axis=-1, keepdims=True)) + lam_init)
    o_t = acc_ref[0] * (1.0 / l_ref[0]) - lam * (acc_ref[1] * (1.0 / l_ref[1]))
    ms = jnp.mean(o_t * o_t, axis=0, keepdims=True)
    o_t = o_t * lax.rsqrt(ms + EPS) * ng_ref[...] * (1.0 - lam_init)
    o_ref[...] = o_t.T.astype(o_ref.dtype)


def _attention(att, lq1, lk1, lq2, lk2, norm_g, layer, batch, seq, tq=512, qs=256):
    T = att.shape[0]
    tk = tq
    nq = seq // tq
    hw = 2 * DA_HEAD_DIM
    lam_init = 0.8 - 0.6 * math.exp(-0.3 * layer)
    vec = lambda: pl.BlockSpec((1, DA_HEAD_DIM), lambda b, h, i: (0, 0))
    return pl.pallas_call(
        functools.partial(_attn_kernel, tq=tq, tk=tk, qs=qs, lam_init=lam_init, seq=seq),
        grid=(batch, DA_HEADS, nq),
        in_specs=[pl.BlockSpec((tq, hw), lambda b, h, i: (b * nq + i, h)),
                  pl.BlockSpec((seq, hw), lambda b, h, i: (b, DA_HEADS + h)),
                  pl.BlockSpec((seq, hw), lambda b, h, i: (b, 2 * DA_HEADS + h)),
                  vec(), vec(), vec(), vec(),
                  pl.BlockSpec((hw, 1), lambda b, h, i: (h, 0))],
        out_specs=pl.BlockSpec((tq, hw), lambda b, h, i: (b * nq + i, h)),
        out_shape=jax.ShapeDtypeStruct((T, DA_WIDTH), BF16),
        scratch_shapes=[pltpu.VMEM((seq // tk, hw, tk), BF16), pltpu.VMEM((2, tq, hw), BF16),
                        pltpu.VMEM((tk, tq), F32), pltpu.VMEM((tk, tq), F32),
                        pltpu.VMEM((2, 1, tq), F32), pltpu.VMEM((2, 1, tq), F32),
                        pltpu.VMEM((2, hw, tq), F32)],
        compiler_params=_params("arbitrary", "arbitrary", "arbitrary"),
        name="diff_attn",
    )(att, att, att, lq1.reshape(1, -1), lk1.reshape(1, -1), lq2.reshape(1, -1),
      lk2.reshape(1, -1), norm_g.reshape(-1, 1))


def _out_proj_kernel(hg_ref, lru_ref, da_ref, w_ref, x_ref, g_ref, b_ref, o_ref):
    h = jnp.dot(hg_ref[...], w_ref[0:HG_WIDTH, :], preferred_element_type=F32)
    h = h + jnp.dot(lru_ref[...], w_ref[HG_WIDTH:HG_WIDTH + LRU_WIDTH, :], preferred_element_type=F32)
    h = h + jnp.dot(da_ref[...], w_ref[HG_WIDTH + LRU_WIDTH:, :], preferred_element_type=F32)
    o_ref[...] = _layer_norm_rows(DN_ALPHA * x_ref[...] + h, g_ref[...], b_ref[...])


def _out_proj_ln(o_hg, o_lru, o_da, w_bf16, x, g, b, tm=512):
    T = x.shape[0]
    rows = lambda w: pl.BlockSpec((tm, w), lambda i: (i, 0))
    const = lambda shape: pl.BlockSpec(shape, lambda i: (0, 0))
    return pl.pallas_call(
        _out_proj_kernel,
        grid=(T // tm,),
        in_specs=[rows(HG_WIDTH), rows(LRU_WIDTH), rows(DA_WIDTH), const((D_MODEL, D_MODEL)),
                  rows(D_MODEL), const((1, D_MODEL)), const((1, D_MODEL))],
        out_specs=rows(D_MODEL),
        out_shape=jax.ShapeDtypeStruct((T, D_MODEL), F32),
        compiler_params=_params("parallel"),
        name="out_proj_ln",
    )(o_hg, o_lru, o_da, w_bf16, x, g.reshape(1, -1), b.reshape(1, -1))


def _ffn_kernel(x_ref, wg_ref, wu_ref, wd_ref, g_ref, b_ref, o_ref, xb_ref, acc_ref):
    f = pl.program_id(1)

    @pl.when(f == 0)
    def _():
        xb_ref[...] = x_ref[...].astype(BF16)
        acc_ref[...] = jnp.zeros_like(acc_ref)

    xb = xb_ref[...]
    gate = jnp.dot(xb, wg_ref[...], preferred_element_type=F32)
    up = jnp.dot(xb, wu_ref[...], preferred_element_type=F32)
    hmid = (gate * jax.nn.sigmoid(gate) * up).astype(BF16)
    acc_ref[...] += jnp.dot(hmid, wd_ref[...], preferred_element_type=F32)

    @pl.when(f == pl.num_programs(1) - 1)
    def _():
        o_ref[...] = _layer_norm_rows(DN_ALPHA * x_ref[...] + acc_ref[...], g_ref[...], b_ref[...])


def _ffn_ln(x, wg, wu, wd, g, b, tm=1024, tf=512):
    T = x.shape[0]
    return pl.pallas_call(
        _ffn_kernel,
        grid=(T // tm, D_FF // tf),
        in_specs=[pl.BlockSpec((tm, D_MODEL), lambda i, f: (i, 0)),
                  pl.BlockSpec((D_MODEL, tf), lambda i, f: (0, f)),
                  pl.BlockSpec((D_MODEL, tf), lambda i, f: (0, f)),
                  pl.BlockSpec((tf, D_MODEL), lambda i, f: (f, 0)),
                  pl.BlockSpec((1, D_MODEL), lambda i, f: (0, 0)),
                  pl.BlockSpec((1, D_MODEL), lambda i, f: (0, 0))],
        out_specs=pl.BlockSpec((tm, D_MODEL), lambda i, f: (i, 0)),
        out_shape=jax.ShapeDtypeStruct((T, D_MODEL), F32),
        scratch_shapes=[pltpu.VMEM((tm, D_MODEL), BF16), pltpu.VMEM((tm, D_MODEL), F32)],
        compiler_params=_params("parallel", "arbitrary"),
        name="ffn_ln",
    )(x, wg, wu, wd, g.reshape(1, -1), b.reshape(1, -1))


ROUTER_LANES = 128
META_I1, META_I2, META_R1, META_R2 = 0, 1, 2, 3


def _router_kernel(x_ref, rw_ref, tri_ref, meta_ref, gate_ref, cnt_ref, carry_ref):
    @pl.when(pl.program_id(0) == 0)
    def _():
        carry_ref[...] = jnp.zeros_like(carry_ref)

    logits = jnp.dot(x_ref[...], rw_ref[...], preferred_element_type=F32,
                     precision=lax.Precision.HIGHEST)
    lane = lax.broadcasted_iota(jnp.int32, logits.shape, 1)
    neg = -jnp.inf
    l1 = jnp.where(lane < N_EXPERTS, logits, neg)
    v1 = jnp.max(l1, axis=-1, keepdims=True)
    i1 = jnp.min(jnp.where(l1 == v1, lane, ROUTER_LANES), axis=-1, keepdims=True)
    l2 = jnp.where(lane == i1, neg, l1)
    v2 = jnp.max(l2, axis=-1, keepdims=True)
    i2 = jnp.min(jnp.where(l2 == v2, lane, ROUTER_LANES), axis=-1, keepdims=True)
    e2 = jnp.exp(v2 - v1)
    g1 = 1.0 / (1.0 + e2)
    g2 = e2 / (1.0 + e2)
    member = jnp.where((lane == i1) | (lane == i2), 1.0, 0.0)
    rank = jnp.dot(tri_ref[...], member.astype(BF16), preferred_element_type=F32) + carry_ref[0:1, :]
    r1 = jnp.sum(jnp.where(lane == i1, rank, 0.0), axis=-1, keepdims=True).astype(jnp.int32)
    r2 = jnp.sum(jnp.where(lane == i2, rank, 0.0), axis=-1, keepdims=True).astype(jnp.int32)
    meta = jnp.where(lane == META_I1, i1, 0) + jnp.where(lane == META_I2, i2, 0)
    meta = meta + jnp.where(lane == META_R1, r1, 0) + jnp.where(lane == META_R2, r2, 0)
    meta_ref[...] = meta
    gate_ref[...] = jnp.where(lane == 0, g1, 0.0) + jnp.where(lane == 1, g2, 0.0)
    carry = carry_ref[...] + jnp.sum(member, axis=0, keepdims=True)
    carry_ref[...] = carry
    cnt_ref[...] = carry


def _router(x, router_pad, tm=1024):
    T = x.shape[0]
    tri = (jnp.arange(tm)[:, None] > jnp.arange(tm)[None, :]).astype(BF16)
    return pl.pallas_call(
        _router_kernel,
        grid=(T // tm,),
        in_specs=[pl.BlockSpec((tm, D_MODEL), lambda i: (i, 0)),
                  pl.BlockSpec((D_MODEL, ROUTER_LANES), lambda i: (0, 0)),
                  pl.BlockSpec((tm, tm), lambda i: (0, 0))],
        out_specs=[pl.BlockSpec((tm, ROUTER_LANES), lambda i: (i, 0)),
                   pl.BlockSpec((tm, ROUTER_LANES), lambda i: (i, 0)),
                   pl.BlockSpec((8, ROUTER_LANES), lambda i: (0, 0))],
        out_shape=[jax.ShapeDtypeStruct((T, ROUTER_LANES), jnp.int32),
                   jax.ShapeDtypeStruct((T, ROUTER_LANES), F32),
                   jax.ShapeDtypeStruct((8, ROUTER_LANES), F32)],
        scratch_shapes=[pltpu.VMEM((8, ROUTER_LANES), F32)],
        compiler_params=_params("arbitrary"),
        name="moe_router",
    )(x, router_pad, tri)


def _dispatch_tables(meta, gates, cnt, tm):
    T = meta.shape[0]
    counts = cnt[0, :N_EXPERTS].astype(jnp.int32)
    padded = ((counts + tm - 1) // tm) * tm
    ends = jnp.cumsum(padded)
    off = ends - padded
    pos1 = off[meta[:, META_I1]] + meta[:, META_R1]
    pos2 = off[meta[:, META_I2]] + meta[:, META_R2]
    n_rows = 2 * T + N_EXPERTS * tm
    tok = jnp.arange(T, dtype=jnp.int32)
    token_of = jnp.zeros((n_rows,), jnp.int32).at[pos1].set(tok).at[pos2].set(tok)
    gate_of = jnp.zeros((n_rows,), F32).at[pos1].set(gates[:, 0]).at[pos2].set(gates[:, 1])
    n_tiles = n_rows // tm
    tile_start = jnp.arange(n_tiles, dtype=jnp.int32) * tm
    tile_expert = jnp.minimum(jnp.sum((tile_start[:, None] >= ends[None, :]).astype(jnp.int32), axis=1),
                              N_EXPERTS - 1)
    n_used = (ends[-1] // tm).astype(jnp.int32).reshape(1)
    return pos1, pos2, token_of, gate_of, tile_expert, n_used


def _expert_kernel(te_ref, nu_ref, tok_ref, x_hbm, gate_ref, wg_ref, wu_ref, wd_ref, o_ref,
                   xbuf_ref, xb_ref, acc_ref, sem, *, tm):
    i = pl.program_id(0)
    f = pl.program_id(1)
    used = i < nu_ref[0]

    def row_copy(r):
        return pltpu.make_async_copy(x_hbm.at[pl.ds(tok_ref[0, 0, r], 1)],
                                     xbuf_ref.at[pl.ds(r, 1)], sem)

    @pl.when(used & (f == 0))
    def _():
        def issue(r, carry):
            row_copy(r).start()
            return carry

        def wait(r, carry):
            row_copy(r).wait()
            return carry

        lax.fori_loop(0, tm, issue, 0, unroll=8)
        lax.fori_loop(0, tm, wait, 0, unroll=8)
        xb_ref[...] = xbuf_ref[...].astype(BF16)
        acc_ref[...] = jnp.zeros_like(acc_ref)

    @pl.when(used)
    def _():
        xb = xb_ref[...]
        gate = jnp.dot(xb, wg_ref[0], preferred_element_type=F32)
        up = jnp.dot(xb, wu_ref[0], preferred_element_type=F32)
        hmid = (gate * jax.nn.sigmoid(gate) * up).astype(BF16)
        acc_ref[...] += jnp.dot(hmid, wd_ref[0], preferred_element_type=F32)

    last = f == pl.num_programs(1) - 1

    @pl.when(used & last)
    def _():
        o_ref[...] = acc_ref[...] * gate_ref[...]

    @pl.when(jnp.logical_not(used) & last)
    def _():
        o_ref[...] = jnp.zeros_like(o_ref)


def _experts(x, token_of, gate_of, tile_expert, n_used, wg, wu, wd, tm, tf=512):
    n_rows = token_of.shape[0]
    n_tiles = n_rows // tm
    nf = D_FF // tf
    fidx = lambda i, f, nu: jnp.where(i < nu[0], f, nf - 1)
    grid_spec = pltpu.PrefetchScalarGridSpec(
        num_scalar_prefetch=2,
        grid=(n_tiles, nf),
        in_specs=[pl.BlockSpec((1, 1, tm), lambda i, f, te, nu: (i, 0, 0), memory_space=pltpu.SMEM),
                  pl.BlockSpec(memory_space=pl.ANY),
                  pl.BlockSpec((tm, 1), lambda i, f, te, nu: (i, 0)),
                  pl.BlockSpec((1, D_MODEL, tf), lambda i, f, te, nu: (te[i], 0, fidx(i, f, nu))),
                  pl.BlockSpec((1, D_MODEL, tf), lambda i, f, te, nu: (te[i], 0, fidx(i, f, nu))),
                  pl.BlockSpec((1, tf, D_MODEL), lambda i, f, te, nu: (te[i], fidx(i, f, nu), 0))],
        out_specs=pl.BlockSpec((tm, D_MODEL), lambda i, f, te, nu: (i, 0)),
        scratch_shapes=[pltpu.VMEM((tm, D_MODEL), F32), pltpu.VMEM((tm, D_MODEL), BF16),
                        pltpu.VMEM((tm, D_MODEL), F32), pltpu.SemaphoreType.DMA(())],
    )
    return pl.pallas_call(
        functools.partial(_expert_kernel, tm=tm),
        grid_spec=grid_spec,
        out_shape=jax.ShapeDtypeStruct((n_rows, D_MODEL), F32),
        compiler_params=_params("arbitrary", "arbitrary"),
        name="moe_experts",
    )(tile_expert, n_used, token_of.reshape(n_tiles, 1, tm), x, gate_of.reshape(n_rows, 1),
      wg, wu, wd)


def _combine_kernel(p1_ref, p2_ref, x_ref, ys_hbm, g_ref, b_ref, o_ref, buf_ref, sem, *, tm):
    def row_copy(slot, p_ref, r):
        return pltpu.make_async_copy(ys_hbm.at[pl.ds(p_ref[0, 0, r], 1)],
                                     buf_ref.at[slot, pl.ds(r, 1)], sem.at[slot])

    def issue(r, carry):
        row_copy(0, p1_ref, r).start()
        row_copy(1, p2_ref, r).start()
        return carry

    def wait(r, carry):
        row_copy(0, p1_ref, r).wait()
        row_copy(1, p2_ref, r).wait()
        return carry

    lax.fori_loop(0, tm, issue, 0, unroll=8)
    lax.fori_loop(0, tm, wait, 0, unroll=8)
    y = buf_ref[0] + buf_ref[1]
    o_ref[...] = _layer_norm_rows(DN_ALPHA * x_ref[...] + y, g_ref[...], b_ref[...])


def _combine_ln(x, ys, pos1, pos2, g, b, tm=512):
    T = x.shape[0]
    nt = T // tm
    smem_rows = lambda: pl.BlockSpec((1, 1, tm), lambda i: (i, 0, 0), memory_space=pltpu.SMEM)
    return pl.pallas_call(
        functools.partial(_combine_kernel, tm=tm),
        grid=(nt,),
        in_specs=[smem_rows(), smem_rows(),
                  pl.BlockSpec((tm, D_MODEL), lambda i: (i, 0)),
                  pl.BlockSpec(memory_space=pl.ANY),
                  pl.BlockSpec((1, D_MODEL), lambda i: (0, 0)),
                  pl.BlockSpec((1, D_MODEL), lambda i: (0, 0))],
        out_specs=pl.BlockSpec((tm, D_MODEL), lambda i: (i, 0)),
        out_shape=jax.ShapeDtypeStruct((T, D_MODEL), F32),
        scratch_shapes=[pltpu.VMEM((2, tm, D_MODEL), F32), pltpu.SemaphoreType.DMA((2,))],
        compiler_params=_params("arbitrary"),
        name="moe_combine_ln",
    )(pos1.reshape(nt, 1, tm), pos2.reshape(nt, 1, tm), x, ys, g.reshape(1, -1), b.reshape(1, -1))


def _moe_ln(x, router_w, wg, wu, wd, g, b, tm=512):
    router_pad = jnp.pad(router_w, ((0, 0), (0, ROUTER_LANES - N_EXPERTS)))
    meta, gates, cnt = _router(x, router_pad)
    pos1, pos2, token_of, gate_of, tile_expert, n_used = _dispatch_tables(meta, gates, cnt, tm)
    ys = _experts(x, token_of, gate_of, tile_expert, n_used, wg, wu, wd, tm)
    return _combine_ln(x, ys, pos1, pos2, g, b)


def _block_diag(w):
    nb, n, _ = w.shape
    eye = jnp.eye(nb, dtype=w.dtype)
    return (eye[:, None, :, None] * w[:, :, None, :]).reshape(nb * n, nb * n)


def kernel(x, w_in, w_out, hg_lb_logits, hg_norm_g, lru_conv_w, lru_conv_b, lru_wa, lru_ba, lru_wx, lru_bx, lru_lambda, da_lq1, da_lk1, da_lq2, da_lk2, da_norm_g, ln1_g, ln1_b, ln2_g, ln2_b, ffn_wg, ffn_wu, ffn_wd, router_w, moe_wg, moe_wu, moe_wd):
    batch, seq, _ = x.shape
    depth = w_in.shape[0]
    xf = x.reshape(batch * seq, D_MODEL)
    lb_p = jax.nn.softmax(hg_lb_logits.astype(F32), axis=0)
    lb_all = jnp.cumsum(lb_p, axis=0) - lb_p[0:1]
    for l in range(depth):
        rec, att = _in_proj(xf, w_in[l].astype(BF16))
        o_hg = _hgrn(rec, lb_all[l], hg_norm_g[l], batch, seq)
        o_lru = _lru(rec, lru_conv_w[l], lru_conv_b[l], _block_diag(lru_wa[l]).astype(BF16),
                     _block_diag(lru_wx[l]).astype(BF16), lru_ba[l], lru_bx[l], lru_lambda[l],
                     batch, seq)
        o_da = _attention(att, da_lq1[l], da_lk1[l], da_lq2[l], da_lk2[l], da_norm_g[l], l,
                          batch, seq)
        xf = _out_proj_ln(o_hg, o_lru, o_da, w_out[l].astype(BF16), xf, ln1_g[l], ln1_b[l])
        j = l // 2
        if l % 2 == 0:
            xf = _ffn_ln(xf, ffn_wg[j].astype(BF16), ffn_wu[j].astype(BF16), ffn_wd[j].astype(BF16),
                         ln2_g[l], ln2_b[l])
        else:
            xf = _moe_ln(xf, router_w[j], moe_wg[j].astype(BF16), moe_wu[j].astype(BF16),
                         moe_wd[j].astype(BF16), ln2_g[l], ln2_b[l])
    return xf.reshape(batch, seq, D_MODEL)
```

```python
import functools
import math

import jax
import jax.numpy as jnp
from jax import lax
from jax.experimental import pallas as pl
from jax.experimental.pallas import tpu as pltpu

D_MODEL = 1024
DEPTH = 2
HG_HEADS = 4
HG_KEY_DIM = 64
HG_WIDTH = 256
HG_CHUNK = 16
LRU_WIDTH = 256
LRU_BLOCKS = 4
CONV_WIDTH = 4
LRU_C = 8.0
DA_WIDTH = 512
DA_HEADS = 4
DA_HEAD_DIM = 64
D_FF = 3584
N_EXPERTS = 8
DN_ALPHA = (2.0 * DEPTH) ** 0.25
EPS = 1e-5
REC_WIDTH = 4 * HG_WIDTH + 2 * LRU_WIDTH
ATT_WIDTH = 3 * DA_WIDTH

V7X_VMEM_BYTES = 64 * 1024 * 1024
VMEM_LIMIT = 48 * 1024 * 1024

BF16 = jnp.bfloat16
F32 = jnp.float32


def _params(*semantics):
    return pltpu.CompilerParams(dimension_semantics=semantics, vmem_limit_bytes=VMEM_LIMIT)


def _layer_norm_rows(y, g, b):
    mu = jnp.mean(y, axis=-1, keepdims=True)
    yc = y - mu
    var = jnp.mean(yc * yc, axis=-1, keepdims=True)
    return yc * lax.rsqrt(var + EPS) * g + b


Q_SCALE = DA_HEAD_DIM ** -0.5 * math.log2(math.e)


def _in_proj_kernel(x_ref, w_ref, rec_ref, att_ref):
    xb = x_ref[...].astype(BF16)
    rec_ref[...] = jnp.dot(xb, w_ref[:, :REC_WIDTH], preferred_element_type=F32)
    q = jnp.dot(xb, w_ref[:, REC_WIDTH:REC_WIDTH + DA_WIDTH], preferred_element_type=F32)
    att_ref[:, :DA_WIDTH] = (q * Q_SCALE).astype(BF16)
    att_ref[:, DA_WIDTH:] = jnp.dot(xb, w_ref[:, REC_WIDTH + DA_WIDTH:],
                                    preferred_element_type=F32).astype(BF16)


def _in_proj(x, w_bf16, tm=512):
    T = x.shape[0]
    return pl.pallas_call(
        _in_proj_kernel,
        grid=(T // tm,),
        in_specs=[pl.BlockSpec((tm, D_MODEL), lambda i: (i, 0)),
                  pl.BlockSpec((D_MODEL, REC_WIDTH + ATT_WIDTH), lambda i: (0, 0))],
        out_specs=[pl.BlockSpec((tm, REC_WIDTH), lambda i: (i, 0)),
                   pl.BlockSpec((tm, ATT_WIDTH), lambda i: (i, 0))],
        out_shape=[jax.ShapeDtypeStruct((T, REC_WIDTH), F32),
                   jax.ShapeDtypeStruct((T, ATT_WIDTH), BF16)],
        compiler_params=_params("parallel"),
        name="in_proj",
    )(x, w_bf16)


def _hgrn_kernel(rec_ref, lb_ref, ng_ref, gmat_ref, sel_ref, bmask_ref, o_ref, st_ref, *, rows):
    @pl.when(pl.program_id(1) == 0)
    def _():
        st_ref[...] = jnp.zeros_like(st_ref)

    C = HG_CHUNK
    W = HG_WIDTH
    lb = lb_ref[...]
    ng = ng_ref[...]
    gmat = gmat_ref[...]
    sel = sel_ref[...]
    row = lax.broadcasted_iota(jnp.int32, (C, W), 0)

    def chunk(c, carry):
        r0 = pl.multiple_of(c * C, C)
        qr = rec_ref[pl.ds(r0, C), 0:W]
        z = rec_ref[pl.ds(r0, C), W:2 * W]
        v = rec_ref[pl.ds(r0, C), 2 * W:3 * W]
        g = rec_ref[pl.ds(r0, C), 3 * W:4 * W]
        logf = jnp.log(lb + (1.0 - lb) * jax.nn.sigmoid(z))
        kk = (1.0 - lb) * jax.nn.sigmoid(-z)
        q = qr * jax.nn.sigmoid(qr)
        b = logf
        for s in (1, 2, 4, 8):
            b = b + jnp.where(row >= s, pltpu.roll(b, s, 0), 0.0)
        v_b = v.astype(BF16)
        w_rows = []
        for t in range(C):
            rel = b[t:t + 1, :] - b
            dec = jnp.exp(jnp.where(row <= t, rel, -jnp.inf))
            w_rows.append(((q[t:t + 1, :] * kk) * dec).astype(BF16))
        w2 = jnp.concatenate(w_rows, axis=0)
        a = jnp.dot(w2, gmat, preferred_element_type=F32)
        p = a * jnp.concatenate([v] * C, axis=0)
        o_intra = jnp.dot(sel, p.astype(BF16), preferred_element_type=F32)
        st = st_ref[...]
        qd = (q * jnp.exp(b)).astype(BF16)
        o_inter = lax.dot_general(qd, st.astype(BF16), (((1,), (1,)), ((), ())),
                                  preferred_element_type=F32)
        b_last = b[C - 1:C, :]
        kd = (kk * jnp.exp(b_last - b)).astype(BF16)
        u_t = lax.dot_general(v_b, kd, (((0,), (0,)), ((), ())), preferred_element_type=F32)
        st_ref[...] = st * jnp.exp(b_last) + u_t * bmask_ref[...]
        o = o_intra + o_inter
        ms = jnp.dot(o * o, gmat.astype(F32), preferred_element_type=F32,
                     precision=lax.Precision.HIGHEST) * (1.0 / HG_KEY_DIM)
        out = o * lax.rsqrt(ms + EPS) * ng * (g * jax.nn.sigmoid(g))
        o_ref[pl.ds(r0, C), :] = out.astype(o_ref.dtype)
        return carry

    lax.fori_loop(0, rows // C, chunk, 0, unroll=8)


def _hgrn(rec, lb, norm_g, batch, seq, rows=256):
    T = rec.shape[0]
    nblk = seq // rows
    head = jnp.arange(HG_WIDTH) // HG_KEY_DIM
    same_head = head[:, None] == head[None, :]
    gmat = same_head.astype(BF16)
    bmask = same_head.astype(F32)
    sel = (jnp.arange(HG_CHUNK)[:, None] == (jnp.arange(HG_CHUNK * HG_CHUNK) // HG_CHUNK)[None, :]).astype(BF16)
    const = lambda shape: pl.BlockSpec(shape, lambda b, i: (0, 0))
    return pl.pallas_call(
        functools.partial(_hgrn_kernel, rows=rows),
        grid=(batch, nblk),
        in_specs=[pl.BlockSpec((rows, 4 * HG_WIDTH), lambda b, i: (b * nblk + i, 0)),
                  const((1, HG_WIDTH)), const((1, HG_WIDTH)),
                  const((HG_WIDTH, HG_WIDTH)), const((HG_CHUNK, HG_CHUNK * HG_CHUNK)),
                  const((HG_WIDTH, HG_WIDTH))],
        out_specs=pl.BlockSpec((rows, HG_WIDTH), lambda b, i: (b * nblk + i, 0)),
        out_shape=jax.ShapeDtypeStruct((T, HG_WIDTH), BF16),
        scratch_shapes=[pltpu.VMEM((HG_WIDTH, HG_WIDTH), F32)],
        compiler_params=_params("parallel", "arbitrary"),
        name="hgrn2",
    )(rec, lb.reshape(1, -1), norm_g.reshape(1, -1), gmat, sel, bmask)


def _lru_kernel(rec_ref, cw_ref, cb_ref, wa_ref, wx_ref, ba_ref, bx_ref, lam_ref, o_ref,
                xprev_ref, h_ref, *, rows):
    @pl.when(pl.program_id(1) == 0)
    def _():
        xprev_ref[...] = jnp.zeros_like(xprev_ref)
        h_ref[...] = jnp.zeros_like(h_ref)

    W = LRU_WIDTH
    x = rec_ref[:, 0:W]
    gate = rec_ref[:, W:2 * W]
    row = lax.broadcasted_iota(jnp.int32, (rows, W), 0)
    xp = xprev_ref[...]
    tail = jnp.zeros((rows - 8, W), F32)
    xc = cb_ref[...] + cw_ref[CONV_WIDTH - 1:CONV_WIDTH, :] * x
    for j in range(1, CONV_WIDTH):
        prev = jnp.concatenate([pltpu.roll(xp, j, 0), tail], axis=0)
        xs = jnp.where(row >= j, pltpu.roll(x, j, 0), prev)
        xc = xc + cw_ref[CONV_WIDTH - 1 - j:CONV_WIDTH - j, :] * xs
    xprev_ref[...] = x[rows - 8:rows, :]

    xcb = xc.astype(BF16)
    r = jax.nn.sigmoid(jnp.dot(xcb, wa_ref[...], preferred_element_type=F32) + ba_ref[...])
    i = jax.nn.sigmoid(jnp.dot(xcb, wx_ref[...], preferred_element_type=F32) + bx_ref[...])
    lam = lam_ref[...]
    log_sig = jnp.minimum(lam, 0.0) - jnp.log1p(jnp.exp(-jnp.abs(lam)))
    log_a = LRU_C * r * log_sig
    a = jnp.exp(log_a)
    th = jnp.tanh(log_a)
    one_minus_a2 = -2.0 * th / (1.0 - th)
    u = jnp.sqrt(one_minus_a2) * i * xc

    for s in (1, 2, 4):
        a_s = jnp.where(row >= s, pltpu.roll(a, s, 0), 1.0)
        u_s = jnp.where(row >= s, pltpu.roll(u, s, 0), 0.0)
        u = a * u_s + u
        a = a * a_s
    s = 8
    while s < rows:
        a_s = jnp.concatenate([jnp.ones((s, W), F32), a[:rows - s, :]], axis=0)
        u_s = jnp.concatenate([jnp.zeros((s, W), F32), u[:rows - s, :]], axis=0)
        u = a * u_s + u
        a = a * a_s
        s *= 2
    h = a * h_ref[0:1, :] + u
    h_ref[...] = jnp.broadcast_to(h[rows - 1:rows, :], h_ref.shape)
    o_ref[...] = (h * jax.nn.gelu(gate)).astype(o_ref.dtype)


def _lru(rec, conv_w, conv_b, wa_bd, wx_bd, ba, bx, lam, batch, seq, rows=256):
    T = rec.shape[0]
    nblk = seq // rows
    W = LRU_WIDTH
    const = lambda shape: pl.BlockSpec(shape, lambda b, i: (0, 0))
    return pl.pallas_call(
        functools.partial(_lru_kernel, rows=rows),
        grid=(batch, nblk),
        in_specs=[pl.BlockSpec((rows, 2 * W), lambda b, i: (b * nblk + i, 2)),
                  const((CONV_WIDTH, W)), const((1, W)), const((W, W)), const((W, W)),
                  const((1, W)), const((1, W)), const((1, W))],
        out_specs=pl.BlockSpec((rows, W), lambda b, i: (b * nblk + i, 0)),
        out_shape=jax.ShapeDtypeStruct((T, W), BF16),
        scratch_shapes=[pltpu.VMEM((8, W), F32), pltpu.VMEM((8, W), F32)],
        compiler_params=_params("parallel", "arbitrary"),
        name="rglru",
    )(rec, conv_w, conv_b.reshape(1, -1), wa_bd, wx_bd, ba.reshape(1, -1), bx.reshape(1, -1),
      lam.reshape(1, -1))


def _attn_kernel(q_ref, k_ref, v_ref, lq1_ref, lk1_ref, lq2_ref, lk2_ref, ng_ref, o_ref,
                 vt_ref, qm_ref, s0_ref, s1_ref, m_ref, l_ref, acc_ref, *, tq, tk, qs, lam_init, seq):
    d = DA_HEAD_DIM
    qi = pl.program_id(2)

    @pl.when(qi == 0)
    def _():
        def transpose_values(j, carry):
            r0 = pl.multiple_of(j * tk, tk)
            vt_ref[j] = v_ref[pl.ds(r0, tk), :].astype(F32).T.astype(BF16)
            return carry
        lax.fori_loop(0, seq // tk, transpose_values, 0)

    q = q_ref[...]
    lane = lax.broadcasted_iota(jnp.int32, q.shape, 1)
    qm_ref[0] = jnp.where(lane < d, q, jnp.zeros_like(q))
    qm_ref[1] = jnp.where(lane >= d, q, jnp.zeros_like(q))
    m_ref[...] = jnp.full_like(m_ref, -jnp.inf)
    l_ref[...] = jnp.zeros_like(l_ref)
    acc_ref[...] = jnp.zeros_like(acc_ref)
    s_refs = (s0_ref, s1_ref)

    def scores(j, c):
        k0 = pl.multiple_of(j * tk, tk)
        s_refs[c][...] = lax.dot_general(k_ref[pl.ds(k0, tk), :], qm_ref[c], (((1,), (1,)), ((), ())),
                                         preferred_element_type=F32)

    def softmax_pv(j, c, masked):
        for st in range(tq // qs):
            cols = pl.ds(st * qs, qs)
            nk = min((st + 1) * qs, tk) if masked else tk
            vt = vt_ref[j, :, 0:nk]
            s = s_refs[c][0:nk, cols]
            if masked:
                kpos = lax.broadcasted_iota(jnp.int32, (nk, qs), 0)
                qpos = st * qs + lax.broadcasted_iota(jnp.int32, (nk, qs), 1)
                s = jnp.where(kpos <= qpos, s, -jnp.inf)
            m_old = m_ref[c, :, cols]
            m_new = jnp.maximum(m_old, jnp.max(s, axis=0, keepdims=True))
            p = jnp.exp2(s - m_new)
            alpha = jnp.exp2(m_old - m_new)
            l_ref[c, :, cols] = alpha * l_ref[c, :, cols] + jnp.sum(p, axis=0, keepdims=True)
            acc_ref[c, :, cols] = (alpha * acc_ref[c, :, cols]
                                   + jnp.dot(vt, p.astype(BF16), preferred_element_type=F32))
            m_ref[c, :, cols] = m_new

    def full_block(j, carry):
        scores(j, 1)
        softmax_pv(j, 0, False)
        scores(j + 1, 0)
        softmax_pv(j, 1, False)
        return carry

    scores(0, 0)
    lax.fori_loop(0, qi, full_block, 0)
    scores(qi, 1)
    softmax_pv(qi, 0, True)
    softmax_pv(qi, 1, True)

    lam =(jnp.exp(jnp.sum(lq1_ref[...] * lk1_ref[...], axis=-1, keepdims=True))
           - jnp.exp(jnp.sum(lq2_ref[...] * lk2_ref[...], axis=-1, keepdims=True)) + lam_init)
    o_t = acc_ref[0] * (1.0 / l_ref[0]) - lam * (acc_ref[1] * (1.0 / l_ref[1]))
    ms = jnp.mean(o_t * o_t, axis=0, keepdims=True)
    o_t = o_t * lax.rsqrt(ms + EPS) * ng_ref[...] * (1.0 - lam_init)
    o_ref[...] = o_t.T.astype(o_ref.dtype)


def _attention(att, lq1, lk1, lq2, lk2, norm_g, layer, batch, seq, tq=512, qs=256):
    T = att.shape[0]
    tk = tq
    nq = seq // tq
    hw = 2 * DA_HEAD_DIM
    lam_init = 0.8 - 0.6 * math.exp(-0.3 * layer)
    vec = lambda: pl.BlockSpec((1, DA_HEAD_DIM), lambda b, h, i: (0, 0))
    return pl.pallas_call(
        functools.partial(_attn_kernel, tq=tq, tk=tk, qs=qs, lam_init=lam_init, seq=seq),
        grid=(batch, DA_HEADS, nq),
        in_specs=[pl.BlockSpec((tq, hw), lambda b, h, i: (b * nq + i, h)),
                  pl.BlockSpec((seq, hw), lambda b, h, i: (b, DA_HEADS + h)),
                  pl.BlockSpec((seq, hw), lambda b, h, i: (b, 2 * DA_HEADS + h)),
                  vec(), vec(), vec(), vec(),
                  pl.BlockSpec((hw, 1), lambda b, h, i: (h, 0))],
        out_specs=pl.BlockSpec((tq, hw), lambda b, h, i: (b * nq + i, h)),
        out_shape=jax.ShapeDtypeStruct((T, DA_WIDTH), BF16),
        scratch_shapes=[pltpu.VMEM((seq // tk, hw, tk), BF16), pltpu.VMEM((2, tq, hw), BF16),
                        pltpu.VMEM((tk, tq), F32), pltpu.VMEM((tk, tq), F32),
                        pltpu.VMEM((2, 1, tq), F32), pltpu.VMEM((2, 1, tq), F32),
                        pltpu.VMEM((2, hw, tq), F32)],
        compiler_params=_params("arbitrary", "arbitrary", "arbitrary"),
        name="diff_attn",
    )(att, att, att, lq1.reshape(1, -1), lk1.reshape(1, -1), lq2.reshape(1, -1),
      lk2.reshape(1, -1), norm_g.reshape(-1, 1))


def _out_proj_kernel(hg_ref, lru_ref, da_ref, w_ref, x_ref, g_ref, b_ref, o_ref):
    h = jnp.dot(hg_ref[...], w_ref[0:HG_WIDTH, :], preferred_element_type=F32)
    h = h + jnp.dot(lru_ref[...], w_ref[HG_WIDTH:HG_WIDTH + LRU_WIDTH, :], preferred_element_type=F32)
    h = h + jnp.dot(da_ref[...], w_ref[HG_WIDTH + LRU_WIDTH:, :], preferred_element_type=F32)
    o_ref[...] = _layer_norm_rows(DN_ALPHA * x_ref[...] + h, g_ref[...], b_ref[...])


def _out_proj_ln(o_hg, o_lru, o_da, w_bf16, x, g, b, tm=512):
    T = x.shape[0]
    rows = lambda w: pl.BlockSpec((tm, w), lambda i: (i, 0))
    const = lambda shape: pl.BlockSpec(shape, lambda i: (0, 0))
    return pl.pallas_call(
        _out_proj_kernel,
        grid=(T // tm,),
        in_specs=[rows(HG_WIDTH), rows(LRU_WIDTH), rows(DA_WIDTH), const((D_MODEL, D_MODEL)),
                  rows(D_MODEL), const((1, D_MODEL)), const((1, D_MODEL))],
        out_specs=rows(D_MODEL),
        out_shape=jax.ShapeDtypeStruct((T, D_MODEL), F32),
        compiler_params=_params("parallel"),
        name="out_proj_ln",
    )(o_hg, o_lru, o_da, w_bf16, x, g.reshape(1, -1), b.reshape(1, -1))


def _ffn_kernel(x_ref, wg_ref, wu_ref, wd_ref, g_ref, b_ref, o_ref, xb_ref, acc_ref):
    f = pl.program_id(1)

    @pl.when(f == 0)
    def _():
        xb_ref[...] = x_ref[...].astype(BF16)
        acc_ref[...] = jnp.zeros_like(acc_ref)

    xb = xb_ref[...]
    gate = jnp.dot(xb, wg_ref[...], preferred_element_type=F32)
    up = jnp.dot(xb, wu_ref[...], preferred_element_type=F32)
    hmid = (gate * jax.nn.sigmoid(gate) * up).astype(BF16)
    acc_ref[...] += jnp.dot(hmid, wd_ref[...], preferred_element_type=F32)

    @pl.when(f == pl.num_programs(1) - 1)
    def _():
        o_ref[...] = _layer_norm_rows(DN_ALPHA * x_ref[...] + acc_ref[...], g_ref[...], b_ref[...])


def _ffn_ln(x, wg, wu, wd, g, b, tm=1024, tf=512):
    T = x.shape[0]
    return pl.pallas_call(
        _ffn_kernel,
        grid=(T // tm, D_FF // tf),
        in_specs=[pl.BlockSpec((tm, D_MODEL), lambda i, f: (i, 0)),
                  pl.BlockSpec((D_MODEL, tf), lambda i, f: (0, f)),
                  pl.BlockSpec((D_MODEL, tf), lambda i, f: (0, f)),
                  pl.BlockSpec((tf, D_MODEL), lambda i, f: (f, 0)),
                  pl.BlockSpec((1, D_MODEL), lambda i, f: (0, 0)),
                  pl.BlockSpec((1, D_MODEL), lambda i, f: (0, 0))],
        out_specs=pl.BlockSpec((tm, D_MODEL), lambda i, f: (i, 0)),
        out_shape=jax.ShapeDtypeStruct((T, D_MODEL), F32),
        scratch_shapes=[pltpu.VMEM((tm, D_MODEL), BF16), pltpu.VMEM((tm, D_MODEL), F32)],
        compiler_params=_params("parallel", "arbitrary"),
        name="ffn_ln",
    )(x, wg, wu, wd, g.reshape(1, -1), b.reshape(1, -1))


ROUTER_LANES = 128
META_I1, META_I2, META_R1, META_R2 = 0, 1, 2, 3


def _router_kernel(x_ref, rw_ref, tri_ref, meta_ref, gate_ref, cnt_ref, carry_ref):
    @pl.when(pl.program_id(0) == 0)
    def _():
        carry_ref[...] = jnp.zeros_like(carry_ref)

    logits = jnp.dot(x_ref[...], rw_ref[...], preferred_element_type=F32,
                     precision=lax.Precision.HIGHEST)
    lane = lax.broadcasted_iota(jnp.int32, logits.shape, 1)
    neg = -jnp.inf
    l1 = jnp.where(lane < N_EXPERTS, logits, neg)
    v1 = jnp.max(l1, axis=-1, keepdims=True)
    i1 = jnp.min(jnp.where(l1 == v1, lane, ROUTER_LANES), axis=-1, keepdims=True)
    l2 = jnp.where(lane == i1, neg, l1)
    v2 = jnp.max(l2, axis=-1, keepdims=True)
    i2 = jnp.min(jnp.where(l2 == v2, lane, ROUTER_LANES), axis=-1, keepdims=True)
    e2 = jnp.exp(v2 - v1)
    g1 = 1.0 / (1.0 + e2)
    g2 = e2 / (1.0 + e2)
    member = jnp.where((lane == i1) | (lane == i2), 1.0, 0.0)
    rank = jnp.dot(tri_ref[...], member.astype(BF16), preferred_element_type=F32) + carry_ref[0:1, :]
    r1 = jnp.sum(jnp.where(lane == i1, rank, 0.0), axis=-1, keepdims=True).astype(jnp.int32)
    r2 = jnp.sum(jnp.where(lane == i2, rank, 0.0), axis=-1, keepdims=True).astype(jnp.int32)
    meta = jnp.where(lane == META_I1, i1, 0) + jnp.where(lane == META_I2, i2, 0)
    meta = meta + jnp.where(lane == META_R1, r1, 0) + jnp.where(lane == META_R2, r2, 0)
    meta_ref[...] = meta
    gate_ref[...] = jnp.where(lane == 0, g1, 0.0) + jnp.where(lane == 1, g2, 0.0)
    carry = carry_ref[...] + jnp.sum(member, axis=0, keepdims=True)
    carry_ref[...] = carry
    cnt_ref[...] = carry


def _router(x, router_pad, tm=1024):
    T = x.shape[0]
    tri = (jnp.arange(tm)[:, None] > jnp.arange(tm)[None, :]).astype(BF16)
    return pl.pallas_call(
        _router_kernel,
        grid=(T // tm,),
        in_specs=[pl.BlockSpec((tm, D_MODEL), lambda i: (i, 0)),
                  pl.BlockSpec((D_MODEL, ROUTER_LANES), lambda i: (0, 0)),
                  pl.BlockSpec((tm, tm), lambda i: (0, 0))],
        out_specs=[pl.BlockSpec((tm, ROUTER_LANES), lambda i: (i, 0)),
                   pl.BlockSpec((tm, ROUTER_LANES), lambda i: (i, 0)),
                   pl.BlockSpec((8, ROUTER_LANES), lambda i: (0, 0))],
        out_shape=[jax.ShapeDtypeStruct((T, ROUTER_LANES), jnp.int32),
                   jax.ShapeDtypeStruct((T, ROUTER_LANES), F32),
                   jax.ShapeDtypeStruct((8, ROUTER_LANES), F32)],
        scratch_shapes=[pltpu.VMEM((8, ROUTER_LANES), F32)],
        compiler_params=_params("arbitrary"),
        name="moe_router",
    )(x, router_pad, tri)


def _dispatch_tables(meta, cnt, tm):
    T = meta.shape[0]
    counts = cnt[0, :N_EXPERTS].astype(jnp.int32)
    padded = ((counts + tm - 1) // tm) * tm
    ends = jnp.cumsum(padded)
    off = ends - padded
    pos1 = off[meta[:, META_I1]] + meta[:, META_R1]
    pos2 = off[meta[:, META_I2]] + meta[:, META_R2]
    n_tiles = 2 * T // tm + N_EXPERTS + 1
    tok = jnp.arange(T, dtype=jnp.int32)
    code_of = jnp.full((n_tiles * tm,), 2 * T, jnp.int32).at[jnp.concatenate([pos1, pos2])].set(
        jnp.concatenate([tok, tok + T]))
    tile_start = jnp.arange(n_tiles, dtype=jnp.int32) * tm
    tile_expert = jnp.minimum(jnp.sum((tile_start[:, None] >= ends[None, :]).astype(jnp.int32), axis=1),
                              N_EXPERTS - 1)
    n_used = (ends[-1] // tm).astype(jnp.int32).reshape(1)
    return code_of.reshape(n_tiles, 1, tm), tile_expert, n_used


def _expert_kernel(te_ref, nu_ref, cur_ref, nxt_ref, prv_ref, x_hbm, wg_ref, wu_ref, wd_ref, y_hbm,
                   xbuf_ref, xb_ref, acc_ref, sem_g, sem_s, *, tm, nf, n_tok):
    i = pl.program_id(0)
    f = pl.program_id(1)
    n_used = nu_ref[0]
    slot = i % 2
    other = 1 - slot
    pad_code = 2 * n_tok
    rows_per_step = tm // nf

    def gather(code_ref, r, buf):
        code = code_ref[0, 0, r]
        tok = jnp.where(code >= pad_code, 0, jnp.where(code >= n_tok, code - n_tok, code))
        return pltpu.make_async_copy(x_hbm.at[pl.ds(tok, 1)], xbuf_ref.at[buf, pl.ds(r, 1)],
                                     sem_g.at[buf])

    def scatter(r, buf, to_dump):
        code = prv_ref[0, 0, r]
        dest = jnp.where(to_dump | (code >= pad_code), pad_code + buf * tm + r, code)
        return pltpu.make_async_copy(acc_ref.at[buf, pl.ds(r, 1)], y_hbm.at[pl.ds(dest, 1)], sem_s)

    def wait_gather(buf):
        pltpu.make_async_copy(x_hbm.at[pl.ds(0, tm)], xbuf_ref.at[buf], sem_g.at[buf]).wait()

    def wait_scatter(buf):
        pltpu.make_async_copy(acc_ref.at[buf], y_hbm.at[pl.ds(0, tm)], sem_s).wait()

    @pl.when((i < n_used) & (f == 0))
    def _():
        @pl.when(i == 0)
        def _():
            def issue(r, carry):
                gather(cur_ref, r, 0).start()
                return carry
            lax.fori_loop(0, tm, issue, 0, unroll=8)
            acc_ref[1] = jnp.zeros((tm, D_MODEL), F32)
            clear = pltpu.make_async_copy(acc_ref.at[1], y_hbm.at[pl.ds(pad_code, tm)], sem_s)
            clear.start()
            clear.wait()

        wait_gather(slot)
        xb_ref[...] = xbuf_ref[slot].astype(BF16)
        acc_ref[slot] = jnp.zeros((tm, D_MODEL), F32)

    @pl.when(i < n_used)
    def _():
        for rr in range(rows_per_step):
            r = f * rows_per_step + rr
            gather(nxt_ref, r, other).start()
            scatter(r, other, i == 0).start()
        xb = xb_ref[...]
        gate = jnp.dot(xb, wg_ref[0], preferred_element_type=F32)
        up = jnp.dot(xb, wu_ref[0], preferred_element_type=F32)
        hmid = (gate * jax.nn.sigmoid(gate) * up).astype(BF16)
        acc_ref[slot] += jnp.dot(hmid, wd_ref[0], preferred_element_type=F32)

        @pl.when(f == nf - 1)
        def _():
            wait_scatter(other)

    @pl.when((i == n_used) & (f == 0))
    def _():
        wait_gather(slot)

        def issue(r, carry):
            scatter(r, other, False).start()
            return carry
        lax.fori_loop(0, tm, issue, 0, unroll=8)
        wait_scatter(other)


def _experts(x, code_of, tile_expert, n_used, wg, wu, wd, tm, tf=1792):
    n_tok = x.shape[0]
    n_tiles = code_of.shape[0]
    nf = D_FF // tf
    fidx = lambda i, f, nu: jnp.where(i < nu[0], f, nf - 1)
    codes = lambda shift: pl.BlockSpec(
        (1, 1, tm), lambda i, f, te, nu: (jnp.clip(i + shift, 0, n_tiles - 1), 0, 0),
        memory_space=pltpu.SMEM)
    grid_spec = pltpu.PrefetchScalarGridSpec(
        num_scalar_prefetch=2,
        grid=(n_tiles, nf),
        in_specs=[codes(0), codes(1), codes(-1),
                  pl.BlockSpec(memory_space=pl.ANY),
                  pl.BlockSpec((1, D_MODEL, tf), lambda i, f, te, nu: (te[i], 0, fidx(i, f, nu))),
                  pl.BlockSpec((1, D_MODEL, tf), lambda i, f, te, nu: (te[i], 0, fidx(i, f, nu))),
                  pl.BlockSpec((1, tf, D_MODEL), lambda i, f, te, nu: (te[i], fidx(i, f, nu), 0))],
        out_specs=pl.BlockSpec(memory_space=pl.ANY),
        scratch_shapes=[pltpu.VMEM((2, tm, D_MODEL), F32), pltpu.VMEM((tm, D_MODEL), BF16),
                        pltpu.VMEM((2, tm, D_MODEL), F32),
                        pltpu.SemaphoreType.DMA((2,)), pltpu.SemaphoreType.DMA(())],
    )
    return pl.pallas_call(
        functools.partial(_expert_kernel, tm=tm, nf=nf, n_tok=n_tok),
        grid_spec=grid_spec,
        out_shape=jax.ShapeDtypeStruct((2 * n_tok + 2 * tm, D_MODEL), F32),
        compiler_params=_params("arbitrary", "arbitrary"),
        name="moe_experts",
    )(tile_expert, n_used, code_of, code_of, code_of, x, wg, wu, wd)


def _combine_kernel(x_ref, y1_ref, y2_ref, gate_ref, g_ref, b_ref, o_ref):
    gates = gate_ref[...]
    y = gates[:, 0:1] * y1_ref[...] + gates[:, 1:2] * y2_ref[...]
    o_ref[...] = _layer_norm_rows(DN_ALPHA * x_ref[...] + y, g_ref[...], b_ref[...])


def _combine_ln(x, y, gates, g, b, tm=512):
    T = x.shape[0]
    nt = T // tm
    return pl.pallas_call(
        _combine_kernel,
        grid=(nt,),
        in_specs=[pl.BlockSpec((tm, D_MODEL), lambda i: (i, 0)),
                  pl.BlockSpec((tm, D_MODEL), lambda i: (i, 0)),
                  pl.BlockSpec((tm, D_MODEL), lambda i: (nt + i, 0)),
                  pl.BlockSpec((tm, ROUTER_LANES), lambda i: (i, 0)),
                  pl.BlockSpec((1, D_MODEL), lambda i: (0, 0)),
                  pl.BlockSpec((1, D_MODEL), lambda i: (0, 0))],
        out_specs=pl.BlockSpec((tm, D_MODEL), lambda i: (i, 0)),
        out_shape=jax.ShapeDtypeStruct((T, D_MODEL), F32),
        compiler_params=_params("parallel"),
        name="moe_combine_ln",
    )(x, y, y, gates, g.reshape(1, -1), b.reshape(1, -1))


def _moe_ln(x, router_w, wg, wu, wd, g, b, tm=512):
    router_pad = jnp.pad(router_w, ((0, 0), (0, ROUTER_LANES - N_EXPERTS)))
    meta, gates, cnt = _router(x, router_pad)
    code_of, tile_expert, n_used = _dispatch_tables(meta, cnt, tm)
    y = _experts(x, code_of, tile_expert, n_used, wg, wu, wd, tm)
    return _combine_ln(x, y, gates, g, b)


def _block_diag(w):
    nb, n, _ = w.shape
    eye = jnp.eye(nb, dtype=w.dtype)
    return (eye[:, None, :, None] * w[:, :, None, :]).reshape(nb * n, nb * n)


def kernel(x, w_in, w_out, hg_lb_logits, hg_norm_g, lru_conv_w, lru_conv_b, lru_wa, lru_ba, lru_wx, lru_bx, lru_lambda, da_lq1, da_lk1, da_lq2, da_lk2, da_norm_g, ln1_g, ln1_b, ln2_g, ln2_b, ffn_wg, ffn_wu, ffn_wd, router_w, moe_wg, moe_wu, moe_wd):
    batch, seq, _ = x.shape
    depth = w_in.shape[0]
    xf = x.reshape(batch * seq, D_MODEL)
    lb_p = jax.nn.softmax(hg_lb_logits.astype(F32), axis=0)
    lb_all = jnp.cumsum(lb_p, axis=0) - lb_p[0:1]
    for l in range(depth):
        rec, att = _in_proj(xf, w_in[l].astype(BF16))
        o_hg = _hgrn(rec, lb_all[l], hg_norm_g[l], batch, seq)
        o_lru = _lru(rec, lru_conv_w[l], lru_conv_b[l], _block_diag(lru_wa[l]).astype(BF16),
                     _block_diag(lru_wx[l]).astype(BF16), lru_ba[l], lru_bx[l], lru_lambda[l],
                     batch, seq)
        o_da = _attention(att, da_lq1[l], da_lk1[l], da_lq2[l], da_lk2[l], da_norm_g[l], l,
                          batch, seq)
        xf = _out_proj_ln(o_hg, o_lru, o_da, w_out[l].astype(BF16), xf, ln1_g[l], ln1_b[l])
        j = l // 2
        if l % 2 == 0:
            xf = _ffn_ln(xf, ffn_wg[j].astype(BF16), ffn_wu[j].astype(BF16), ffn_wd[j].astype(BF16),
                         ln2_g[l], ln2_b[l])
        else:
            xf = _moe_ln(xf, router_w[j], moe_wg[j].astype(BF16), moe_wu[j].astype(BF16),
                         moe_wd[j].astype(BF16), ln2_g[l], ln2_b[l])
    return xf.reshape(batch, seq, D_MODEL)
```

```python
import functools
import math

import jax
import jax.numpy as jnp
from jax import lax
from jax.experimental import pallas as pl
from jax.experimental.pallas import tpu as pltpu

D_MODEL = 1024
DEPTH = 2
HG_HEADS = 4
HG_KEY_DIM = 64
HG_WIDTH = 256
HG_CHUNK = 16
LRU_WIDTH = 256
LRU_BLOCKS = 4
CONV_WIDTH = 4
LRU_C = 8.0
DA_WIDTH = 512
DA_HEADS = 4
DA_HEAD_DIM = 64
D_FF = 3584
N_EXPERTS = 8
DN_ALPHA = (2.0 * DEPTH) ** 0.25
EPS = 1e-5
REC_WIDTH = 4 * HG_WIDTH + 2 * LRU_WIDTH
ATT_WIDTH = 3 * DA_WIDTH

V7X_VMEM_BYTES = 64 * 1024 * 1024
VMEM_LIMIT = 48 * 1024 * 1024

BF16 = jnp.bfloat16
F32 = jnp.float32


def _params(*semantics):
    return pltpu.CompilerParams(dimension_semantics=semantics, vmem_limit_bytes=VMEM_LIMIT)


def _layer_norm_rows(y, g, b):
    mu = jnp.mean(y, axis=-1, keepdims=True)
    yc = y - mu
    var = jnp.mean(yc * yc, axis=-1, keepdims=True)
    return yc * lax.rsqrt(var + EPS) * g + b


Q_SCALE = DA_HEAD_DIM ** -0.5 * math.log2(math.e)


def _in_proj_kernel(x_ref, w_ref, rec_ref, att_ref):
    xb = x_ref[...].astype(BF16)
    rec_ref[...] = jnp.dot(xb, w_ref[:, :REC_WIDTH], preferred_element_type=F32)
    q = jnp.dot(xb, w_ref[:, REC_WIDTH:REC_WIDTH + DA_WIDTH], preferred_element_type=F32)
    att_ref[:, :DA_WIDTH] = (q * Q_SCALE).astype(BF16)
    att_ref[:, DA_WIDTH:] = jnp.dot(xb, w_ref[:, REC_WIDTH + DA_WIDTH:],
                                    preferred_element_type=F32).astype(BF16)


def _in_proj(x, w_bf16, tm=512):
    T = x.shape[0]
    return pl.pallas_call(
        _in_proj_kernel,
        grid=(T // tm,),
        in_specs=[pl.BlockSpec((tm, D_MODEL), lambda i: (i, 0)),
                  pl.BlockSpec((D_MODEL, REC_WIDTH + ATT_WIDTH), lambda i: (0, 0))],
        out_specs=[pl.BlockSpec((tm, REC_WIDTH), lambda i: (i, 0)),
                   pl.BlockSpec((tm, ATT_WIDTH), lambda i: (i, 0))],
        out_shape=[jax.ShapeDtypeStruct((T, REC_WIDTH), F32),
                   jax.ShapeDtypeStruct((T, ATT_WIDTH), BF16)],
        compiler_params=_params("parallel"),
        name="in_proj",
    )(x, w_bf16)


def _hgrn_kernel(rec_ref, lb_ref, ng_ref, gmat_ref, sel_ref, bmask_ref, o_ref,
                 st_ref, q_s, kk_s, b_s, bl_s, qd_s, kd_s, o_s, *, rows):
    @pl.when(pl.program_id(1) == 0)
    def _():
        st_ref[...] = jnp.zeros_like(st_ref)

    C = HG_CHUNK
    W = HG_WIDTH
    nc = rows // C
    lb = lb_ref[...]
    gmat = gmat_ref[...]
    sel = sel_ref[...]

    qr = rec_ref[:, 0:W]
    z = rec_ref[:, W:2 * W]
    rin = lax.broadcasted_iota(jnp.int32, (rows, W), 0) % C
    logf = jnp.log(lb + (1.0 - lb) * jax.nn.sigmoid(z))
    kk = (1.0 - lb) * jax.nn.sigmoid(-z)
    q = qr * jax.nn.sigmoid(qr)
    b = logf
    for s in (1, 2, 4, 8):
        b = b + jnp.where(rin >= s, pltpu.roll(b, s, 0), 0.0)
    r = jnp.where(rin < C - 1, pltpu.roll(logf, rows - 1, 0), 0.0)
    for s in (1, 2, 4, 8):
        r = r + jnp.where(rin < C - s, pltpu.roll(r, rows - s, 0), 0.0)
    q_s[...] = q
    kk_s[...] = kk
    b_s[...] = b
    bl_s[...] = b + r
    qd_s[...] = (q * jnp.exp(b)).astype(BF16)
    kd_s[...] = (kk * jnp.exp(r)).astype(BF16)

    row = lax.broadcasted_iota(jnp.int32, (C, W), 0)

    def chunk(c, carry):
        r0 = pl.multiple_of(c * C, C)
        q = q_s[pl.ds(r0, C), :]
        kk = kk_s[pl.ds(r0, C), :]
        b = b_s[pl.ds(r0, C), :]
        v = rec_ref[pl.ds(r0, C), 2 * W:3 * W]
        w_rows = []
        for t in range(C):
            rel = b[t:t + 1, :] - b
            dec = jnp.exp(jnp.where(row <= t, rel, -jnp.inf))
            w_rows.append(((q[t:t + 1, :] * kk) * dec).astype(BF16))
        w2 = jnp.concatenate(w_rows, axis=0)
        a = jnp.dot(w2, gmat, preferred_element_type=F32)
        p = a * jnp.concatenate([v] * C, axis=0)
        o_intra = jnp.dot(sel, p.astype(BF16), preferred_element_type=F32)
        u_t = lax.dot_general(v.astype(BF16), kd_s[pl.ds(r0, C), :], (((0,), (0,)), ((), ())),
                              preferred_element_type=F32)
        st = st_ref[...]
        o_inter = lax.dot_general(qd_s[pl.ds(r0, C), :], st.astype(BF16), (((1,), (1,)), ((), ())),
                                  preferred_element_type=F32)
        o_s[pl.ds(r0, C), :] = o_intra + o_inter
        st_ref[...] = st * jnp.exp(bl_s[pl.ds(r0, 1), :]) + u_t * bmask_ref[...]
        return carry

    lax.fori_loop(0, nc, chunk, 0, unroll=4)

    o = o_s[...]
    g = rec_ref[:, 3 * W:4 * W]
    ms = jnp.dot(o * o, gmat.astype(F32), preferred_element_type=F32,
                 precision=lax.Precision.HIGHEST) * (1.0 / HG_KEY_DIM)
    out = o * lax.rsqrt(ms + EPS) * ng_ref[...] * (g * jax.nn.sigmoid(g))
    o_ref[...] = out.astype(o_ref.dtype)


def _hgrn(rec, lb, norm_g, batch, seq, rows=256):
    T = rec.shape[0]
    nblk = seq // rows
    head = jnp.arange(HG_WIDTH) // HG_KEY_DIM
    same_head = head[:, None] == head[None, :]
    gmat = same_head.astype(BF16)
    bmask = same_head.astype(F32)
    sel = (jnp.arange(HG_CHUNK)[:, None] == (jnp.arange(HG_CHUNK * HG_CHUNK) // HG_CHUNK)[None, :]).astype(BF16)
    const = lambda shape: pl.BlockSpec(shape, lambda b, i: (0, 0))
    return pl.pallas_call(
        functools.partial(_hgrn_kernel, rows=rows),
        grid=(batch, nblk),
        in_specs=[pl.BlockSpec((rows, 4 * HG_WIDTH), lambda b, i: (b * nblk + i, 0)),
                  const((1, HG_WIDTH)), const((1, HG_WIDTH)),
                  const((HG_WIDTH, HG_WIDTH)), const((HG_CHUNK, HG_CHUNK * HG_CHUNK)),
                  const((HG_WIDTH, HG_WIDTH))],
        out_specs=pl.BlockSpec((rows, HG_WIDTH), lambda b, i: (b * nblk + i, 0)),
        out_shape=jax.ShapeDtypeStruct((T, HG_WIDTH), BF16),
        scratch_shapes=[pltpu.VMEM((HG_WIDTH, HG_WIDTH), F32)]
        + [pltpu.VMEM((rows, HG_WIDTH), F32)] * 4
        + [pltpu.VMEM((rows, HG_WIDTH), BF16)] * 2
        + [pltpu.VMEM((rows, HG_WIDTH), F32)],
        compiler_params=_params("parallel", "arbitrary"),
        name="hgrn2",
    )(rec, lb.reshape(1, -1), norm_g.reshape(1, -1), gmat, sel, bmask)


def _lru_kernel(rec_ref, cw_ref, cb_ref, wa_ref, wx_ref, ba_ref, bx_ref, lam_ref, o_ref,
                xprev_ref, h_ref, *, rows):
    @pl.when(pl.program_id(1) == 0)
    def _():
        xprev_ref[...] = jnp.zeros_like(xprev_ref)
        h_ref[...] = jnp.zeros_like(h_ref)

    W = LRU_WIDTH
    x = rec_ref[:, 0:W]
    gate = rec_ref[:, W:2 * W]
    row = lax.broadcasted_iota(jnp.int32, (rows, W), 0)
    xp = xprev_ref[...]
    tail = jnp.zeros((rows - 8, W), F32)
    xc = cb_ref[...] + cw_ref[CONV_WIDTH - 1:CONV_WIDTH, :] * x
    for j in range(1, CONV_WIDTH):
        prev = jnp.concatenate([pltpu.roll(xp, j, 0), tail], axis=0)
        xs = jnp.where(row >= j, pltpu.roll(x, j, 0), prev)
        xc = xc + cw_ref[CONV_WIDTH - 1 - j:CONV_WIDTH - j, :] * xs
    xprev_ref[...] = x[rows - 8:rows, :]

    xcb = xc.astype(BF16)
    r = jax.nn.sigmoid(jnp.dot(xcb, wa_ref[...], preferred_element_type=F32) + ba_ref[...])
    i = jax.nn.sigmoid(jnp.dot(xcb, wx_ref[...], preferred_element_type=F32) + bx_ref[...])
    lam = lam_ref[...]
    log_sig = jnp.minimum(lam, 0.0) - jnp.log1p(jnp.exp(-jnp.abs(lam)))
    log_a = LRU_C * r * log_sig
    a = jnp.exp(log_a)
    th = jnp.tanh(log_a)
    one_minus_a2 = -2.0 * th / (1.0 - th)
    u = jnp.sqrt(one_minus_a2) * i * xc

    for s in (1, 2, 4):
        a_s = jnp.where(row >= s, pltpu.roll(a, s, 0), 1.0)
        u_s = jnp.where(row >= s, pltpu.roll(u, s, 0), 0.0)
        u = a * u_s + u
        a = a * a_s
    s = 8
    while s < rows:
        a_s = jnp.concatenate([jnp.ones((s, W), F32), a[:rows - s, :]], axis=0)
        u_s = jnp.concatenate([jnp.zeros((s, W), F32), u[:rows - s, :]], axis=0)
        u = a * u_s + u
        a = a * a_s
        s *= 2
    h = a * h_ref[0:1, :] + u
    h_ref[...] = jnp.broadcast_to(h[rows - 1:rows, :], h_ref.shape)
    o_ref[...] = (h * jax.nn.gelu(gate)).astype(o_ref.dtype)


def _lru(rec, conv_w, conv_b, wa_bd, wx_bd, ba, bx, lam, batch, seq, rows=256):
    T = rec.shape[0]
    nblk = seq // rows
    W = LRU_WIDTH
    const = lambda shape: pl.BlockSpec(shape, lambda b, i: (0, 0))
    return pl.pallas_call(
        functools.partial(_lru_kernel, rows=rows),
        grid=(batch, nblk),
        in_specs=[pl.BlockSpec((rows, 2 * W), lambda b, i: (b * nblk + i, 2)),
                  const((CONV_WIDTH, W)), const((1, W)), const((W, W)), const((W, W)),
                  const((1, W)), const((1, W)), const((1, W))],
        out_specs=pl.BlockSpec((rows, W), lambda b, i: (b * nblk + i, 0)),
        out_shape=jax.ShapeDtypeStruct((T, W), BF16),
        scratch_shapes=[pltpu.VMEM((8, W), F32), pltpu.VMEM((8, W), F32)],
        compiler_params=_params("parallel", "arbitrary"),
        name="rglru",
    )(rec, conv_w, conv_b.reshape(1, -1), wa_bd, wx_bd, ba.reshape(1, -1), bx.reshape(1, -1),
      lam.reshape(1, -1))


def _attn_kernel(q_ref, k_ref, v_ref, lq1_ref, lk1_ref, lq2_ref, lk2_ref, ng_ref, o_ref,
                 vt_ref, qm_ref, s0_ref, s1_ref, m_ref, l_ref, acc_ref, *, tq, tk, qs, lam_init, seq):
    d = DA_HEAD_DIM
    qi = pl.program_id(2)

    @pl.when(qi == 0)
    def _():
        def transpose_values(j, carry):
            r0 = pl.multiple_of(j * tk, tk)
            vt_ref[j] = v_ref[pl.ds(r0, tk), :].astype(F32).T.astype(BF16)
            return carry
        lax.fori_loop(0, seq // tk, transpose_values, 0)

    q = q_ref[...]
    lane = lax.broadcasted_iota(jnp.int32, q.shape, 1)
    qm_ref[0] = jnp.where(lane < d, q, jnp.zeros_like(q))
    qm_ref[1] = jnp.where(lane >= d, q, jnp.zeros_like(q))
    m_ref[...] = jnp.full_like(m_ref, -jnp.inf)
    l_ref[...] = jnp.zeros_like(l_ref)
    acc_ref[...] = jnp.zeros_like(acc_ref)
    s_refs = (s0_ref, s1_ref)

    def scores(j, c):
        k0 = pl.multiple_of(j * tk, tk)
        s_refs[c][...] = lax.dot_general(k_ref[pl.ds(k0, tk), :], qm_ref[c], (((1,), (1,)), ((), ())),
                                         preferred_element_type=F32)

    def softmax_pv(j, c, masked):
        for st in range(tq // qs):
            cols = pl.ds(st * qs, qs)
            nk = min((st + 1) * qs, tk) if masked else tk
            vt = vt_ref[j, :, 0:nk]
            s = s_refs[c][0:nk, cols]
            if masked:
                kpos = lax.broadcasted_iota(jnp.int32, (nk, qs), 0)
                qpos = st * qs + lax.broadcasted_iota(jnp.int32, (nk, qs), 1)
                s = jnp.where(kpos <= qpos, s, -jnp.inf)
            m_old = m_ref[c, :, cols]
            m_new = jnp.maximum(m_old, jnp.max(s, axis=0, keepdims=True))
            p = jnp.exp2(s - m_new)
            alpha = jnp.exp2(m_old - m_new)
            l_ref[c, :, cols] = alpha * l_ref[c, :, cols] + jnp.sum(p, axis=0, keepdims=True)
            acc_ref[c, :, cols] = (alpha * acc_ref[c, :, cols]
                                   + jnp.dot(vt, p.astype(BF16), preferred_element_type=F32))
            m_ref[c, :, cols] = m_new

    def full_block(j, carry):
        scores(j, 1)
        softmax_pv(j, 0, False)
        scores(j + 1, 0)
        softmax_pv(j, 1, False)
        return carry

    def two_full_blocks(jj, carry):
        full_block(2 * jj, carry)
        return full_block(2 * jj + 1, carry)

    scores(0, 0)
    lax.fori_loop(0, qi // 2, two_full_blocks, 0)

    @pl.when(qi % 2 == 1)
    def _():
        full_block(qi - 1, 0)

    scores(qi, 1)
    softmax_pv(qi, 0, True)
    softmax_pv(qi, 1, True)

    lam =(jnp.exp(jnp.sum(lq1_ref[...] * lk1_ref[...], axis=-1, keepdims=True))
           - jnp.exp(jnp.sum(lq2_ref[...] * lk2_ref[...], axis=-1, keepdims=True)) + lam_init)
    o_t = acc_ref[0] * (1.0 / l_ref[0]) - lam * (acc_ref[1] * (1.0 / l_ref[1]))
    ms = jnp.mean(o_t * o_t, axis=0, keepdims=True)
    o_t = o_t * lax.rsqrt(ms + EPS) * ng_ref[...] * (1.0 - lam_init)
    o_ref[...] = o_t.T.astype(o_ref.dtype)


def _attention(att, lq1, lk1, lq2, lk2, norm_g, layer, batch, seq, tq=512, qs=256):
    T = att.shape[0]
    tk = tq
    nq = seq // tq
    hw = 2 * DA_HEAD_DIM
    lam_init = 0.8 - 0.6 * math.exp(-0.3 * layer)
    vec = lambda: pl.BlockSpec((1, DA_HEAD_DIM), lambda b, h, i: (0, 0))
    return pl.pallas_call(
        functools.partial(_attn_kernel, tq=tq, tk=tk, qs=qs, lam_init=lam_init, seq=seq),
        grid=(batch, DA_HEADS, nq),
        in_specs=[pl.BlockSpec((tq, hw), lambda b, h, i: (b * nq + i, h)),
                  pl.BlockSpec((seq, hw), lambda b, h, i: (b, DA_HEADS + h)),
                  pl.BlockSpec((seq, hw), lambda b, h, i: (b, 2 * DA_HEADS + h)),
                  vec(), vec(), vec(), vec(),
                  pl.BlockSpec((hw, 1), lambda b, h, i: (h, 0))],
        out_specs=pl.BlockSpec((tq, hw), lambda b, h, i: (b * nq + i, h)),
        out_shape=jax.ShapeDtypeStruct((T, DA_WIDTH), BF16),
        scratch_shapes=[pltpu.VMEM((seq // tk, hw, tk), BF16), pltpu.VMEM((2, tq, hw), BF16),
                        pltpu.VMEM((tk, tq), F32), pltpu.VMEM((tk, tq), F32),
                        pltpu.VMEM((2, 1, tq), F32), pltpu.VMEM((2, 1, tq), F32),
                        pltpu.VMEM((2, hw, tq), F32)],
        compiler_params=_params("arbitrary", "arbitrary", "arbitrary"),
        name="diff_attn",
    )(att, att, att, lq1.reshape(1, -1), lk1.reshape(1, -1), lq2.reshape(1, -1),
      lk2.reshape(1, -1), norm_g.reshape(-1, 1))


def _out_proj_kernel(hg_ref, lru_ref, da_ref, w_ref, x_ref, g_ref, b_ref, o_ref):
    h = jnp.dot(hg_ref[...], w_ref[0:HG_WIDTH, :], preferred_element_type=F32)
    h = h + jnp.dot(lru_ref[...], w_ref[HG_WIDTH:HG_WIDTH + LRU_WIDTH, :], preferred_element_type=F32)
    h = h + jnp.dot(da_ref[...], w_ref[HG_WIDTH + LRU_WIDTH:, :], preferred_element_type=F32)
    o_ref[...] = _layer_norm_rows(DN_ALPHA * x_ref[...] + h, g_ref[...], b_ref[...])


def _out_proj_ln(o_hg, o_lru, o_da, w_bf16, x, g, b, tm=512):
    T = x.shape[0]
    rows = lambda w: pl.BlockSpec((tm, w), lambda i: (i, 0))
    const = lambda shape: pl.BlockSpec(shape, lambda i: (0, 0))
    return pl.pallas_call(
        _out_proj_kernel,
        grid=(T // tm,),
        in_specs=[rows(HG_WIDTH), rows(LRU_WIDTH), rows(DA_WIDTH), const((D_MODEL, D_MODEL)),
                  rows(D_MODEL), const((1, D_MODEL)), const((1, D_MODEL))],
        out_specs=rows(D_MODEL),
        out_shape=jax.ShapeDtypeStruct((T, D_MODEL), F32),
        compiler_params=_params("parallel"),
        name="out_proj_ln",
    )(o_hg, o_lru, o_da, w_bf16, x, g.reshape(1, -1), b.reshape(1, -1))


def _ffn_kernel(x_ref, wg_ref, wu_ref, wd_ref, g_ref, b_ref, o_ref, xb_ref, acc_ref):
    f = pl.program_id(1)

    @pl.when(f == 0)
    def _():
        xb_ref[...] = x_ref[...].astype(BF16)
        acc_ref[...] = jnp.zeros_like(acc_ref)

    xb = xb_ref[...]
    gate = jnp.dot(xb, wg_ref[...], preferred_element_type=F32)
    up = jnp.dot(xb, wu_ref[...], preferred_element_type=F32)
    hmid = (gate * jax.nn.sigmoid(gate) * up).astype(BF16)
    acc_ref[...] += jnp.dot(hmid, wd_ref[...], preferred_element_type=F32)

    @pl.when(f == pl.num_programs(1) - 1)
    def _():
        o_ref[...] = _layer_norm_rows(DN_ALPHA * x_ref[...] + acc_ref[...], g_ref[...], b_ref[...])


def _ffn_ln(x, wg, wu, wd, g, b, tm=1024, tf=512):
    T = x.shape[0]
    return pl.pallas_call(
        _ffn_kernel,
        grid=(T // tm, D_FF // tf),
        in_specs=[pl.BlockSpec((tm, D_MODEL), lambda i, f: (i, 0)),
                  pl.BlockSpec((D_MODEL, tf), lambda i, f: (0, f)),
                  pl.BlockSpec((D_MODEL, tf), lambda i, f: (0, f)),
                  pl.BlockSpec((tf, D_MODEL), lambda i, f: (f, 0)),
                  pl.BlockSpec((1, D_MODEL), lambda i, f: (0, 0)),
                  pl.BlockSpec((1, D_MODEL), lambda i, f: (0, 0))],
        out_specs=pl.BlockSpec((tm, D_MODEL), lambda i, f: (i, 0)),
        out_shape=jax.ShapeDtypeStruct((T, D_MODEL), F32),
        scratch_shapes=[pltpu.VMEM((tm, D_MODEL), BF16), pltpu.VMEM((tm, D_MODEL), F32)],
        compiler_params=_params("parallel", "arbitrary"),
        name="ffn_ln",
    )(x, wg, wu, wd, g.reshape(1, -1), b.reshape(1, -1))


ROUTER_LANES = 128
META_I1, META_I2, META_R1, META_R2 = 0, 1, 2, 3


def _router_kernel(x_ref, rw_ref, tri_ref, meta_ref, gate_ref, cnt_ref, carry_ref):
    @pl.when(pl.program_id(0) == 0)
    def _():
        carry_ref[...] = jnp.zeros_like(carry_ref)

    logits = jnp.dot(x_ref[...], rw_ref[...], preferred_element_type=F32,
                     precision=lax.Precision.HIGHEST)
    lane = lax.broadcasted_iota(jnp.int32, logits.shape, 1)
    neg = -jnp.inf
    l1 = jnp.where(lane < N_EXPERTS, logits, neg)
    v1 = jnp.max(l1, axis=-1, keepdims=True)
    i1 = jnp.min(jnp.where(l1 == v1, lane, ROUTER_LANES), axis=-1, keepdims=True)
    l2 = jnp.where(lane == i1, neg, l1)
    v2 = jnp.max(l2, axis=-1, keepdims=True)
    i2 = jnp.min(jnp.where(l2 == v2, lane, ROUTER_LANES), axis=-1, keepdims=True)
    e2 = jnp.exp(v2 - v1)
    g1 = 1.0 / (1.0 + e2)
    g2 = e2 / (1.0 + e2)
    member = jnp.where((lane == i1) | (lane == i2), 1.0, 0.0)
    rank = jnp.dot(tri_ref[...], member.astype(BF16), preferred_element_type=F32) + carry_ref[0:1, :]
    r1 = jnp.sum(jnp.where(lane == i1, rank, 0.0), axis=-1, keepdims=True).astype(jnp.int32)
    r2 = jnp.sum(jnp.where(lane == i2, rank, 0.0), axis=-1, keepdims=True).astype(jnp.int32)
    meta = jnp.where(lane == META_I1, i1, 0) + jnp.where(lane == META_I2, i2, 0)
    meta = meta + jnp.where(lane == META_R1, r1, 0) + jnp.where(lane == META_R2, r2, 0)
    meta_ref[...] = meta
    gate_ref[...] = jnp.where(lane == 0, g1, 0.0) + jnp.where(lane == 1, g2, 0.0)
    carry = carry_ref[...] + jnp.sum(member, axis=0, keepdims=True)
    carry_ref[...] = carry
    cnt_ref[...] = carry


def _router(x, router_pad, tm=1024):
    T = x.shape[0]
    tri = (jnp.arange(tm)[:, None] > jnp.arange(tm)[None, :]).astype(BF16)
    return pl.pallas_call(
        _router_kernel,
        grid=(T // tm,),
        in_specs=[pl.BlockSpec((tm, D_MODEL), lambda i: (i, 0)),
                  pl.BlockSpec((D_MODEL, ROUTER_LANES), lambda i: (0, 0)),
                  pl.BlockSpec((tm, tm), lambda i: (0, 0))],
        out_specs=[pl.BlockSpec((tm, ROUTER_LANES), lambda i: (i, 0)),
                   pl.BlockSpec((tm, ROUTER_LANES), lambda i: (i, 0)),
                   pl.BlockSpec((8, ROUTER_LANES), lambda i: (0, 0))],
        out_shape=[jax.ShapeDtypeStruct((T, ROUTER_LANES), jnp.int32),
                   jax.ShapeDtypeStruct((T, ROUTER_LANES), F32),
                   jax.ShapeDtypeStruct((8, ROUTER_LANES), F32)],
        scratch_shapes=[pltpu.VMEM((8, ROUTER_LANES), F32)],
        compiler_params=_params("arbitrary"),
        name="moe_router",
    )(x, router_pad, tri)


def _dispatch_tables(meta, cnt, tm):
    T = meta.shape[0]
    counts = cnt[0, :N_EXPERTS].astype(jnp.int32)
    padded = ((counts + tm - 1) // tm) * tm
    ends = jnp.cumsum(padded)
    off = ends - padded
    pos1 = off[meta[:, META_I1]] + meta[:, META_R1]
    pos2 = off[meta[:, META_I2]] + meta[:, META_R2]
    n_tiles = 2 * T // tm + N_EXPERTS + 1
    tok = jnp.arange(T, dtype=jnp.int32)
    code_of = jnp.full((n_tiles * tm,), 2 * T, jnp.int32).at[jnp.concatenate([pos1, pos2])].set(
        jnp.concatenate([tok, tok + T]))
    tile_start = jnp.arange(n_tiles, dtype=jnp.int32) * tm
    tile_expert = jnp.minimum(jnp.sum((tile_start[:, None] >= ends[None, :]).astype(jnp.int32), axis=1),
                              N_EXPERTS - 1)
    n_used = (ends[-1] // tm).astype(jnp.int32).reshape(1)
    return code_of.reshape(n_tiles, 1, tm), tile_expert, n_used


def _expert_kernel(te_ref, nu_ref, cur_ref, nxt_ref, prv_ref, x_hbm, wg_ref, wu_ref, wd_ref, y_hbm,
                   xbuf_ref, xb_ref, acc_ref, sem_g, sem_s, *, tm, nf, n_tok):
    i = pl.program_id(0)
    f = pl.program_id(1)
    n_used = nu_ref[0]
    slot = i % 2
    other = 1 - slot
    pad_code = 2 * n_tok
    rows_per_step = tm // nf

    def gather(code_ref, r, buf):
        code = code_ref[0, 0, r]
        tok = jnp.where(code >= pad_code, 0, jnp.where(code >= n_tok, code - n_tok, code))
        return pltpu.make_async_copy(x_hbm.at[pl.ds(tok, 1)], xbuf_ref.at[buf, pl.ds(r, 1)],
                                     sem_g.at[buf])

    def scatter(r, buf, to_dump):
        code = prv_ref[0, 0, r]
        dest = jnp.where(to_dump | (code >= pad_code), pad_code + buf * tm + r, code)
        return pltpu.make_async_copy(acc_ref.at[buf, pl.ds(r, 1)], y_hbm.at[pl.ds(dest, 1)], sem_s)

    def wait_gather(buf):
        pltpu.make_async_copy(x_hbm.at[pl.ds(0, tm)], xbuf_ref.at[buf], sem_g.at[buf]).wait()

    def wait_scatter(buf):
        pltpu.make_async_copy(acc_ref.at[buf], y_hbm.at[pl.ds(0, tm)], sem_s).wait()

    @pl.when((i < n_used) & (f == 0))
    def _():
        @pl.when(i == 0)
        def _():
            def issue(r, carry):
                gather(cur_ref, r, 0).start()
                return carry
            lax.fori_loop(0, tm, issue, 0, unroll=8)
            acc_ref[1] = jnp.zeros((tm, D_MODEL), F32)
            clear = pltpu.make_async_copy(acc_ref.at[1], y_hbm.at[pl.ds(pad_code, tm)], sem_s)
            clear.start()
            clear.wait()

        wait_gather(slot)
        xb_ref[...] = xbuf_ref[slot].astype(BF16)
        acc_ref[slot] = jnp.zeros((tm, D_MODEL), F32)

    @pl.when(i < n_used)
    def _():
        for rr in range(rows_per_step):
            r = f * rows_per_step + rr
            gather(nxt_ref, r, other).start()
            scatter(r, other, i == 0).start()
        xb = xb_ref[...]
        gate = jnp.dot(xb, wg_ref[0], preferred_element_type=F32)
        up = jnp.dot(xb, wu_ref[0], preferred_element_type=F32)
        hmid = (gate * jax.nn.sigmoid(gate) * up).astype(BF16)
        acc_ref[slot] += jnp.dot(hmid, wd_ref[0], preferred_element_type=F32)

        @pl.when(f == nf - 1)
        def _():
            wait_scatter(other)

    @pl.when((i == n_used) & (f == 0))
    def _():
        wait_gather(slot)

        def issue(r, carry):
            scatter(r, other, False).start()
            return carry
        lax.fori_loop(0, tm, issue, 0, unroll=8)
        wait_scatter(other)


def _experts(x, code_of, tile_expert, n_used, wg, wu, wd, tm, tf=1792):
    n_tok = x.shape[0]
    n_tiles = code_of.shape[0]
    nf = D_FF // tf
    fidx = lambda i, f, nu: jnp.where(i < nu[0], f, nf - 1)
    codes = lambda shift: pl.BlockSpec(
        (1, 1, tm), lambda i, f, te, nu: (jnp.clip(i + shift, 0, n_tiles - 1), 0, 0),
        memory_space=pltpu.SMEM)
    grid_spec = pltpu.PrefetchScalarGridSpec(
        num_scalar_prefetch=2,
        grid=(n_tiles, nf),
        in_specs=[codes(0), codes(1), codes(-1),
                  pl.BlockSpec(memory_space=pl.ANY),
                  pl.BlockSpec((1, D_MODEL, tf), lambda i, f, te, nu: (te[i], 0, fidx(i, f, nu))),
                  pl.BlockSpec((1, D_MODEL, tf), lambda i, f, te, nu: (te[i], 0, fidx(i, f, nu))),
                  pl.BlockSpec((1, tf, D_MODEL), lambda i, f, te, nu: (te[i], fidx(i, f, nu), 0))],
        out_specs=pl.BlockSpec(memory_space=pl.ANY),
        scratch_shapes=[pltpu.VMEM((2, tm, D_MODEL), F32), pltpu.VMEM((tm, D_MODEL), BF16),
                        pltpu.VMEM((2, tm, D_MODEL), F32),
                        pltpu.SemaphoreType.DMA((2,)), pltpu.SemaphoreType.DMA(())],
    )
    return pl.pallas_call(
        functools.partial(_expert_kernel, tm=tm, nf=nf, n_tok=n_tok),
        grid_spec=grid_spec,
        out_shape=jax.ShapeDtypeStruct((2 * n_tok + 2 * tm, D_MODEL), F32),
        compiler_params=_params("arbitrary", "arbitrary"),
        name="moe_experts",
    )(tile_expert, n_used, code_of, code_of, code_of, x, wg, wu, wd)


def _combine_kernel(x_ref, y1_ref, y2_ref, gate_ref, g_ref, b_ref, o_ref):
    gates = gate_ref[...]
    y = gates[:, 0:1] * y1_ref[...] + gates[:, 1:2] * y2_ref[...]
    o_ref[...] = _layer_norm_rows(DN_ALPHA * x_ref[...] + y, g_ref[...], b_ref[...])


def _combine_ln(x, y, gates, g, b, tm=512):
    T = x.shape[0]
    nt = T // tm
    return pl.pallas_call(
        _combine_kernel,
        grid=(nt,),
        in_specs=[pl.BlockSpec((tm, D_MODEL), lambda i: (i, 0)),
                  pl.BlockSpec((tm, D_MODEL), lambda i: (i, 0)),
                  pl.BlockSpec((tm, D_MODEL), lambda i: (nt + i, 0)),
                  pl.BlockSpec((tm, ROUTER_LANES), lambda i: (i, 0)),
                  pl.BlockSpec((1, D_MODEL), lambda i: (0, 0)),
                  pl.BlockSpec((1, D_MODEL), lambda i: (0, 0))],
        out_specs=pl.BlockSpec((tm, D_MODEL), lambda i: (i, 0)),
        out_shape=jax.ShapeDtypeStruct((T, D_MODEL), F32),
        compiler_params=_params("parallel"),
        name="moe_combine_ln",
    )(x, y, y, gates, g.reshape(1, -1), b.reshape(1, -1))


def _moe_ln(x, router_w, wg, wu, wd, g, b, tm=512):
    router_pad = jnp.pad(router_w, ((0, 0), (0, ROUTER_LANES - N_EXPERTS)))
    meta, gates, cnt = _router(x, router_pad)
    code_of, tile_expert, n_used = _dispatch_tables(meta, cnt, tm)
    y = _experts(x, code_of, tile_expert, n_used, wg, wu, wd, tm)
    return _combine_ln(x, y, gates, g, b)


def _block_diag(w):
    nb, n, _ = w.shape
    eye = jnp.eye(nb, dtype=w.dtype)
    return (eye[:, None, :, None] * w[:, :, None, :]).reshape(nb * n, nb * n)


def kernel(x, w_in, w_out, hg_lb_logits, hg_norm_g, lru_conv_w, lru_conv_b, lru_wa, lru_ba, lru_wx, lru_bx, lru_lambda, da_lq1, da_lk1, da_lq2, da_lk2, da_norm_g, ln1_g, ln1_b, ln2_g, ln2_b, ffn_wg, ffn_wu, ffn_wd, router_w, moe_wg, moe_wu, moe_wd):
    batch, seq, _ = x.shape
    depth = w_in.shape[0]
    xf = x.reshape(batch * seq, D_MODEL)
    lb_p = jax.nn.softmax(hg_lb_logits.astype(F32), axis=0)
    lb_all = jnp.cumsum(lb_p, axis=0) - lb_p[0:1]
    for l in range(depth):
        rec, att = _in_proj(xf, w_in[l].astype(BF16))
        o_hg = _hgrn(rec, lb_all[l], hg_norm_g[l], batch, seq)
        o_lru = _lru(rec, lru_conv_w[l], lru_conv_b[l], _block_diag(lru_wa[l]).astype(BF16),
                     _block_diag(lru_wx[l]).astype(BF16), lru_ba[l], lru_bx[l], lru_lambda[l],
                     batch, seq)
        o_da = _attention(att, da_lq1[l], da_lk1[l], da_lq2[l], da_lk2[l], da_norm_g[l], l,
                          batch, seq)
        xf = _out_proj_ln(o_hg, o_lru, o_da, w_out[l].astype(BF16), xf, ln1_g[l], ln1_b[l])
        j = l // 2
        if l % 2 == 0:
            xf = _ffn_ln(xf, ffn_wg[j].astype(BF16), ffn_wu[j].astype(BF16), ffn_wd[j].astype(BF16),
                         ln2_g[l], ln2_b[l])
        else:
            xf = _moe_ln(xf, router_w[j], moe_wg[j].astype(BF16), moe_wu[j].astype(BF16),
                         moe_wd[j].astype(BF16), ln2_g[l], ln2_b[l])
    return xf.reshape(batch, seq, D_MODEL)
```

```python
import functools
import math

import jax
import jax.numpy as jnp
from jax import lax
from jax.experimental import pallas as pl
from jax.experimental.pallas import tpu as pltpu

D_MODEL = 1024
DEPTH = 2
HG_HEADS = 4
HG_KEY_DIM = 64
HG_WIDTH = 256
HG_CHUNK = 16
LRU_WIDTH = 256
LRU_BLOCKS = 4
CONV_WIDTH = 4
LRU_C = 8.0
DA_WIDTH = 512
DA_HEADS = 4
DA_HEAD_DIM = 64
D_FF = 3584
N_EXPERTS = 8
DN_ALPHA = (2.0 * DEPTH) ** 0.25
EPS = 1e-5
REC_WIDTH = 4 * HG_WIDTH + 2 * LRU_WIDTH
ATT_WIDTH = 3 * DA_WIDTH

V7X_VMEM_BYTES = 64 * 1024 * 1024
VMEM_LIMIT = 48 * 1024 * 1024

BF16 = jnp.bfloat16
F32 = jnp.float32


def _params(*semantics):
    return pltpu.CompilerParams(dimension_semantics=semantics, vmem_limit_bytes=VMEM_LIMIT)


def _layer_norm_rows(y, g, b):
    mu = jnp.mean(y, axis=-1, keepdims=True)
    yc = y - mu
    var = jnp.mean(yc * yc, axis=-1, keepdims=True)
    return yc * lax.rsqrt(var + EPS) * g + b


Q_SCALE = DA_HEAD_DIM ** -0.5 * math.log2(math.e)


def _in_proj_kernel(x_ref, w_ref, rec_ref, att_ref):
    xb = x_ref[...].astype(BF16)
    rec_ref[...] = jnp.dot(xb, w_ref[:, :REC_WIDTH], preferred_element_type=F32)
    q = jnp.dot(xb, w_ref[:, REC_WIDTH:REC_WIDTH + DA_WIDTH], preferred_element_type=F32)
    att_ref[:, :DA_WIDTH] = (q * Q_SCALE).astype(BF16)
    att_ref[:, DA_WIDTH:] = jnp.dot(xb, w_ref[:, REC_WIDTH + DA_WIDTH:],
                                    preferred_element_type=F32).astype(BF16)


def _in_proj(x, w_bf16, tm=512):
    T = x.shape[0]
    return pl.pallas_call(
        _in_proj_kernel,
        grid=(T // tm,),
        in_specs=[pl.BlockSpec((tm, D_MODEL), lambda i: (i, 0)),
                  pl.BlockSpec((D_MODEL, REC_WIDTH + ATT_WIDTH), lambda i: (0, 0))],
        out_specs=[pl.BlockSpec((tm, REC_WIDTH), lambda i: (i, 0)),
                   pl.BlockSpec((tm, ATT_WIDTH), lambda i: (i, 0))],
        out_shape=[jax.ShapeDtypeStruct((T, REC_WIDTH), F32),
                   jax.ShapeDtypeStruct((T, ATT_WIDTH), BF16)],
        compiler_params=_params("parallel"),
        name="in_proj",
    )(x, w_bf16)


def _hgrn_kernel(rec_ref, lb_ref, ng_ref, gmat_ref, sel_ref, bmask_ref, o_ref,
                 st_ref, q_s, kk_s, b_s, bl_s, qd_s, kd_s, o_s, *, rows):
    @pl.when(pl.program_id(1) == 0)
    def _():
        st_ref[...] = jnp.zeros_like(st_ref)

    C = HG_CHUNK
    W = HG_WIDTH
    nc = rows // C
    lb = lb_ref[...]
    gmat = gmat_ref[...]
    sel = sel_ref[...]

    qr = rec_ref[:, 0:W]
    z = rec_ref[:, W:2 * W]
    rin = lax.broadcasted_iota(jnp.int32, (rows, W), 0) % C
    logf = jnp.log(lb + (1.0 - lb) * jax.nn.sigmoid(z))
    kk = (1.0 - lb) * jax.nn.sigmoid(-z)
    q = qr * jax.nn.sigmoid(qr)
    b = logf
    for s in (1, 2, 4, 8):
        b = b + jnp.where(rin >= s, pltpu.roll(b, s, 0), 0.0)
    r = jnp.where(rin < C - 1, pltpu.roll(logf, rows - 1, 0), 0.0)
    for s in (1, 2, 4, 8):
        r = r + jnp.where(rin < C - s, pltpu.roll(r, rows - s, 0), 0.0)
    q_s[...] = q
    kk_s[...] = kk
    b_s[...] = b
    bl_s[...] = b + r
    qd_s[...] = (q * jnp.exp(b)).astype(BF16)
    kd_s[...] = (kk * jnp.exp(r)).astype(BF16)

    row = lax.broadcasted_iota(jnp.int32, (C, W), 0)

    def chunk(c, carry):
        r0 = pl.multiple_of(c * C, C)
        q = q_s[pl.ds(r0, C), :]
        kk = kk_s[pl.ds(r0, C), :]
        b = b_s[pl.ds(r0, C), :]
        v = rec_ref[pl.ds(r0, C), 2 * W:3 * W]
        w_rows = []
        for t in range(C):
            rel = b[t:t + 1, :] - b
            dec = jnp.exp(jnp.where(row <= t, rel, -jnp.inf))
            w_rows.append(((q[t:t + 1, :] * kk) * dec).astype(BF16))
        w2 = jnp.concatenate(w_rows, axis=0)
        a = jnp.dot(w2, gmat, preferred_element_type=F32)
        p = a * jnp.concatenate([v] * C, axis=0)
        o_intra = jnp.dot(sel, p.astype(BF16), preferred_element_type=F32)
        u_t = lax.dot_general(v.astype(BF16), kd_s[pl.ds(r0, C), :], (((0,), (0,)), ((), ())),
                              preferred_element_type=F32)
        st = st_ref[...]
        o_inter = lax.dot_general(qd_s[pl.ds(r0, C), :], st.astype(BF16), (((1,), (1,)), ((), ())),
                                  preferred_element_type=F32)
        o_s[pl.ds(r0, C), :] = o_intra + o_inter
        st_ref[...] = st * jnp.exp(bl_s[pl.ds(r0, 1), :]) + u_t * bmask_ref[...]
        return carry

    lax.fori_loop(0, nc, chunk, 0, unroll=8)

    o = o_s[...]
    g = rec_ref[:, 3 * W:4 * W]
    sq = o * o
    sq_hi = sq.astype(BF16)
    sq_lo = (sq - sq_hi.astype(F32)).astype(BF16)
    ms = (jnp.dot(sq_hi, gmat, preferred_element_type=F32)
          + jnp.dot(sq_lo, gmat, preferred_element_type=F32)) * (1.0 / HG_KEY_DIM)
    out = o * lax.rsqrt(ms + EPS) * ng_ref[...] * (g * jax.nn.sigmoid(g))
    o_ref[...] = out.astype(o_ref.dtype)


def _hgrn(rec, lb, norm_g, batch, seq, rows=256):
    T = rec.shape[0]
    nblk = seq // rows
    head = jnp.arange(HG_WIDTH) // HG_KEY_DIM
    same_head = head[:, None] == head[None, :]
    gmat = same_head.astype(BF16)
    bmask = same_head.astype(F32)
    sel = (jnp.arange(HG_CHUNK)[:, None] == (jnp.arange(HG_CHUNK * HG_CHUNK) // HG_CHUNK)[None, :]).astype(BF16)
    const = lambda shape: pl.BlockSpec(shape, lambda b, i: (0, 0))
    return pl.pallas_call(
        functools.partial(_hgrn_kernel, rows=rows),
        grid=(batch, nblk),
        in_specs=[pl.BlockSpec((rows, 4 * HG_WIDTH), lambda b, i: (b * nblk + i, 0)),
                  const((1, HG_WIDTH)), const((1, HG_WIDTH)),
                  const((HG_WIDTH, HG_WIDTH)), const((HG_CHUNK, HG_CHUNK * HG_CHUNK)),
                  const((HG_WIDTH, HG_WIDTH))],
        out_specs=pl.BlockSpec((rows, HG_WIDTH), lambda b, i: (b * nblk + i, 0)),
        out_shape=jax.ShapeDtypeStruct((T, HG_WIDTH), BF16),
        scratch_shapes=[pltpu.VMEM((HG_WIDTH, HG_WIDTH), F32)]
        + [pltpu.VMEM((rows, HG_WIDTH), F32)] * 4
        + [pltpu.VMEM((rows, HG_WIDTH), BF16)] * 2
        + [pltpu.VMEM((rows, HG_WIDTH), F32)],
        compiler_params=_params("parallel", "arbitrary"),
        name="hgrn2",
    )(rec, lb.reshape(1, -1), norm_g.reshape(1, -1), gmat, sel, bmask)


def _lru_kernel(rec_ref, cw_ref, cb_ref, wa_ref, wx_ref, ba_ref, bx_ref, lam_ref, o_ref,
                xprev_ref, h_ref, *, rows):
    @pl.when(pl.program_id(1) == 0)
    def _():
        xprev_ref[...] = jnp.zeros_like(xprev_ref)
        h_ref[...] = jnp.zeros_like(h_ref)

    W = LRU_WIDTH
    x = rec_ref[:, 0:W]
    gate = rec_ref[:, W:2 * W]
    row = lax.broadcasted_iota(jnp.int32, (rows, W), 0)
    xp = xprev_ref[...]
    tail = jnp.zeros((rows - 8, W), F32)
    xc = cb_ref[...] + cw_ref[CONV_WIDTH - 1:CONV_WIDTH, :] * x
    for j in range(1, CONV_WIDTH):
        prev = jnp.concatenate([pltpu.roll(xp, j, 0), tail], axis=0)
        xs = jnp.where(row >= j, pltpu.roll(x, j, 0), prev)
        xc = xc + cw_ref[CONV_WIDTH - 1 - j:CONV_WIDTH - j, :] * xs
    xprev_ref[...] = x[rows - 8:rows, :]

    xcb = xc.astype(BF16)
    r = jax.nn.sigmoid(jnp.dot(xcb, wa_ref[...], preferred_element_type=F32) + ba_ref[...])
    i = jax.nn.sigmoid(jnp.dot(xcb, wx_ref[...], preferred_element_type=F32) + bx_ref[...])
    lam = lam_ref[...]
    log_sig = jnp.minimum(lam, 0.0) - jnp.log1p(jnp.exp(-jnp.abs(lam)))
    log_a = LRU_C * r * log_sig
    a = jnp.exp(log_a)
    th = jnp.tanh(log_a)
    one_minus_a2 = -2.0 * th / (1.0 - th)
    u = jnp.sqrt(one_minus_a2) * i * xc

    for s in (1, 2, 4):
        a_s = jnp.where(row >= s, pltpu.roll(a, s, 0), 1.0)
        u_s = jnp.where(row >= s, pltpu.roll(u, s, 0), 0.0)
        u = a * u_s + u
        a = a * a_s
    s = 8
    while s < rows:
        a_s = jnp.concatenate([jnp.ones((s, W), F32), a[:rows - s, :]], axis=0)
        u_s = jnp.concatenate([jnp.zeros((s, W), F32), u[:rows - s, :]], axis=0)
        u = a * u_s + u
        a = a * a_s
        s *= 2
    h = a * h_ref[0:1, :] + u
    h_ref[...] = jnp.broadcast_to(h[rows - 1:rows, :], h_ref.shape)
    o_ref[...] = (h * jax.nn.gelu(gate)).astype(o_ref.dtype)


def _lru(rec, conv_w, conv_b, wa_bd, wx_bd, ba, bx, lam, batch, seq, rows=256):
    T = rec.shape[0]
    nblk = seq // rows
    W = LRU_WIDTH
    const = lambda shape: pl.BlockSpec(shape, lambda b, i: (0, 0))
    return pl.pallas_call(
        functools.partial(_lru_kernel, rows=rows),
        grid=(batch, nblk),
        in_specs=[pl.BlockSpec((rows, 2 * W), lambda b, i: (b * nblk + i, 2)),
                  const((CONV_WIDTH, W)), const((1, W)), const((W, W)), const((W, W)),
                  const((1, W)), const((1, W)), const((1, W))],
        out_specs=pl.BlockSpec((rows, W), lambda b, i: (b * nblk + i, 0)),
        out_shape=jax.ShapeDtypeStruct((T, W), BF16),
        scratch_shapes=[pltpu.VMEM((8, W), F32), pltpu.VMEM((8, W), F32)],
        compiler_params=_params("parallel", "arbitrary"),
        name="rglru",
    )(rec, conv_w, conv_b.reshape(1, -1), wa_bd, wx_bd, ba.reshape(1, -1), bx.reshape(1, -1),
      lam.reshape(1, -1))


def _attn_kernel(q_ref, k_ref, v_ref, lq1_ref, lk1_ref, lq2_ref, lk2_ref, ng_ref, o_ref,
                 vt_ref, qm_ref, s0_ref, s1_ref, m_ref, l_ref, acc_ref, *, tq, tk, qs, lam_init, seq):
    d = DA_HEAD_DIM
    qi = pl.program_id(2)

    @pl.when(qi == 0)
    def _():
        def transpose_values(j, carry):
            r0 = pl.multiple_of(j * tk, tk)
            vt_ref[j] = v_ref[pl.ds(r0, tk), :].astype(F32).T.astype(BF16)
            return carry
        lax.fori_loop(0, seq // tk, transpose_values, 0)

    q = q_ref[...]
    lane = lax.broadcasted_iota(jnp.int32, q.shape, 1)
    qm_ref[0] = jnp.where(lane < d, q, jnp.zeros_like(q))
    qm_ref[1] = jnp.where(lane >= d, q, jnp.zeros_like(q))
    m_ref[...] = jnp.full_like(m_ref, -jnp.inf)
    l_ref[...] = jnp.zeros_like(l_ref)
    acc_ref[...] = jnp.zeros_like(acc_ref)
    s_refs = (s0_ref, s1_ref)

    def scores(j, c):
        k0 = pl.multiple_of(j * tk, tk)
        s_refs[c][...] = lax.dot_general(k_ref[pl.ds(k0, tk), :], qm_ref[c], (((1,), (1,)), ((), ())),
                                         preferred_element_type=F32)

    def softmax_pv(j, c, masked):
        for st in range(tq // qs):
            cols = pl.ds(st * qs, qs)
            nk = min((st + 1) * qs, tk) if masked else tk
            vt = vt_ref[j, :, 0:nk]
            s = s_refs[c][0:nk, cols]
            if masked:
                kpos = lax.broadcasted_iota(jnp.int32, (nk, qs), 0)
                qpos = st * qs + lax.broadcasted_iota(jnp.int32, (nk, qs), 1)
                s = jnp.where(kpos <= qpos, s, -jnp.inf)
            m_old = m_ref[c, :, cols]
            m_new = jnp.maximum(m_old, jnp.max(s, axis=0, keepdims=True))
            p = jnp.exp2(s - m_new)
            alpha = jnp.exp2(m_old - m_new)
            l_ref[c, :, cols] = alpha * l_ref[c, :, cols] + jnp.sum(p, axis=0, keepdims=True)
            acc_ref[c, :, cols] = (alpha * acc_ref[c, :, cols]
                                   + jnp.dot(vt, p.astype(BF16), preferred_element_type=F32))
            m_ref[c, :, cols] = m_new

    def full_block(j, carry):
        scores(j, 1)
        softmax_pv(j, 0, False)
        scores(j + 1, 0)
        softmax_pv(j, 1, False)
        return carry

    def two_full_blocks(jj, carry):
        full_block(2 * jj, carry)
        return full_block(2 * jj + 1, carry)

    scores(0, 0)
    lax.fori_loop(0, qi // 2, two_full_blocks, 0)

    @pl.when(qi % 2 == 1)
    def _():
        full_block(qi - 1, 0)

    scores(qi, 1)
    softmax_pv(qi, 0, True)
    softmax_pv(qi, 1, True)

    lam =(jnp.exp(jnp.sum(lq1_ref[...] * lk1_ref[...], axis=-1, keepdims=True))
           - jnp.exp(jnp.sum(lq2_ref[...] * lk2_ref[...], axis=-1, keepdims=True)) + lam_init)
    o_t = acc_ref[0] * (1.0 / l_ref[0]) - lam * (acc_ref[1] * (1.0 / l_ref[1]))
    ms = jnp.mean(o_t * o_t, axis=0, keepdims=True)
    o_t = o_t * lax.rsqrt(ms + EPS) * ng_ref[...] * (1.0 - lam_init)
    o_ref[...] = o_t.T.astype(o_ref.dtype)


def _attention(att, lq1, lk1, lq2, lk2, norm_g, layer, batch, seq, tq=512, qs=256):
    T = att.shape[0]
    tk = tq
    nq = seq // tq
    hw = 2 * DA_HEAD_DIM
    lam_init = 0.8 - 0.6 * math.exp(-0.3 * layer)
    vec = lambda: pl.BlockSpec((1, DA_HEAD_DIM), lambda b, h, i: (0, 0))
    return pl.pallas_call(
        functools.partial(_attn_kernel, tq=tq, tk=tk, qs=qs, lam_init=lam_init, seq=seq),
        grid=(batch, DA_HEADS, nq),
        in_specs=[pl.BlockSpec((tq, hw), lambda b, h, i: (b * nq + i, h)),
                  pl.BlockSpec((seq, hw), lambda b, h, i: (b, DA_HEADS + h)),
                  pl.BlockSpec((seq, hw), lambda b, h, i: (b, 2 * DA_HEADS + h)),
                  vec(), vec(), vec(), vec(),
                  pl.BlockSpec((hw, 1), lambda b, h, i: (h, 0))],
        out_specs=pl.BlockSpec((tq, hw), lambda b, h, i: (b * nq + i, h)),
        out_shape=jax.ShapeDtypeStruct((T, DA_WIDTH), BF16),
        scratch_shapes=[pltpu.VMEM((seq // tk, hw, tk), BF16), pltpu.VMEM((2, tq, hw), BF16),
                        pltpu.VMEM((tk, tq), F32), pltpu.VMEM((tk, tq), F32),
                        pltpu.VMEM((2, 1, tq), F32), pltpu.VMEM((2, 1, tq), F32),
                        pltpu.VMEM((2, hw, tq), F32)],
        compiler_params=_params("arbitrary", "arbitrary", "arbitrary"),
        name="diff_attn",
    )(att, att, att, lq1.reshape(1, -1), lk1.reshape(1, -1), lq2.reshape(1, -1),
      lk2.reshape(1, -1), norm_g.reshape(-1, 1))


def _out_proj_kernel(hg_ref, lru_ref, da_ref, w_ref, x_ref, g_ref, b_ref, o_ref):
    h = jnp.dot(hg_ref[...], w_ref[0:HG_WIDTH, :], preferred_element_type=F32)
    h = h + jnp.dot(lru_ref[...], w_ref[HG_WIDTH:HG_WIDTH + LRU_WIDTH, :], preferred_element_type=F32)
    h = h + jnp.dot(da_ref[...], w_ref[HG_WIDTH + LRU_WIDTH:, :], preferred_element_type=F32)
    o_ref[...] = _layer_norm_rows(DN_ALPHA * x_ref[...] + h, g_ref[...], b_ref[...])


def _out_proj_ln(o_hg, o_lru, o_da, w_bf16, x, g, b, tm=512):
    T = x.shape[0]
    rows = lambda w: pl.BlockSpec((tm, w), lambda i: (i, 0))
    const = lambda shape: pl.BlockSpec(shape, lambda i: (0, 0))
    return pl.pallas_call(
        _out_proj_kernel,
        grid=(T // tm,),
        in_specs=[rows(HG_WIDTH), rows(LRU_WIDTH), rows(DA_WIDTH), const((D_MODEL, D_MODEL)),
                  rows(D_MODEL), const((1, D_MODEL)), const((1, D_MODEL))],
        out_specs=rows(D_MODEL),
        out_shape=jax.ShapeDtypeStruct((T, D_MODEL), F32),
        compiler_params=_params("parallel"),
        name="out_proj_ln",
    )(o_hg, o_lru, o_da, w_bf16, x, g.reshape(1, -1), b.reshape(1, -1))


def _ffn_kernel(x_ref, wg_ref, wu_ref, wd_ref, g_ref, b_ref, o_ref, xb_ref, acc_ref):
    f = pl.program_id(1)

    @pl.when(f == 0)
    def _():
        xb_ref[...] = x_ref[...].astype(BF16)
        acc_ref[...] = jnp.zeros_like(acc_ref)

    xb = xb_ref[...]
    gate = jnp.dot(xb, wg_ref[...], preferred_element_type=F32)
    up = jnp.dot(xb, wu_ref[...], preferred_element_type=F32)
    hmid = (gate * jax.nn.sigmoid(gate) * up).astype(BF16)
    acc_ref[...] += jnp.dot(hmid, wd_ref[...], preferred_element_type=F32)

    @pl.when(f == pl.num_programs(1) - 1)
    def _():
        o_ref[...] = _layer_norm_rows(DN_ALPHA * x_ref[...] + acc_ref[...], g_ref[...], b_ref[...])


def _ffn_ln(x, wg, wu, wd, g, b, tm=1024, tf=512):
    T = x.shape[0]
    return pl.pallas_call(
        _ffn_kernel,
        grid=(T // tm, D_FF // tf),
        in_specs=[pl.BlockSpec((tm, D_MODEL), lambda i, f: (i, 0)),
                  pl.BlockSpec((D_MODEL, tf), lambda i, f: (0, f)),
                  pl.BlockSpec((D_MODEL, tf), lambda i, f: (0, f)),
                  pl.BlockSpec((tf, D_MODEL), lambda i, f: (f, 0)),
                  pl.BlockSpec((1, D_MODEL), lambda i, f: (0, 0)),
                  pl.BlockSpec((1, D_MODEL), lambda i, f: (0, 0))],
        out_specs=pl.BlockSpec((tm, D_MODEL), lambda i, f: (i, 0)),
        out_shape=jax.ShapeDtypeStruct((T, D_MODEL), F32),
        scratch_shapes=[pltpu.VMEM((tm, D_MODEL), BF16), pltpu.VMEM((tm, D_MODEL), F32)],
        compiler_params=_params("parallel", "arbitrary"),
        name="ffn_ln",
    )(x, wg, wu, wd, g.reshape(1, -1), b.reshape(1, -1))


ROUTER_LANES = 128
META_I1, META_I2, META_R1, META_R2 = 0, 1, 2, 3


def _router_kernel(x_ref, rwh_ref, rwl_ref, tri_ref, meta_ref, gate_ref, cnt_ref, carry_ref):
    @pl.when(pl.program_id(0) == 0)
    def _():
        carry_ref[...] = jnp.zeros_like(carry_ref)

    x = x_ref[...]
    x_hi = x.astype(BF16)
    x_lo = (x - x_hi.astype(F32)).astype(BF16)
    logits = (jnp.dot(x_hi, rwh_ref[...], preferred_element_type=F32)
              + jnp.dot(x_lo, rwh_ref[...], preferred_element_type=F32)
              + jnp.dot(x_hi, rwl_ref[...], preferred_element_type=F32))
    lane = lax.broadcasted_iota(jnp.int32, logits.shape, 1)
    neg = -jnp.inf
    l1 = jnp.where(lane < N_EXPERTS, logits, neg)
    v1 = jnp.max(l1, axis=-1, keepdims=True)
    i1 = jnp.min(jnp.where(l1 == v1, lane, ROUTER_LANES), axis=-1, keepdims=True)
    l2 = jnp.where(lane == i1, neg, l1)
    v2 = jnp.max(l2, axis=-1, keepdims=True)
    i2 = jnp.min(jnp.where(l2 == v2, lane, ROUTER_LANES), axis=-1, keepdims=True)
    e2 = jnp.exp(v2 - v1)
    g1 = 1.0 / (1.0 + e2)
    g2 = e2 / (1.0 + e2)
    member = jnp.where((lane == i1) | (lane == i2), 1.0, 0.0)
    rank = jnp.dot(tri_ref[...], member.astype(BF16), preferred_element_type=F32) + carry_ref[0:1, :]
    r1 = jnp.sum(jnp.where(lane == i1, rank, 0.0), axis=-1, keepdims=True).astype(jnp.int32)
    r2 = jnp.sum(jnp.where(lane == i2, rank, 0.0), axis=-1, keepdims=True).astype(jnp.int32)
    meta = jnp.where(lane == META_I1, i1, 0) + jnp.where(lane == META_I2, i2, 0)
    meta = meta + jnp.where(lane == META_R1, r1, 0) + jnp.where(lane == META_R2, r2, 0)
    meta_ref[...] = meta
    gate_ref[...] = jnp.where(lane == 0, g1, 0.0) + jnp.where(lane == 1, g2, 0.0)
    carry = carry_ref[...] + jnp.sum(member, axis=0, keepdims=True)
    carry_ref[...] = carry
    cnt_ref[...] = carry


def _router(x, router_pad, tm=1024):
    T = x.shape[0]
    tri = (jnp.arange(tm)[:, None] > jnp.arange(tm)[None, :]).astype(BF16)
    rw_hi = router_pad.astype(BF16)
    rw_lo = (router_pad - rw_hi.astype(F32)).astype(BF16)
    return pl.pallas_call(
        _router_kernel,
        grid=(T // tm,),
        in_specs=[pl.BlockSpec((tm, D_MODEL), lambda i: (i, 0)),
                  pl.BlockSpec((D_MODEL, ROUTER_LANES), lambda i: (0, 0)),
                  pl.BlockSpec((D_MODEL, ROUTER_LANES), lambda i: (0, 0)),
                  pl.BlockSpec((tm, tm), lambda i: (0, 0))],
        out_specs=[pl.BlockSpec((tm, ROUTER_LANES), lambda i: (i, 0)),
                   pl.BlockSpec((tm, ROUTER_LANES), lambda i: (i, 0)),
                   pl.BlockSpec((8, ROUTER_LANES), lambda i: (0, 0))],
        out_shape=[jax.ShapeDtypeStruct((T, ROUTER_LANES), jnp.int32),
                   jax.ShapeDtypeStruct((T, ROUTER_LANES), F32),
                   jax.ShapeDtypeStruct((8, ROUTER_LANES), F32)],
        scratch_shapes=[pltpu.VMEM((8, ROUTER_LANES), F32)],
        compiler_params=_params("arbitrary"),
        name="moe_router",
    )(x, rw_hi, rw_lo, tri)


def _dispatch_tables(meta, cnt, tm):
    T = meta.shape[0]
    counts = cnt[0, :N_EXPERTS].astype(jnp.int32)
    padded = ((counts + tm - 1) // tm) * tm
    ends = jnp.cumsum(padded)
    off = ends - padded
    pos1 = off[meta[:, META_I1]] + meta[:, META_R1]
    pos2 = off[meta[:, META_I2]] + meta[:, META_R2]
    n_tiles = 2 * T // tm + N_EXPERTS + 1
    tok = jnp.arange(T, dtype=jnp.int32)
    code_of = jnp.full((n_tiles * tm,), 2 * T, jnp.int32).at[jnp.concatenate([pos1, pos2])].set(
        jnp.concatenate([tok, tok + T]))
    tile_start = jnp.arange(n_tiles, dtype=jnp.int32) * tm
    tile_expert = jnp.minimum(jnp.sum((tile_start[:, None] >= ends[None, :]).astype(jnp.int32), axis=1),
                              N_EXPERTS - 1)
    n_used = (ends[-1] // tm).astype(jnp.int32).reshape(1)
    row = jnp.arange(n_tiles * tm, dtype=jnp.int32)
    is_pad = code_of >= 2 * T
    src_of = jnp.where(is_pad, 0, jnp.where(code_of >= T, code_of - T, code_of))
    dst_of = jnp.where(is_pad, 2 * T + ((row // tm) % 2) * tm + row % tm, code_of)
    return src_of.reshape(n_tiles, 1, tm), dst_of.reshape(n_tiles, 1, tm), tile_expert, n_used


def _expert_kernel(te_ref, nu_ref, cur_ref, nxt_ref, prv_ref, x_hbm, wg_ref, wu_ref, wd_ref, y_hbm,
                   xbuf_ref, xb_ref, acc_ref, sem_g, sem_s, *, tm, nf, n_tok):
    i = pl.program_id(0)
    f = pl.program_id(1)
    n_used = nu_ref[0]
    slot = i % 2
    other = 1 - slot
    pad_code = 2 * n_tok
    rows_per_step = tm // nf

    def gather(src_ref, r, buf):
        return pltpu.make_async_copy(x_hbm.at[pl.ds(src_ref[0, 0, r], 1)],
                                     xbuf_ref.at[buf, pl.ds(r, 1)], sem_g.at[buf])

    def scatter(r, buf, to_dump):
        dest = jnp.where(to_dump, pad_code + buf * tm + r, prv_ref[0, 0, r])
        return pltpu.make_async_copy(acc_ref.at[buf, pl.ds(r, 1)], y_hbm.at[pl.ds(dest, 1)], sem_s)

    def wait_gather(buf):
        pltpu.make_async_copy(x_hbm.at[pl.ds(0, tm)], xbuf_ref.at[buf], sem_g.at[buf]).wait()

    def wait_scatter(buf):
        pltpu.make_async_copy(acc_ref.at[buf], y_hbm.at[pl.ds(0, tm)], sem_s).wait()

    @pl.when((i < n_used) & (f == 0))
    def _():
        @pl.when(i == 0)
        def _():
            def issue(r, carry):
                gather(cur_ref, r, 0).start()
                return carry
            lax.fori_loop(0, tm, issue, 0, unroll=8)
            acc_ref[1] = jnp.zeros((tm, D_MODEL), F32)
            clear = pltpu.make_async_copy(acc_ref.at[1], y_hbm.at[pl.ds(pad_code, tm)], sem_s)
            clear.start()
            clear.wait()

        wait_gather(slot)
        xb_ref[...] = xbuf_ref[slot].astype(BF16)
        acc_ref[slot] = jnp.zeros((tm, D_MODEL), F32)

    def tile_step(fs):
        for r in range(fs * rows_per_step, (fs + 1) * rows_per_step):
            gather(nxt_ref, r, other).start()
            scatter(r, other, i == 0).start()
        xb = xb_ref[...]
        gate = jnp.dot(xb, wg_ref[0], preferred_element_type=F32)
        up = jnp.dot(xb, wu_ref[0], preferred_element_type=F32)
        hmid = (gate * jax.nn.sigmoid(gate) * up).astype(BF16)
        acc_ref[slot] += jnp.dot(hmid, wd_ref[0], preferred_element_type=F32)
        if fs == nf - 1:
            wait_scatter(other)

    for fs in range(nf):
        pl.when((i < n_used) & (f == fs))(functools.partial(tile_step, fs))

    @pl.when((i == n_used) & (f == 0))
    def _():
        wait_gather(slot)

        def issue(r, carry):
            scatter(r, other, False).start()
            return carry
        lax.fori_loop(0, tm, issue, 0, unroll=8)
        wait_scatter(other)


def _experts(x, src_of, dst_of, tile_expert, n_used, wg, wu, wd, tm, tf=1792):
    n_tok = x.shape[0]
    n_tiles = src_of.shape[0]
    nf = D_FF // tf
    fidx = lambda i, f, nu: jnp.where(i < nu[0], f, nf - 1)
    rows_of_tile = lambda shift: pl.BlockSpec(
        (1, 1, tm), lambda i, f, te, nu: (jnp.clip(i + shift, 0, n_tiles - 1), 0, 0),
        memory_space=pltpu.SMEM)
    grid_spec = pltpu.PrefetchScalarGridSpec(
        num_scalar_prefetch=2,
        grid=(n_tiles, nf),
        in_specs=[rows_of_tile(0), rows_of_tile(1), rows_of_tile(-1),
                  pl.BlockSpec(memory_space=pl.ANY),
                  pl.BlockSpec((1, D_MODEL, tf), lambda i, f, te, nu: (te[i], 0, fidx(i, f, nu))),
                  pl.BlockSpec((1, D_MODEL, tf), lambda i, f, te, nu: (te[i], 0, fidx(i, f, nu))),
                  pl.BlockSpec((1, tf, D_MODEL), lambda i, f, te, nu: (te[i], fidx(i, f, nu), 0))],
        out_specs=pl.BlockSpec(memory_space=pl.ANY),
        scratch_shapes=[pltpu.VMEM((2, tm, D_MODEL), F32), pltpu.VMEM((tm, D_MODEL), BF16),
                        pltpu.VMEM((2, tm, D_MODEL), F32),
                        pltpu.SemaphoreType.DMA((2,)), pltpu.SemaphoreType.DMA(())],
    )
    return pl.pallas_call(
        functools.partial(_expert_kernel, tm=tm, nf=nf, n_tok=n_tok),
        grid_spec=grid_spec,
        out_shape=jax.ShapeDtypeStruct((2 * n_tok + 2 * tm, D_MODEL), F32),
        compiler_params=_params("arbitrary", "arbitrary"),
        name="moe_experts",
    )(tile_expert, n_used, src_of, src_of, dst_of, x, wg, wu, wd)


def _combine_kernel(x_ref, y1_ref, y2_ref, gate_ref, g_ref, b_ref, o_ref):
    gates = gate_ref[...]
    y = gates[:, 0:1] * y1_ref[...] + gates[:, 1:2] * y2_ref[...]
    o_ref[...] = _layer_norm_rows(DN_ALPHA * x_ref[...] + y, g_ref[...], b_ref[...])


def _combine_ln(x, y, gates, g, b, tm=512):
    T = x.shape[0]
    nt = T // tm
    return pl.pallas_call(
        _combine_kernel,
        grid=(nt,),
        in_specs=[pl.BlockSpec((tm, D_MODEL), lambda i: (i, 0)),
                  pl.BlockSpec((tm, D_MODEL), lambda i: (i, 0)),
                  pl.BlockSpec((tm, D_MODEL), lambda i: (nt + i, 0)),
                  pl.BlockSpec((tm, ROUTER_LANES), lambda i: (i, 0)),
                  pl.BlockSpec((1, D_MODEL), lambda i: (0, 0)),
                  pl.BlockSpec((1, D_MODEL), lambda i: (0, 0))],
        out_specs=pl.BlockSpec((tm, D_MODEL), lambda i: (i, 0)),
        out_shape=jax.ShapeDtypeStruct((T, D_MODEL), F32),
        compiler_params=_params("parallel"),
        name="moe_combine_ln",
    )(x, y, y, gates, g.reshape(1, -1), b.reshape(1, -1))


def _moe_ln(x, router_w, wg, wu, wd, g, b, tm=512):
    router_pad = jnp.pad(router_w, ((0, 0), (0, ROUTER_LANES - N_EXPERTS)))
    meta, gates, cnt = _router(x, router_pad)
    src_of, dst_of, tile_expert, n_used = _dispatch_tables(meta, cnt, tm)
    y = _experts(x, src_of, dst_of, tile_expert, n_used, wg, wu, wd, tm)
    return _combine_ln(x, y, gates, g, b)


def _block_diag(w):
    nb, n, _ = w.shape
    eye = jnp.eye(nb, dtype=w.dtype)
    return (eye[:, None, :, None] * w[:, :, None, :]).reshape(nb * n, nb * n)


def kernel(x, w_in, w_out, hg_lb_logits, hg_norm_g, lru_conv_w, lru_conv_b, lru_wa, lru_ba, lru_wx, lru_bx, lru_lambda, da_lq1, da_lk1, da_lq2, da_lk2, da_norm_g, ln1_g, ln1_b, ln2_g, ln2_b, ffn_wg, ffn_wu, ffn_wd, router_w, moe_wg, moe_wu, moe_wd):
    batch, seq, _ = x.shape
    depth = w_in.shape[0]
    xf = x.reshape(batch * seq, D_MODEL)
    lb_p = jax.nn.softmax(hg_lb_logits.astype(F32), axis=0)
    lb_all = jnp.cumsum(lb_p, axis=0) - lb_p[0:1]
    for l in range(depth):
        rec, att = _in_proj(xf, w_in[l].astype(BF16))
        o_hg = _hgrn(rec, lb_all[l], hg_norm_g[l], batch, seq)
        o_lru = _lru(rec, lru_conv_w[l], lru_conv_b[l], _block_diag(lru_wa[l]).astype(BF16),
                     _block_diag(lru_wx[l]).astype(BF16), lru_ba[l], lru_bx[l], lru_lambda[l],
                     batch, seq)
        o_da = _attention(att, da_lq1[l], da_lk1[l], da_lq2[l], da_lk2[l], da_norm_g[l], l,
                          batch, seq)
        xf = _out_proj_ln(o_hg, o_lru, o_da, w_out[l].astype(BF16), xf, ln1_g[l], ln1_b[l])
        j = l // 2
        if l % 2 == 0:
            xf = _ffn_ln(xf, ffn_wg[j].astype(BF16), ffn_wu[j].astype(BF16), ffn_wd[j].astype(BF16),
                         ln2_g[l], ln2_b[l])
        else:
            xf = _moe_ln(xf, router_w[j], moe_wg[j].astype(BF16), moe_wu[j].astype(BF16),
                         moe_wd[j].astype(BF16), ln2_g[l], ln2_b[l])
    return xf.reshape(batch, seq, D_MODEL)
```

```python
import functools
import math

import jax
import jax.numpy as jnp
from jax import lax
from jax.experimental import pallas as pl
from jax.experimental.pallas import tpu as pltpu

D_MODEL = 1024
DEPTH = 2
HG_HEADS = 4
HG_KEY_DIM = 64
HG_WIDTH = 256
HG_CHUNK = 16
LRU_WIDTH = 256
LRU_BLOCKS = 4
CONV_WIDTH = 4
LRU_C = 8.0
DA_WIDTH = 512
DA_HEADS = 4
DA_HEAD_DIM = 64
D_FF = 3584
N_EXPERTS = 8
DN_ALPHA = (2.0 * DEPTH) ** 0.25
EPS = 1e-5
REC_WIDTH = 4 * HG_WIDTH + 2 * LRU_WIDTH
ATT_WIDTH = 3 * DA_WIDTH

V7X_VMEM_BYTES = 64 * 1024 * 1024
VMEM_LIMIT = 48 * 1024 * 1024

BF16 = jnp.bfloat16
F32 = jnp.float32


def _params(*semantics):
    return pltpu.CompilerParams(dimension_semantics=semantics, vmem_limit_bytes=VMEM_LIMIT)


def _layer_norm_rows(y, g, b):
    mu = jnp.mean(y, axis=-1, keepdims=True)
    yc = y - mu
    var = jnp.mean(yc * yc, axis=-1, keepdims=True)
    return yc * lax.rsqrt(var + EPS) * g + b


Q_SCALE = DA_HEAD_DIM ** -0.5 * math.log2(math.e)


def _in_proj_kernel(x_ref, w_ref, rec_ref, att_ref):
    xb = x_ref[...].astype(BF16)
    rec_ref[...] = jnp.dot(xb, w_ref[:, :REC_WIDTH], preferred_element_type=F32)
    q = jnp.dot(xb, w_ref[:, REC_WIDTH:REC_WIDTH + DA_WIDTH], preferred_element_type=F32)
    att_ref[:, :DA_WIDTH] = (q * Q_SCALE).astype(BF16)
    att_ref[:, DA_WIDTH:] = jnp.dot(xb, w_ref[:, REC_WIDTH + DA_WIDTH:],
                                    preferred_element_type=F32).astype(BF16)


def _in_proj(x, w_bf16, tm=512):
    T = x.shape[0]
    return pl.pallas_call(
        _in_proj_kernel,
        grid=(T // tm,),
        in_specs=[pl.BlockSpec((tm, D_MODEL), lambda i: (i, 0)),
                  pl.BlockSpec((D_MODEL, REC_WIDTH + ATT_WIDTH), lambda i: (0, 0))],
        out_specs=[pl.BlockSpec((tm, REC_WIDTH), lambda i: (i, 0)),
                   pl.BlockSpec((tm, ATT_WIDTH), lambda i: (i, 0))],
        out_shape=[jax.ShapeDtypeStruct((T, REC_WIDTH), F32),
                   jax.ShapeDtypeStruct((T, ATT_WIDTH), BF16)],
        compiler_params=_params("parallel"),
        name="in_proj",
    )(x, w_bf16)


def _hgrn_kernel(rec_ref, lb_ref, ng_ref, gmat_ref, sel_ref, bmask_ref, o_ref,
                 st_ref, q_s, kk_s, b_s, bl_s, qd_s, kd_s, o_s, *, rows):
    @pl.when(pl.program_id(1) == 0)
    def _():
        st_ref[...] = jnp.zeros_like(st_ref)

    C = HG_CHUNK
    W = HG_WIDTH
    nc = rows // C
    lb = lb_ref[...]
    gmat = gmat_ref[...]
    sel = sel_ref[...]

    qr = rec_ref[:, 0:W]
    z = rec_ref[:, W:2 * W]
    rin = lax.broadcasted_iota(jnp.int32, (rows, W), 0) % C
    logf = jnp.log(lb + (1.0 - lb) * jax.nn.sigmoid(z))
    kk = (1.0 - lb) * jax.nn.sigmoid(-z)
    q = qr * jax.nn.sigmoid(qr)
    b = logf
    for s in (1, 2, 4, 8):
        b = b + jnp.where(rin >= s, pltpu.roll(b, s, 0), 0.0)
    r = jnp.where(rin < C - 1, pltpu.roll(logf, rows - 1, 0), 0.0)
    for s in (1, 2, 4, 8):
        r = r + jnp.where(rin < C - s, pltpu.roll(r, rows - s, 0), 0.0)
    q_s[...] = q
    kk_s[...] = kk
    b_s[...] = b
    bl_s[...] = b + r
    qd_s[...] = (q * jnp.exp(b)).astype(BF16)
    kd_s[...] = (kk * jnp.exp(r)).astype(BF16)

    row = lax.broadcasted_iota(jnp.int32, (C, W), 0)

    def chunk(c, carry):
        r0 = pl.multiple_of(c * C, C)
        q = q_s[pl.ds(r0, C), :]
        kk = kk_s[pl.ds(r0, C), :]
        b = b_s[pl.ds(r0, C), :]
        v = rec_ref[pl.ds(r0, C), 2 * W:3 * W]
        w_rows = []
        for t in range(C):
            rel = b[t:t + 1, :] - b
            dec = jnp.exp(jnp.where(row <= t, rel, -jnp.inf))
            w_rows.append(((q[t:t + 1, :] * kk) * dec).astype(BF16))
        w2 = jnp.concatenate(w_rows, axis=0)
        a = jnp.dot(w2, gmat, preferred_element_type=F32)
        p = a * jnp.concatenate([v] * C, axis=0)
        o_intra = jnp.dot(sel, p.astype(BF16), preferred_element_type=F32)
        u_t = lax.dot_general(v.astype(BF16), kd_s[pl.ds(r0, C), :], (((0,), (0,)), ((), ())),
                              preferred_element_type=F32)
        st = st_ref[...]
        o_inter = lax.dot_general(qd_s[pl.ds(r0, C), :], st.astype(BF16), (((1,), (1,)), ((), ())),
                                  preferred_element_type=F32)
        o_s[pl.ds(r0, C), :] = o_intra + o_inter
        st_ref[...] = st * jnp.exp(bl_s[pl.ds(r0, 1), :]) + u_t * bmask_ref[...]
        return carry

    lax.fori_loop(0, nc, chunk, 0, unroll=8)

    o = o_s[...]
    g = rec_ref[:, 3 * W:4 * W]
    sq = o * o
    sq_hi = sq.astype(BF16)
    sq_lo = (sq - sq_hi.astype(F32)).astype(BF16)
    ms = (jnp.dot(sq_hi, gmat, preferred_element_type=F32)
          + jnp.dot(sq_lo, gmat, preferred_element_type=F32)) * (1.0 / HG_KEY_DIM)
    out = o * lax.rsqrt(ms + EPS) * ng_ref[...] * (g * jax.nn.sigmoid(g))
    o_ref[...] = out.astype(o_ref.dtype)


def _hgrn(rec, lb, norm_g, batch, seq, rows=256):
    T = rec.shape[0]
    nblk = seq // rows
    head = jnp.arange(HG_WIDTH) // HG_KEY_DIM
    same_head = head[:, None] == head[None, :]
    gmat = same_head.astype(BF16)
    bmask = same_head.astype(F32)
    sel = (jnp.arange(HG_CHUNK)[:, None] == (jnp.arange(HG_CHUNK * HG_CHUNK) // HG_CHUNK)[None, :]).astype(BF16)
    const = lambda shape: pl.BlockSpec(shape, lambda b, i: (0, 0))
    return pl.pallas_call(
        functools.partial(_hgrn_kernel, rows=rows),
        grid=(batch, nblk),
        in_specs=[pl.BlockSpec((rows, 4 * HG_WIDTH), lambda b, i: (b * nblk + i, 0)),
                  const((1, HG_WIDTH)), const((1, HG_WIDTH)),
                  const((HG_WIDTH, HG_WIDTH)), const((HG_CHUNK, HG_CHUNK * HG_CHUNK)),
                  const((HG_WIDTH, HG_WIDTH))],
        out_specs=pl.BlockSpec((rows, HG_WIDTH), lambda b, i: (b * nblk + i, 0)),
        out_shape=jax.ShapeDtypeStruct((T, HG_WIDTH), BF16),
        scratch_shapes=[pltpu.VMEM((HG_WIDTH, HG_WIDTH), F32)]
        + [pltpu.VMEM((rows, HG_WIDTH), F32)] * 4
        + [pltpu.VMEM((rows, HG_WIDTH), BF16)] * 2
        + [pltpu.VMEM((rows, HG_WIDTH), F32)],
        compiler_params=_params("parallel", "arbitrary"),
        name="hgrn2",
    )(rec, lb.reshape(1, -1), norm_g.reshape(1, -1), gmat, sel, bmask)


def _lru_kernel(rec_ref, cw_ref, cb_ref, wa_ref, wx_ref, ba_ref, bx_ref, lam_ref, o_ref,
                xprev_ref, h_ref, *, rows):
    @pl.when(pl.program_id(1) == 0)
    def _():
        xprev_ref[...] = jnp.zeros_like(xprev_ref)
        h_ref[...] = jnp.zeros_like(h_ref)

    W = LRU_WIDTH
    x = rec_ref[:, 0:W]
    gate = rec_ref[:, W:2 * W]
    row = lax.broadcasted_iota(jnp.int32, (rows, W), 0)
    xp = xprev_ref[...]
    tail = jnp.zeros((rows - 8, W), F32)
    xc = cb_ref[...] + cw_ref[CONV_WIDTH - 1:CONV_WIDTH, :] * x
    for j in range(1, CONV_WIDTH):
        prev = jnp.concatenate([pltpu.roll(xp, j, 0), tail], axis=0)
        xs = jnp.where(row >= j, pltpu.roll(x, j, 0), prev)
        xc = xc + cw_ref[CONV_WIDTH - 1 - j:CONV_WIDTH - j, :] * xs
    xprev_ref[...] = x[rows - 8:rows, :]

    xcb = xc.astype(BF16)
    r = jax.nn.sigmoid(jnp.dot(xcb, wa_ref[...], preferred_element_type=F32) + ba_ref[...])
    i = jax.nn.sigmoid(jnp.dot(xcb, wx_ref[...], preferred_element_type=F32) + bx_ref[...])
    lam = lam_ref[...]
    log_sig = jnp.minimum(lam, 0.0) - jnp.log1p(jnp.exp(-jnp.abs(lam)))
    log_a = LRU_C * r * log_sig
    a = jnp.exp(log_a)
    th = jnp.tanh(log_a)
    one_minus_a2 = -2.0 * th / (1.0 - th)
    u = jnp.sqrt(one_minus_a2) * i * xc

    for s in (1, 2, 4):
        a_s = jnp.where(row >= s, pltpu.roll(a, s, 0), 1.0)
        u_s = jnp.where(row >= s, pltpu.roll(u, s, 0), 0.0)
        u = a * u_s + u
        a = a * a_s
    s = 8
    while s < rows:
        a_s = jnp.concatenate([jnp.ones((s, W), F32), a[:rows - s, :]], axis=0)
        u_s = jnp.concatenate([jnp.zeros((s, W), F32), u[:rows - s, :]], axis=0)
        u = a * u_s + u
        a = a * a_s
        s *= 2
    h = a * h_ref[0:1, :] + u
    h_ref[...] = jnp.broadcast_to(h[rows - 1:rows, :], h_ref.shape)
    o_ref[...] = (h * jax.nn.gelu(gate)).astype(o_ref.dtype)


def _lru(rec, conv_w, conv_b, wa_bd, wx_bd, ba, bx, lam, batch, seq, rows=256):
    T = rec.shape[0]
    nblk = seq // rows
    W = LRU_WIDTH
    const = lambda shape: pl.BlockSpec(shape, lambda b, i: (0, 0))
    return pl.pallas_call(
        functools.partial(_lru_kernel, rows=rows),
        grid=(batch, nblk),
        in_specs=[pl.BlockSpec((rows, 2 * W), lambda b, i: (b * nblk + i, 2)),
                  const((CONV_WIDTH, W)), const((1, W)), const((W, W)), const((W, W)),
                  const((1, W)), const((1, W)), const((1, W))],
        out_specs=pl.BlockSpec((rows, W), lambda b, i: (b * nblk + i, 0)),
        out_shape=jax.ShapeDtypeStruct((T, W), BF16),
        scratch_shapes=[pltpu.VMEM((8, W), F32), pltpu.VMEM((8, W), F32)],
        compiler_params=_params("parallel", "arbitrary"),
        name="rglru",
    )(rec, conv_w, conv_b.reshape(1, -1), wa_bd, wx_bd, ba.reshape(1, -1), bx.reshape(1, -1),
      lam.reshape(1, -1))


def _attn_kernel(q_ref, k_ref, v_ref, lq1_ref, lk1_ref, lq2_ref, lk2_ref, ng_ref, o_ref,
                 vt_ref, qm_ref, s0_ref, s1_ref, m_ref, l_ref, acc_ref, *, tq, tk, qs, lam_init, seq):
    d = DA_HEAD_DIM
    qi = pl.program_id(2)

    @pl.when(qi == 0)
    def _():
        def transpose_values(j, carry):
            r0 = pl.multiple_of(j * tk, tk)
            vt_ref[j] = v_ref[pl.ds(r0, tk), :].astype(F32).T.astype(BF16)
            return carry
        lax.fori_loop(0, seq // tk, transpose_values, 0)

    q = q_ref[...]
    lane = lax.broadcasted_iota(jnp.int32, q.shape, 1)
    qm_ref[0] = jnp.where(lane < d, q, jnp.zeros_like(q))
    qm_ref[1] = jnp.where(lane >= d, q, jnp.zeros_like(q))
    m_ref[...] = jnp.full_like(m_ref, -jnp.inf)
    l_ref[...] = jnp.zeros_like(l_ref)
    acc_ref[...] = jnp.zeros_like(acc_ref)
    s_refs = (s0_ref, s1_ref)

    def scores(j, c):
        k0 = pl.multiple_of(j * tk, tk)
        kb = k_ref[pl.ds(k0, tk), :]
        for st in range(tq // qs):
            s_refs[c][st] = lax.dot_general(kb, qm_ref[c, st * qs:(st + 1) * qs, :],
                                            (((1,), (1,)), ((), ())),
                                            preferred_element_type=F32)

    def softmax_pv(j, c, masked):
        for st in range(tq // qs):
            cols = pl.ds(st * qs, qs)
            nk = min((st + 1) * qs, tk) if masked else tk
            vt = vt_ref[j, :, 0:nk]
            s = s_refs[c][st, 0:nk, :]
            if masked:
                kpos = lax.broadcasted_iota(jnp.int32, (nk, qs), 0)
                qpos = st * qs + lax.broadcasted_iota(jnp.int32, (nk, qs), 1)
                s = jnp.where(kpos <= qpos, s, -jnp.inf)
            m_old = m_ref[c, :, cols]
            m_new = jnp.maximum(m_old, jnp.max(s, axis=0, keepdims=True))
            p = jnp.exp2(s - m_new)
            alpha = jnp.exp2(m_old - m_new)
            l_ref[c, :, cols] = alpha * l_ref[c, :, cols] + jnp.sum(p, axis=0, keepdims=True)
            acc_ref[c, st] = (alpha * acc_ref[c, st]
                                   + jnp.dot(vt, p.astype(BF16), preferred_element_type=F32))
            m_ref[c, :, cols] = m_new

    def full_block(j, carry):
        scores(j, 1)
        softmax_pv(j, 0, False)
        scores(j + 1, 0)
        softmax_pv(j, 1, False)
        return carry

    def two_full_blocks(jj, carry):
        full_block(2 * jj, carry)
        return full_block(2 * jj + 1, carry)

    scores(0, 0)
    lax.fori_loop(0, qi // 2, two_full_blocks, 0)

    @pl.when(qi % 2 == 1)
    def _():
        full_block(qi - 1, 0)

    scores(qi, 1)
    softmax_pv(qi, 0, True)
    softmax_pv(qi, 1, True)

    lam =(jnp.exp(jnp.sum(lq1_ref[...] * lk1_ref[...], axis=-1, keepdims=True))
           - jnp.exp(jnp.sum(lq2_ref[...] * lk2_ref[...], axis=-1, keepdims=True)) + lam_init)
    acc = [jnp.concatenate([acc_ref[c, st] for st in range(tq // qs)], axis=1) for c in range(2)]
    o_t = acc[0] * (1.0 / l_ref[0]) - lam * (acc[1] * (1.0 / l_ref[1]))
    ms = jnp.mean(o_t * o_t, axis=0, keepdims=True)
    o_t = o_t * lax.rsqrt(ms + EPS) * ng_ref[...] * (1.0 - lam_init)
    o_ref[...] = o_t.T.astype(o_ref.dtype)


def _attention(att, lq1, lk1, lq2, lk2, norm_g, layer, batch, seq, tq=512, qs=256):
    T = att.shape[0]
    tk = tq
    nq = seq // tq
    hw = 2 * DA_HEAD_DIM
    lam_init = 0.8 - 0.6 * math.exp(-0.3 * layer)
    vec = lambda: pl.BlockSpec((1, DA_HEAD_DIM), lambda b, h, i: (0, 0))
    return pl.pallas_call(
        functools.partial(_attn_kernel, tq=tq, tk=tk, qs=qs, lam_init=lam_init, seq=seq),
        grid=(batch, DA_HEADS, nq),
        in_specs=[pl.BlockSpec((tq, hw), lambda b, h, i: (b * nq + i, h)),
                  pl.BlockSpec((seq, hw), lambda b, h, i: (b, DA_HEADS + h)),
                  pl.BlockSpec((seq, hw), lambda b, h, i: (b, 2 * DA_HEADS + h)),
                  vec(), vec(), vec(), vec(),
                  pl.BlockSpec((hw, 1), lambda b, h, i: (h, 0))],
        out_specs=pl.BlockSpec((tq, hw), lambda b, h, i: (b * nq + i, h)),
        out_shape=jax.ShapeDtypeStruct((T, DA_WIDTH), BF16),
        scratch_shapes=[pltpu.VMEM((seq // tk, hw, tk), BF16), pltpu.VMEM((2, tq, hw), BF16),
                        pltpu.VMEM((tq // qs, tk, qs), F32), pltpu.VMEM((tq // qs, tk, qs), F32),
                        pltpu.VMEM((2, 1, tq), F32), pltpu.VMEM((2, 1, tq), F32),
                        pltpu.VMEM((2, tq // qs, hw, qs), F32)],
        compiler_params=_params("arbitrary", "arbitrary", "arbitrary"),
        name="diff_attn",
    )(att, att, att, lq1.reshape(1, -1), lk1.reshape(1, -1), lq2.reshape(1, -1),
      lk2.reshape(1, -1), norm_g.reshape(-1, 1))


def _out_proj_kernel(hg_ref, lru_ref, da_ref, w_ref, x_ref, g_ref, b_ref, o_ref):
    h = jnp.dot(hg_ref[...], w_ref[0:HG_WIDTH, :], preferred_element_type=F32)
    h = h + jnp.dot(lru_ref[...], w_ref[HG_WIDTH:HG_WIDTH + LRU_WIDTH, :], preferred_element_type=F32)
    h = h + jnp.dot(da_ref[...], w_ref[HG_WIDTH + LRU_WIDTH:, :], preferred_element_type=F32)
    o_ref[...] = _layer_norm_rows(DN_ALPHA * x_ref[...] + h, g_ref[...], b_ref[...])


def _out_proj_ln(o_hg, o_lru, o_da, w_bf16, x, g, b, tm=512):
    T = x.shape[0]
    rows = lambda w: pl.BlockSpec((tm, w), lambda i: (i, 0))
    const = lambda shape: pl.BlockSpec(shape, lambda i: (0, 0))
    return pl.pallas_call(
        _out_proj_kernel,
        grid=(T // tm,),
        in_specs=[rows(HG_WIDTH), rows(LRU_WIDTH), rows(DA_WIDTH), const((D_MODEL, D_MODEL)),
                  rows(D_MODEL), const((1, D_MODEL)), const((1, D_MODEL))],
        out_specs=rows(D_MODEL),
        out_shape=jax.ShapeDtypeStruct((T, D_MODEL), F32),
        compiler_params=_params("parallel"),
        name="out_proj_ln",
    )(o_hg, o_lru, o_da, w_bf16, x, g.reshape(1, -1), b.reshape(1, -1))


def _ffn_kernel(x_ref, wg_ref, wu_ref, wd_ref, g_ref, b_ref, o_ref, xb_ref, acc_ref):
    f = pl.program_id(1)

    @pl.when(f == 0)
    def _():
        xb_ref[...] = x_ref[...].astype(BF16)
        acc_ref[...] = jnp.zeros_like(acc_ref)

    xb = xb_ref[...]
    gate = jnp.dot(xb, wg_ref[...], preferred_element_type=F32)
    up = jnp.dot(xb, wu_ref[...], preferred_element_type=F32)
    hmid = (gate * jax.nn.sigmoid(gate) * up).astype(BF16)
    acc_ref[...] += jnp.dot(hmid, wd_ref[...], preferred_element_type=F32)

    @pl.when(f == pl.num_programs(1) - 1)
    def _():
        o_ref[...] = _layer_norm_rows(DN_ALPHA * x_ref[...] + acc_ref[...], g_ref[...], b_ref[...])


def _ffn_ln(x, wg, wu, wd, g, b, tm=1024, tf=512):
    T = x.shape[0]
    return pl.pallas_call(
        _ffn_kernel,
        grid=(T // tm, D_FF // tf),
        in_specs=[pl.BlockSpec((tm, D_MODEL), lambda i, f: (i, 0)),
                  pl.BlockSpec((D_MODEL, tf), lambda i, f: (0, f)),
                  pl.BlockSpec((D_MODEL, tf), lambda i, f: (0, f)),
                  pl.BlockSpec((tf, D_MODEL), lambda i, f: (f, 0)),
                  pl.BlockSpec((1, D_MODEL), lambda i, f: (0, 0)),
                  pl.BlockSpec((1, D_MODEL), lambda i, f: (0, 0))],
        out_specs=pl.BlockSpec((tm, D_MODEL), lambda i, f: (i, 0)),
        out_shape=jax.ShapeDtypeStruct((T, D_MODEL), F32),
        scratch_shapes=[pltpu.VMEM((tm, D_MODEL), BF16), pltpu.VMEM((tm, D_MODEL), F32)],
        compiler_params=_params("parallel", "arbitrary"),
        name="ffn_ln",
    )(x, wg, wu, wd, g.reshape(1, -1), b.reshape(1, -1))


ROUTER_LANES = 128
META_I1, META_I2, META_R1, META_R2 = 0, 1, 2, 3


def _router_kernel(x_ref, rwh_ref, rwl_ref, tri_ref, meta_ref, gate_ref, cnt_ref, carry_ref):
    @pl.when(pl.program_id(0) == 0)
    def _():
        carry_ref[...] = jnp.zeros_like(carry_ref)

    x = x_ref[...]
    x_hi = x.astype(BF16)
    x_lo = (x - x_hi.astype(F32)).astype(BF16)
    logits = (jnp.dot(x_hi, rwh_ref[...], preferred_element_type=F32)
              + jnp.dot(x_lo, rwh_ref[...], preferred_element_type=F32)
              + jnp.dot(x_hi, rwl_ref[...], preferred_element_type=F32))
    lane = lax.broadcasted_iota(jnp.int32, logits.shape, 1)
    neg = -jnp.inf
    l1 = jnp.where(lane < N_EXPERTS, logits, neg)
    v1 = jnp.max(l1, axis=-1, keepdims=True)
    i1 = jnp.min(jnp.where(l1 == v1, lane, ROUTER_LANES), axis=-1, keepdims=True)
    l2 = jnp.where(lane == i1, neg, l1)
    v2 = jnp.max(l2, axis=-1, keepdims=True)
    i2 = jnp.min(jnp.where(l2 == v2, lane, ROUTER_LANES), axis=-1, keepdims=True)
    e2 = jnp.exp(v2 - v1)
    g1 = 1.0 / (1.0 + e2)
    g2 = e2 / (1.0 + e2)
    member = jnp.where((lane == i1) | (lane == i2), 1.0, 0.0)
    rank = jnp.dot(tri_ref[...], member.astype(BF16), preferred_element_type=F32) + carry_ref[0:1, :]
    r1 = jnp.sum(jnp.where(lane == i1, rank, 0.0), axis=-1, keepdims=True).astype(jnp.int32)
    r2 = jnp.sum(jnp.where(lane == i2, rank, 0.0), axis=-1, keepdims=True).astype(jnp.int32)
    meta = jnp.where(lane == META_I1, i1, 0) + jnp.where(lane == META_I2, i2, 0)
    meta = meta + jnp.where(lane == META_R1, r1, 0) + jnp.where(lane == META_R2, r2, 0)
    meta_ref[...] = meta
    gate_ref[...] = jnp.where(lane == 0, g1, 0.0) + jnp.where(lane == 1, g2, 0.0)
    carry = carry_ref[...] + jnp.sum(member, axis=0, keepdims=True)
    carry_ref[...] = carry
    cnt_ref[...] = carry


def _router(x, router_pad, tm=1024):
    T = x.shape[0]
    tri = (jnp.arange(tm)[:, None] > jnp.arange(tm)[None, :]).astype(BF16)
    rw_hi = router_pad.astype(BF16)
    rw_lo = (router_pad - rw_hi.astype(F32)).astype(BF16)
    return pl.pallas_call(
        _router_kernel,
        grid=(T // tm,),
        in_specs=[pl.BlockSpec((tm, D_MODEL), lambda i: (i, 0)),
                  pl.BlockSpec((D_MODEL, ROUTER_LANES), lambda i: (0, 0)),
                  pl.BlockSpec((D_MODEL, ROUTER_LANES), lambda i: (0, 0)),
                  pl.BlockSpec((tm, tm), lambda i: (0, 0))],
        out_specs=[pl.BlockSpec((tm, ROUTER_LANES), lambda i: (i, 0)),
                   pl.BlockSpec((tm, ROUTER_LANES), lambda i: (i, 0)),
                   pl.BlockSpec((8, ROUTER_LANES), lambda i: (0, 0))],
        out_shape=[jax.ShapeDtypeStruct((T, ROUTER_LANES), jnp.int32),
                   jax.ShapeDtypeStruct((T, ROUTER_LANES), F32),
                   jax.ShapeDtypeStruct((8, ROUTER_LANES), F32)],
        scratch_shapes=[pltpu.VMEM((8, ROUTER_LANES), F32)],
        compiler_params=_params("arbitrary"),
        name="moe_router",
    )(x, rw_hi, rw_lo, tri)


def _dispatch_tables(meta, cnt, tm):
    T = meta.shape[0]
    counts = cnt[0, :N_EXPERTS].astype(jnp.int32)
    padded = ((counts + tm - 1) // tm) * tm
    ends = jnp.cumsum(padded)
    off = ends - padded
    pos1 = off[meta[:, META_I1]] + meta[:, META_R1]
    pos2 = off[meta[:, META_I2]] + meta[:, META_R2]
    n_tiles = 2 * T // tm + N_EXPERTS + 1
    tok = jnp.arange(T, dtype=jnp.int32)
    code_of = jnp.full((n_tiles * tm,), 2 * T, jnp.int32).at[jnp.concatenate([pos1, pos2])].set(
        jnp.concatenate([tok, tok + T]))
    tile_start = jnp.arange(n_tiles, dtype=jnp.int32) * tm
    tile_expert = jnp.minimum(jnp.sum((tile_start[:, None] >= ends[None, :]).astype(jnp.int32), axis=1),
                              N_EXPERTS - 1)
    n_used = (ends[-1] // tm).astype(jnp.int32).reshape(1)
    row = jnp.arange(n_tiles * tm, dtype=jnp.int32)
    is_pad = code_of >= 2 * T
    src_of = jnp.where(is_pad, 0, jnp.where(code_of >= T, code_of - T, code_of))
    dst_of = jnp.where(is_pad, 2 * T + ((row // tm) % 2) * tm + row % tm, code_of)
    return src_of.reshape(n_tiles, 1, tm), dst_of.reshape(n_tiles, 1, tm), tile_expert, n_used


def _expert_kernel(te_ref, nu_ref, cur_ref, nxt_ref, prv_ref, x_hbm, wg_ref, wu_ref, wd_ref, y_hbm,
                   xbuf_ref, xb_ref, acc_ref, sem_g, sem_s, *, tm, nf, n_tok):
    i = pl.program_id(0)
    f = pl.program_id(1)
    n_used = nu_ref[0]
    slot = i % 2
    other = 1 - slot
    pad_code = 2 * n_tok
    rows_per_step = tm // nf

    def gather(src_ref, r, buf):
        return pltpu.make_async_copy(x_hbm.at[pl.ds(src_ref[0, 0, r], 1)],
                                     xbuf_ref.at[buf, pl.ds(r, 1)], sem_g.at[buf])

    def scatter(r, buf, to_dump):
        dest = jnp.where(to_dump, pad_code + buf * tm + r, prv_ref[0, 0, r])
        return pltpu.make_async_copy(acc_ref.at[buf, pl.ds(r, 1)], y_hbm.at[pl.ds(dest, 1)], sem_s)

    def wait_gather(buf):
        pltpu.make_async_copy(x_hbm.at[pl.ds(0, tm)], xbuf_ref.at[buf], sem_g.at[buf]).wait()

    def wait_scatter(buf):
        pltpu.make_async_copy(acc_ref.at[buf], y_hbm.at[pl.ds(0, tm)], sem_s).wait()

    @pl.when((i < n_used) & (f == 0))
    def _():
        @pl.when(i == 0)
        def _():
            def issue(r, carry):
                gather(cur_ref, r, 0).start()
                return carry
            lax.fori_loop(0, tm, issue, 0, unroll=8)
            acc_ref[1] = jnp.zeros((tm, D_MODEL), F32)
            clear = pltpu.make_async_copy(acc_ref.at[1], y_hbm.at[pl.ds(pad_code, tm)], sem_s)
            clear.start()
            clear.wait()

        wait_gather(slot)
        xb_ref[...] = xbuf_ref[slot].astype(BF16)
        acc_ref[slot] = jnp.zeros((tm, D_MODEL), F32)

    def tile_step(fs):
        if fs == 0:
            for r in range(tm):
                gather(nxt_ref, r, other).start()
                scatter(r, other, i == 0).start()
        xb = xb_ref[...]
        gate = jnp.dot(xb, wg_ref[0], preferred_element_type=F32)
        up = jnp.dot(xb, wu_ref[0], preferred_element_type=F32)
        hmid = (gate * jax.nn.sigmoid(gate) * up).astype(BF16)
        acc_ref[slot] += jnp.dot(hmid, wd_ref[0], preferred_element_type=F32)
        if fs == nf - 1:
            wait_scatter(other)

    for fs in range(nf):
        pl.when((i < n_used) & (f == fs))(functools.partial(tile_step, fs))

    @pl.when((i == n_used) & (f == 0))
    def _():
        wait_gather(slot)

        def issue(r, carry):
            scatter(r, other, False).start()
            return carry
        lax.fori_loop(0, tm, issue, 0, unroll=8)
        wait_scatter(other)


def _experts(x, src_of, dst_of, tile_expert, n_used, wg, wu, wd, tm, tf=1792):
    n_tok = x.shape[0]
    n_tiles = src_of.shape[0]
    nf = D_FF // tf
    fidx = lambda i, f, nu: jnp.where(i < nu[0], f, nf - 1)
    rows_of_tile = lambda shift: pl.BlockSpec(
        (1, 1, tm), lambda i, f, te, nu: (jnp.clip(i + shift, 0, n_tiles - 1), 0, 0),
        memory_space=pltpu.SMEM)
    grid_spec = pltpu.PrefetchScalarGridSpec(
        num_scalar_prefetch=2,
        grid=(n_tiles, nf),
        in_specs=[rows_of_tile(0), rows_of_tile(1), rows_of_tile(-1),
                  pl.BlockSpec(memory_space=pl.ANY),
                  pl.BlockSpec((1, D_MODEL, tf), lambda i, f, te, nu: (te[i], 0, fidx(i, f, nu))),
                  pl.BlockSpec((1, D_MODEL, tf), lambda i, f, te, nu: (te[i], 0, fidx(i, f, nu))),
                  pl.BlockSpec((1, tf, D_MODEL), lambda i, f, te, nu: (te[i], fidx(i, f, nu), 0))],
        out_specs=pl.BlockSpec(memory_space=pl.ANY),
        scratch_shapes=[pltpu.VMEM((2, tm, D_MODEL), F32), pltpu.VMEM((tm, D_MODEL), BF16),
                        pltpu.VMEM((2, tm, D_MODEL), F32),
                        pltpu.SemaphoreType.DMA((2,)), pltpu.SemaphoreType.DMA(())],
    )
    return pl.pallas_call(
        functools.partial(_expert_kernel, tm=tm, nf=nf, n_tok=n_tok),
        grid_spec=grid_spec,
        out_shape=jax.ShapeDtypeStruct((2 * n_tok + 2 * tm, D_MODEL), F32),
        compiler_params=_params("arbitrary", "arbitrary"),
        name="moe_experts",
    )(tile_expert, n_used, src_of, src_of, dst_of, x, wg, wu, wd)


def _combine_kernel(x_ref, y1_ref, y2_ref, gate_ref, g_ref, b_ref, o_ref):
    gates = gate_ref[...]
    y = gates[:, 0:1] * y1_ref[...] + gates[:, 1:2] * y2_ref[...]
    o_ref[...] = _layer_norm_rows(DN_ALPHA * x_ref[...] + y, g_ref[...], b_ref[...])


def _combine_ln(x, y, gates, g, b, tm=512):
    T = x.shape[0]
    nt = T // tm
    return pl.pallas_call(
        _combine_kernel,
        grid=(nt,),
        in_specs=[pl.BlockSpec((tm, D_MODEL), lambda i: (i, 0)),
                  pl.BlockSpec((tm, D_MODEL), lambda i: (i, 0)),
                  pl.BlockSpec((tm, D_MODEL), lambda i: (nt + i, 0)),
                  pl.BlockSpec((tm, ROUTER_LANES), lambda i: (i, 0)),
                  pl.BlockSpec((1, D_MODEL), lambda i: (0, 0)),
                  pl.BlockSpec((1, D_MODEL), lambda i: (0, 0))],
        out_specs=pl.BlockSpec((tm, D_MODEL), lambda i: (i, 0)),
        out_shape=jax.ShapeDtypeStruct((T, D_MODEL), F32),
        compiler_params=_params("parallel"),
        name="moe_combine_ln",
    )(x, y, y, gates, g.reshape(1, -1), b.reshape(1, -1))


def _moe_ln(x, router_w, wg, wu, wd, g, b, tm=512):
    router_pad = jnp.pad(router_w, ((0, 0), (0, ROUTER_LANES - N_EXPERTS)))
    meta, gates, cnt = _router(x, router_pad)
    src_of, dst_of, tile_expert, n_used = _dispatch_tables(meta, cnt, tm)
    y = _experts(x, src_of, dst_of, tile_expert, n_used, wg, wu, wd, tm)
    return _combine_ln(x, y, gates, g, b)


def _block_diag(w):
    nb, n, _ = w.shape
    eye = jnp.eye(nb, dtype=w.dtype)
    return (eye[:, None, :, None] * w[:, :, None, :]).reshape(nb * n, nb * n)


def kernel(x, w_in, w_out, hg_lb_logits, hg_norm_g, lru_conv_w, lru_conv_b, lru_wa, lru_ba, lru_wx, lru_bx, lru_lambda, da_lq1, da_lk1, da_lq2, da_lk2, da_norm_g, ln1_g, ln1_b, ln2_g, ln2_b, ffn_wg, ffn_wu, ffn_wd, router_w, moe_wg, moe_wu, moe_wd):
    batch, seq, _ = x.shape
    depth = w_in.shape[0]
    xf = x.reshape(batch * seq, D_MODEL)
    lb_p = jax.nn.softmax(hg_lb_logits.astype(F32), axis=0)
    lb_all = jnp.cumsum(lb_p, axis=0) - lb_p[0:1]
    for l in range(depth):
        rec, att = _in_proj(xf, w_in[l].astype(BF16))
        o_hg = _hgrn(rec, lb_all[l], hg_norm_g[l], batch, seq)
        o_lru = _lru(rec, lru_conv_w[l], lru_conv_b[l], _block_diag(lru_wa[l]).astype(BF16),
                     _block_diag(lru_wx[l]).astype(BF16), lru_ba[l], lru_bx[l], lru_lambda[l],
                     batch, seq)
        o_da = _attention(att, da_lq1[l], da_lk1[l], da_lq2[l], da_lk2[l], da_norm_g[l], l,
                          batch, seq)
        xf = _out_proj_ln(o_hg, o_lru, o_da, w_out[l].astype(BF16), xf, ln1_g[l], ln1_b[l])
        j = l // 2
        if l % 2 == 0:
            xf = _ffn_ln(xf, ffn_wg[j].astype(BF16), ffn_wu[j].astype(BF16), ffn_wd[j].astype(BF16),
                         ln2_g[l], ln2_b[l])
        else:
            xf = _moe_ln(xf, router_w[j], moe_wg[j].astype(BF16), moe_wu[j].astype(BF16),
                         moe_wd[j].astype(BF16), ln2_g[l], ln2_b[l])
    return xf.reshape(batch, seq, D_MODEL)
```

```python
import functools
import math

import jax
import jax.numpy as jnp
from jax import lax
from jax.experimental import pallas as pl
from jax.experimental.pallas import tpu as pltpu

D_MODEL = 1024
DEPTH = 2
HG_HEADS = 4
HG_KEY_DIM = 64
HG_WIDTH = 256
HG_CHUNK = 16
LRU_WIDTH = 256
LRU_BLOCKS = 4
CONV_WIDTH = 4
LRU_C = 8.0
DA_WIDTH = 512
DA_HEADS = 4
DA_HEAD_DIM = 64
D_FF = 3584
N_EXPERTS = 8
DN_ALPHA = (2.0 * DEPTH) ** 0.25
EPS = 1e-5
REC_WIDTH = 4 * HG_WIDTH + 2 * LRU_WIDTH
ATT_WIDTH = 3 * DA_WIDTH

V7X_VMEM_BYTES = 64 * 1024 * 1024
VMEM_LIMIT = 48 * 1024 * 1024

BF16 = jnp.bfloat16
F32 = jnp.float32


def _params(*semantics):
    return pltpu.CompilerParams(dimension_semantics=semantics, vmem_limit_bytes=VMEM_LIMIT)


def _layer_norm_rows(y, g, b):
    mu = jnp.mean(y, axis=-1, keepdims=True)
    yc = y - mu
    var = jnp.mean(yc * yc, axis=-1, keepdims=True)
    return yc * lax.rsqrt(var + EPS) * g + b


Q_SCALE = DA_HEAD_DIM ** -0.5 * math.log2(math.e)


def _in_proj_kernel(x_ref, w_ref, rec_ref, att_ref):
    xb = x_ref[...].astype(BF16)
    rec_ref[...] = jnp.dot(xb, w_ref[:, :REC_WIDTH], preferred_element_type=F32)
    q = jnp.dot(xb, w_ref[:, REC_WIDTH:REC_WIDTH + DA_WIDTH], preferred_element_type=F32)
    att_ref[:, :DA_WIDTH] = (q * Q_SCALE).astype(BF16)
    att_ref[:, DA_WIDTH:] = jnp.dot(xb, w_ref[:, REC_WIDTH + DA_WIDTH:],
                                    preferred_element_type=F32).astype(BF16)


def _in_proj(x, w_bf16, tm=512):
    T = x.shape[0]
    return pl.pallas_call(
        _in_proj_kernel,
        grid=(T // tm,),
        in_specs=[pl.BlockSpec((tm, D_MODEL), lambda i: (i, 0)),
                  pl.BlockSpec((D_MODEL, REC_WIDTH + ATT_WIDTH), lambda i: (0, 0))],
        out_specs=[pl.BlockSpec((tm, REC_WIDTH), lambda i: (i, 0)),
                   pl.BlockSpec((tm, ATT_WIDTH), lambda i: (i, 0))],
        out_shape=[jax.ShapeDtypeStruct((T, REC_WIDTH), F32),
                   jax.ShapeDtypeStruct((T, ATT_WIDTH), BF16)],
        compiler_params=_params("parallel"),
        name="in_proj",
    )(x, w_bf16)


def _hgrn_kernel(rec_ref, lb_ref, ng_ref, gmat_ref, sel_ref, bmask_ref, o_ref,
                 st_ref, q_s, kk_s, b_s, bl_s, qd_s, kd_s, o_s, *, rows):
    @pl.when(pl.program_id(1) == 0)
    def _():
        st_ref[...] = jnp.zeros_like(st_ref)

    C = HG_CHUNK
    W = HG_WIDTH
    nc = rows // C
    lb = lb_ref[...]
    gmat = gmat_ref[...]
    sel = sel_ref[...]

    qr = rec_ref[:, 0:W]
    z = rec_ref[:, W:2 * W]
    rin = lax.broadcasted_iota(jnp.int32, (rows, W), 0) % C
    logf = jnp.log(lb + (1.0 - lb) * jax.nn.sigmoid(z))
    kk = (1.0 - lb) * jax.nn.sigmoid(-z)
    q = qr * jax.nn.sigmoid(qr)
    b = logf
    for s in (1, 2, 4, 8):
        b = b + jnp.where(rin >= s, pltpu.roll(b, s, 0), 0.0)
    r = jnp.where(rin < C - 1, pltpu.roll(logf, rows - 1, 0), 0.0)
    for s in (1, 2, 4, 8):
        r = r + jnp.where(rin < C - s, pltpu.roll(r, rows - s, 0), 0.0)
    q_s[...] = q
    kk_s[...] = kk
    b_s[...] = b
    bl_s[...] = b + r
    qd_s[...] = (q * jnp.exp(b)).astype(BF16)
    kd_s[...] = (kk * jnp.exp(r)).astype(BF16)

    row = lax.broadcasted_iota(jnp.int32, (C, W), 0)

    def chunk(c, carry):
        r0 = pl.multiple_of(c * C, C)
        q = q_s[pl.ds(r0, C), :]
        kk = kk_s[pl.ds(r0, C), :]
        b = b_s[pl.ds(r0, C), :]
        v = rec_ref[pl.ds(r0, C), 2 * W:3 * W]
        w_rows = []
        for t in range(C):
            rel = b[t:t + 1, :] - b
            dec = jnp.exp(jnp.where(row <= t, rel, -jnp.inf))
            w_rows.append(((q[t:t + 1, :] * kk) * dec).astype(BF16))
        w2 = jnp.concatenate(w_rows, axis=0)
        a = jnp.dot(w2, gmat, preferred_element_type=F32)
        p = a * jnp.concatenate([v] * C, axis=0)
        o_intra = jnp.dot(sel, p.astype(BF16), preferred_element_type=F32)
        u_t = lax.dot_general(v.astype(BF16), kd_s[pl.ds(r0, C), :], (((0,), (0,)), ((), ())),
                              preferred_element_type=F32)
        st = st_ref[...]
        o_inter = lax.dot_general(qd_s[pl.ds(r0, C), :], st.astype(BF16), (((1,), (1,)), ((), ())),
                                  preferred_element_type=F32)
        o_s[pl.ds(r0, C), :] = o_intra + o_inter
        st_ref[...] = st * jnp.exp(bl_s[pl.ds(r0, 1), :]) + u_t * bmask_ref[...]
        return carry

    lax.fori_loop(0, nc, chunk, 0, unroll=8)

    o = o_s[...]
    g = rec_ref[:, 3 * W:4 * W]
    sq = o * o
    sq_hi = sq.astype(BF16)
    sq_lo = (sq - sq_hi.astype(F32)).astype(BF16)
    ms = (jnp.dot(sq_hi, gmat, preferred_element_type=F32)
          + jnp.dot(sq_lo, gmat, preferred_element_type=F32)) * (1.0 / HG_KEY_DIM)
    out = o * lax.rsqrt(ms + EPS) * ng_ref[...] * (g * jax.nn.sigmoid(g))
    o_ref[...] = out.astype(o_ref.dtype)


def _hgrn(rec, lb, norm_g, batch, seq, rows=256):
    T = rec.shape[0]
    nblk = seq // rows
    head = jnp.arange(HG_WIDTH) // HG_KEY_DIM
    same_head = head[:, None] == head[None, :]
    gmat = same_head.astype(BF16)
    bmask = same_head.astype(F32)
    sel = (jnp.arange(HG_CHUNK)[:, None] == (jnp.arange(HG_CHUNK * HG_CHUNK) // HG_CHUNK)[None, :]).astype(BF16)
    const = lambda shape: pl.BlockSpec(shape, lambda b, i: (0, 0))
    return pl.pallas_call(
        functools.partial(_hgrn_kernel, rows=rows),
        grid=(batch, nblk),
        in_specs=[pl.BlockSpec((rows, 4 * HG_WIDTH), lambda b, i: (b * nblk + i, 0)),
                  const((1, HG_WIDTH)), const((1, HG_WIDTH)),
                  const((HG_WIDTH, HG_WIDTH)), const((HG_CHUNK, HG_CHUNK * HG_CHUNK)),
                  const((HG_WIDTH, HG_WIDTH))],
        out_specs=pl.BlockSpec((rows, HG_WIDTH), lambda b, i: (b * nblk + i, 0)),
        out_shape=jax.ShapeDtypeStruct((T, HG_WIDTH), BF16),
        scratch_shapes=[pltpu.VMEM((HG_WIDTH, HG_WIDTH), F32)]
        + [pltpu.VMEM((rows, HG_WIDTH), F32)] * 4
        + [pltpu.VMEM((rows, HG_WIDTH), BF16)] * 2
        + [pltpu.VMEM((rows, HG_WIDTH), F32)],
        compiler_params=_params("parallel", "arbitrary"),
        name="hgrn2",
    )(rec, lb.reshape(1, -1), norm_g.reshape(1, -1), gmat, sel, bmask)


def _lru_kernel(rec_ref, cw_ref, cb_ref, wa_ref, wx_ref, ba_ref, bx_ref, lam_ref, o_ref,
                xprev_ref, h_ref, *, rows):
    @pl.when(pl.program_id(1) == 0)
    def _():
        xprev_ref[...] = jnp.zeros_like(xprev_ref)
        h_ref[...] = jnp.zeros_like(h_ref)

    W = LRU_WIDTH
    x = rec_ref[:, 0:W]
    gate = rec_ref[:, W:2 * W]
    row = lax.broadcasted_iota(jnp.int32, (rows, W), 0)
    xp = xprev_ref[...]
    tail = jnp.zeros((rows - 8, W), F32)
    xc = cb_ref[...] + cw_ref[CONV_WIDTH - 1:CONV_WIDTH, :] * x
    for j in range(1, CONV_WIDTH):
        prev = jnp.concatenate([pltpu.roll(xp, j, 0), tail], axis=0)
        xs = jnp.where(row >= j, pltpu.roll(x, j, 0), prev)
        xc = xc + cw_ref[CONV_WIDTH - 1 - j:CONV_WIDTH - j, :] * xs
    xprev_ref[...] = x[rows - 8:rows, :]

    xcb = xc.astype(BF16)
    r = jax.nn.sigmoid(jnp.dot(xcb, wa_ref[...], preferred_element_type=F32) + ba_ref[...])
    i = jax.nn.sigmoid(jnp.dot(xcb, wx_ref[...], preferred_element_type=F32) + bx_ref[...])
    lam = lam_ref[...]
    log_sig = jnp.minimum(lam, 0.0) - jnp.log1p(jnp.exp(-jnp.abs(lam)))
    log_a = LRU_C * r * log_sig
    a = jnp.exp(log_a)
    th = jnp.tanh(log_a)
    one_minus_a2 = -2.0 * th / (1.0 - th)
    u = jnp.sqrt(one_minus_a2) * i * xc

    for s in (1, 2, 4):
        a_s = jnp.where(row >= s, pltpu.roll(a, s, 0), 1.0)
        u_s = jnp.where(row >= s, pltpu.roll(u, s, 0), 0.0)
        u = a * u_s + u
        a = a * a_s
    s = 8
    while s < rows:
        a_s = jnp.concatenate([jnp.ones((s, W), F32), a[:rows - s, :]], axis=0)
        u_s = jnp.concatenate([jnp.zeros((s, W), F32), u[:rows - s, :]], axis=0)
        u = a * u_s + u
        a = a * a_s
        s *= 2
    h = a * h_ref[0:1, :] + u
    h_ref[...] = jnp.broadcast_to(h[rows - 1:rows, :], h_ref.shape)
    o_ref[...] = (h * jax.nn.gelu(gate)).astype(o_ref.dtype)


def _lru(rec, conv_w, conv_b, wa_bd, wx_bd, ba, bx, lam, batch, seq, rows=256):
    T = rec.shape[0]
    nblk = seq // rows
    W = LRU_WIDTH
    const = lambda shape: pl.BlockSpec(shape, lambda b, i: (0, 0))
    return pl.pallas_call(
        functools.partial(_lru_kernel, rows=rows),
        grid=(batch, nblk),
        in_specs=[pl.BlockSpec((rows, 2 * W), lambda b, i: (b * nblk + i, 2)),
                  const((CONV_WIDTH, W)), const((1, W)), const((W, W)), const((W, W)),
                  const((1, W)), const((1, W)), const((1, W))],
        out_specs=pl.BlockSpec((rows, W), lambda b, i: (b * nblk + i, 0)),
        out_shape=jax.ShapeDtypeStruct((T, W), BF16),
        scratch_shapes=[pltpu.VMEM((8, W), F32), pltpu.VMEM((8, W), F32)],
        compiler_params=_params("parallel", "arbitrary"),
        name="rglru",
    )(rec, conv_w, conv_b.reshape(1, -1), wa_bd, wx_bd, ba.reshape(1, -1), bx.reshape(1, -1),
      lam.reshape(1, -1))


def _attn_kernel(q_ref, k_ref, v_ref, lq1_ref, lk1_ref, lq2_ref, lk2_ref, ng_ref, o_ref,
                 vt_ref, qm_ref, s0_ref, s1_ref, m_ref, l_ref, acc_ref, *, tq, tk, qs, lam_init, seq):
    d = DA_HEAD_DIM
    qi = pl.program_id(2)

    @pl.when(qi == 0)
    def _():
        def transpose_values(j, carry):
            r0 = pl.multiple_of(j * tk, tk)
            vt_ref[j] = v_ref[pl.ds(r0, tk), :].astype(F32).T.astype(BF16)
            return carry
        lax.fori_loop(0, seq // tk, transpose_values, 0)

    q = q_ref[...]
    lane = lax.broadcasted_iota(jnp.int32, q.shape, 1)
    qm_ref[0] = jnp.where(lane < d, q, jnp.zeros_like(q))
    qm_ref[1] = jnp.where(lane >= d, q, jnp.zeros_like(q))
    m_ref[...] = jnp.full_like(m_ref, -jnp.inf)
    l_ref[...] = jnp.zeros_like(l_ref)
    acc_ref[...] = jnp.zeros_like(acc_ref)
    s_refs = (s0_ref, s1_ref)

    def scores(j, c):
        k0 = pl.multiple_of(j * tk, tk)
        kb = k_ref[pl.ds(k0, tk), :]
        for st in range(tq // qs):
            s_refs[c][st] = lax.dot_general(kb, qm_ref[c, st * qs:(st + 1) * qs, :],
                                            (((1,), (1,)), ((), ())),
                                            preferred_element_type=F32)

    def softmax_pv(j, c, masked):
        for st in range(tq // qs):
            cols = pl.ds(st * qs, qs)
            nk = min((st + 1) * qs, tk) if masked else tk
            vt = vt_ref[j, :, 0:nk]
            s = s_refs[c][st, 0:nk, :]
            if masked:
                kpos = lax.broadcasted_iota(jnp.int32, (nk, qs), 0)
                qpos = st * qs + lax.broadcasted_iota(jnp.int32, (nk, qs), 1)
                s = jnp.where(kpos <= qpos, s, -jnp.inf)
            m_old = m_ref[c, :, cols]
            m_new = jnp.maximum(m_old, jnp.max(s, axis=0, keepdims=True))
            p = jnp.exp2(s - m_new)
            alpha = jnp.exp2(m_old - m_new)
            l_ref[c, :, cols] = alpha * l_ref[c, :, cols] + jnp.sum(p, axis=0, keepdims=True)
            acc_ref[c, st] = (alpha * acc_ref[c, st]
                                   + jnp.dot(vt, p.astype(BF16), preferred_element_type=F32))
            m_ref[c, :, cols] = m_new

    def full_block(j, carry):
        scores(j, 1)
        softmax_pv(j, 0, False)
        scores(j + 1, 0)
        softmax_pv(j, 1, False)
        return carry

    def four_full_blocks(jj, carry):
        for u in range(4):
            full_block(4 * jj + u, carry)
        return carry

    def remaining_full_block(j, carry):
        return full_block(j, carry)

    scores(0, 0)
    lax.fori_loop(0, qi // 4, four_full_blocks, 0)
    lax.fori_loop((qi // 4) * 4, qi, remaining_full_block, 0)

    scores(qi, 1)
    softmax_pv(qi, 0, True)
    softmax_pv(qi, 1, True)

    lam =(jnp.exp(jnp.sum(lq1_ref[...] * lk1_ref[...], axis=-1, keepdims=True))
           - jnp.exp(jnp.sum(lq2_ref[...] * lk2_ref[...], axis=-1, keepdims=True)) + lam_init)
    acc = [jnp.concatenate([acc_ref[c, st] for st in range(tq // qs)], axis=1) for c in range(2)]
    o_t = acc[0] * (1.0 / l_ref[0]) - lam * (acc[1] * (1.0 / l_ref[1]))
    ms = jnp.mean(o_t * o_t, axis=0, keepdims=True)
    o_t = o_t * lax.rsqrt(ms + EPS) * ng_ref[...] * (1.0 - lam_init)
    o_ref[...] = o_t.T.astype(o_ref.dtype)


def _attention(att, lq1, lk1, lq2, lk2, norm_g, layer, batch, seq, tq=512, qs=256):
    T = att.shape[0]
    tk = tq
    nq = seq // tq
    hw = 2 * DA_HEAD_DIM
    lam_init = 0.8 - 0.6 * math.exp(-0.3 * layer)
    vec = lambda: pl.BlockSpec((1, DA_HEAD_DIM), lambda b, h, i: (0, 0))
    return pl.pallas_call(
        functools.partial(_attn_kernel, tq=tq, tk=tk, qs=qs, lam_init=lam_init, seq=seq),
        grid=(batch, DA_HEADS, nq),
        in_specs=[pl.BlockSpec((tq, hw), lambda b, h, i: (b * nq + i, h)),
                  pl.BlockSpec((seq, hw), lambda b, h, i: (b, DA_HEADS + h)),
                  pl.BlockSpec((seq, hw), lambda b, h, i: (b, 2 * DA_HEADS + h)),
                  vec(), vec(), vec(), vec(),
                  pl.BlockSpec((hw, 1), lambda b, h, i: (h, 0))],
        out_specs=pl.BlockSpec((tq, hw), lambda b, h, i: (b * nq + i, h)),
        out_shape=jax.ShapeDtypeStruct((T, DA_WIDTH), BF16),
        scratch_shapes=[pltpu.VMEM((seq // tk, hw, tk), BF16), pltpu.VMEM((2, tq, hw), BF16),
                        pltpu.VMEM((tq // qs, tk, qs), F32), pltpu.VMEM((tq // qs, tk, qs), F32),
                        pltpu.VMEM((2, 1, tq), F32), pltpu.VMEM((2, 1, tq), F32),
                        pltpu.VMEM((2, tq // qs, hw, qs), F32)],
        compiler_params=_params("arbitrary", "arbitrary", "arbitrary"),
        name="diff_attn",
    )(att, att, att, lq1.reshape(1, -1), lk1.reshape(1, -1), lq2.reshape(1, -1),
      lk2.reshape(1, -1), norm_g.reshape(-1, 1))


def _out_proj_kernel(hg_ref, lru_ref, da_ref, w_ref, x_ref, g_ref, b_ref, o_ref):
    h = jnp.dot(hg_ref[...], w_ref[0:HG_WIDTH, :], preferred_element_type=F32)
    h = h + jnp.dot(lru_ref[...], w_ref[HG_WIDTH:HG_WIDTH + LRU_WIDTH, :], preferred_element_type=F32)
    h = h + jnp.dot(da_ref[...], w_ref[HG_WIDTH + LRU_WIDTH:, :], preferred_element_type=F32)
    o_ref[...] = _layer_norm_rows(DN_ALPHA * x_ref[...] + h, g_ref[...], b_ref[...])


def _out_proj_ln(o_hg, o_lru, o_da, w_bf16, x, g, b, tm=512):
    T = x.shape[0]
    rows = lambda w: pl.BlockSpec((tm, w), lambda i: (i, 0))
    const = lambda shape: pl.BlockSpec(shape, lambda i: (0, 0))
    return pl.pallas_call(
        _out_proj_kernel,
        grid=(T // tm,),
        in_specs=[rows(HG_WIDTH), rows(LRU_WIDTH), rows(DA_WIDTH), const((D_MODEL, D_MODEL)),
                  rows(D_MODEL), const((1, D_MODEL)), const((1, D_MODEL))],
        out_specs=rows(D_MODEL),
        out_shape=jax.ShapeDtypeStruct((T, D_MODEL), F32),
        compiler_params=_params("parallel"),
        name="out_proj_ln",
    )(o_hg, o_lru, o_da, w_bf16, x, g.reshape(1, -1), b.reshape(1, -1))


def _ffn_kernel(x_ref, wg_ref, wu_ref, wd_ref, g_ref, b_ref, o_ref, xb_ref, acc_ref):
    f = pl.program_id(1)

    @pl.when(f == 0)
    def _():
        xb_ref[...] = x_ref[...].astype(BF16)
        acc_ref[...] = jnp.zeros_like(acc_ref)

    xb = xb_ref[...]
    gate = jnp.dot(xb, wg_ref[...], preferred_element_type=F32)
    up = jnp.dot(xb, wu_ref[...], preferred_element_type=F32)
    hmid = (gate * jax.nn.sigmoid(gate) * up).astype(BF16)
    acc_ref[...] += jnp.dot(hmid, wd_ref[...], preferred_element_type=F32)

    @pl.when(f == pl.num_programs(1) - 1)
    def _():
        o_ref[...] = _layer_norm_rows(DN_ALPHA * x_ref[...] + acc_ref[...], g_ref[...], b_ref[...])


def _ffn_ln(x, wg, wu, wd, g, b, tm=1024, tf=512):
    T = x.shape[0]
    return pl.pallas_call(
        _ffn_kernel,
        grid=(T // tm, D_FF // tf),
        in_specs=[pl.BlockSpec((tm, D_MODEL), lambda i, f: (i, 0)),
                  pl.BlockSpec((D_MODEL, tf), lambda i, f: (0, f)),
                  pl.BlockSpec((D_MODEL, tf), lambda i, f: (0, f)),
                  pl.BlockSpec((tf, D_MODEL), lambda i, f: (f, 0)),
                  pl.BlockSpec((1, D_MODEL), lambda i, f: (0, 0)),
                  pl.BlockSpec((1, D_MODEL), lambda i, f: (0, 0))],
        out_specs=pl.BlockSpec((tm, D_MODEL), lambda i, f: (i, 0)),
        out_shape=jax.ShapeDtypeStruct((T, D_MODEL), F32),
        scratch_shapes=[pltpu.VMEM((tm, D_MODEL), BF16), pltpu.VMEM((tm, D_MODEL), F32)],
        compiler_params=_params("parallel", "arbitrary"),
        name="ffn_ln",
    )(x, wg, wu, wd, g.reshape(1, -1), b.reshape(1, -1))


ROUTER_LANES = 128
META_I1, META_I2, META_R1, META_R2 = 0, 1, 2, 3


def _router_kernel(x_ref, rwh_ref, rwl_ref, tri_ref, meta_ref, gate_ref, cnt_ref, carry_ref):
    @pl.when(pl.program_id(0) == 0)
    def _():
        carry_ref[...] = jnp.zeros_like(carry_ref)

    x = x_ref[...]
    x_hi = x.astype(BF16)
    x_lo = (x - x_hi.astype(F32)).astype(BF16)
    logits = (jnp.dot(x_hi, rwh_ref[...], preferred_element_type=F32)
              + jnp.dot(x_lo, rwh_ref[...], preferred_element_type=F32)
              + jnp.dot(x_hi, rwl_ref[...], preferred_element_type=F32))
    lane = lax.broadcasted_iota(jnp.int32, logits.shape, 1)
    neg = -jnp.inf
    l1 = jnp.where(lane < N_EXPERTS, logits, neg)
    v1 = jnp.max(l1, axis=-1, keepdims=True)
    i1 = jnp.min(jnp.where(l1 == v1, lane, ROUTER_LANES), axis=-1, keepdims=True)
    l2 = jnp.where(lane == i1, neg, l1)
    v2 = jnp.max(l2, axis=-1, keepdims=True)
    i2 = jnp.min(jnp.where(l2 == v2, lane, ROUTER_LANES), axis=-1, keepdims=True)
    e2 = jnp.exp(v2 - v1)
    g1 = 1.0 / (1.0 + e2)
    g2 = e2 / (1.0 + e2)
    member = jnp.where((lane == i1) | (lane == i2), 1.0, 0.0)
    rank = jnp.dot(tri_ref[...], member.astype(BF16), preferred_element_type=F32) + carry_ref[0:1, :]
    r1 = jnp.sum(jnp.where(lane == i1, rank, 0.0), axis=-1, keepdims=True).astype(jnp.int32)
    r2 = jnp.sum(jnp.where(lane == i2, rank, 0.0), axis=-1, keepdims=True).astype(jnp.int32)
    meta = jnp.where(lane == META_I1, i1, 0) + jnp.where(lane == META_I2, i2, 0)
    meta = meta + jnp.where(lane == META_R1, r1, 0) + jnp.where(lane == META_R2, r2, 0)
    meta_ref[...] = meta
    gate_ref[...] = jnp.where(lane == 0, g1, 0.0) + jnp.where(lane == 1, g2, 0.0)
    carry = carry_ref[...] + jnp.sum(member, axis=0, keepdims=True)
    carry_ref[...] = carry
    cnt_ref[...] = carry


def _router(x, router_pad, tm=1024):
    T = x.shape[0]
    tri = (jnp.arange(tm)[:, None] > jnp.arange(tm)[None, :]).astype(BF16)
    rw_hi = router_pad.astype(BF16)
    rw_lo = (router_pad - rw_hi.astype(F32)).astype(BF16)
    return pl.pallas_call(
        _router_kernel,
        grid=(T // tm,),
        in_specs=[pl.BlockSpec((tm, D_MODEL), lambda i: (i, 0)),
                  pl.BlockSpec((D_MODEL, ROUTER_LANES), lambda i: (0, 0)),
                  pl.BlockSpec((D_MODEL, ROUTER_LANES), lambda i: (0, 0)),
                  pl.BlockSpec((tm, tm), lambda i: (0, 0))],
        out_specs=[pl.BlockSpec((tm, ROUTER_LANES), lambda i: (i, 0)),
                   pl.BlockSpec((tm, ROUTER_LANES), lambda i: (i, 0)),
                   pl.BlockSpec((8, ROUTER_LANES), lambda i: (0, 0))],
        out_shape=[jax.ShapeDtypeStruct((T, ROUTER_LANES), jnp.int32),
                   jax.ShapeDtypeStruct((T, ROUTER_LANES), F32),
                   jax.ShapeDtypeStruct((8, ROUTER_LANES), F32)],
        scratch_shapes=[pltpu.VMEM((8, ROUTER_LANES), F32)],
        compiler_params=_params("arbitrary"),
        name="moe_router",
    )(x, rw_hi, rw_lo, tri)


def _dispatch_tables(meta, cnt, tm):
    T = meta.shape[0]
    counts = cnt[0, :N_EXPERTS].astype(jnp.int32)
    padded = ((counts + tm - 1) // tm) * tm
    ends = jnp.cumsum(padded)
    off = ends - padded
    pos1 = off[meta[:, META_I1]] + meta[:, META_R1]
    pos2 = off[meta[:, META_I2]] + meta[:, META_R2]
    n_tiles = 2 * T // tm + N_EXPERTS + 1
    tok = jnp.arange(T, dtype=jnp.int32)
    code_of = jnp.full((n_tiles * tm,), 2 * T, jnp.int32).at[jnp.concatenate([pos1, pos2])].set(
        jnp.concatenate([tok, tok + T]))
    tile_start = jnp.arange(n_tiles, dtype=jnp.int32) * tm
    tile_expert = jnp.minimum(jnp.sum((tile_start[:, None] >= ends[None, :]).astype(jnp.int32), axis=1),
                              N_EXPERTS - 1)
    n_used = (ends[-1] // tm).astype(jnp.int32).reshape(1)
    row = jnp.arange(n_tiles * tm, dtype=jnp.int32)
    is_pad = code_of >= 2 * T
    src_of = jnp.where(is_pad, 0, jnp.where(code_of >= T, code_of - T, code_of))
    dst_of = jnp.where(is_pad, 2 * T + ((row // tm) % 2) * tm + row % tm, code_of)
    return src_of.reshape(n_tiles, 1, tm), dst_of.reshape(n_tiles, 1, tm), tile_expert, n_used


def _expert_kernel(te_ref, nu_ref, cur_ref, nxt_ref, prv_ref, x_hbm, wg_ref, wu_ref, wd_ref, y_hbm,
                   xbuf_ref, xb_ref, acc_ref, sem_g, sem_s, *, tm, nf, n_tok):
    i = pl.program_id(0)
    f = pl.program_id(1)
    n_used = nu_ref[0]
    slot = i % 2
    other = 1 - slot
    pad_code = 2 * n_tok
    rows_per_step = tm // nf

    def gather(src_ref, r, buf):
        return pltpu.make_async_copy(x_hbm.at[pl.ds(src_ref[0, 0, r], 1)],
                                     xbuf_ref.at[buf, pl.ds(r, 1)], sem_g.at[buf])

    def scatter(r, buf, to_dump):
        dest = jnp.where(to_dump, pad_code + buf * tm + r, prv_ref[0, 0, r])
        return pltpu.make_async_copy(acc_ref.at[buf, pl.ds(r, 1)], y_hbm.at[pl.ds(dest, 1)], sem_s)

    def wait_gather(buf):
        pltpu.make_async_copy(x_hbm.at[pl.ds(0, tm)], xbuf_ref.at[buf], sem_g.at[buf]).wait()

    def wait_scatter(buf):
        pltpu.make_async_copy(acc_ref.at[buf], y_hbm.at[pl.ds(0, tm)], sem_s).wait()

    @pl.when((i < n_used) & (f == 0))
    def _():
        @pl.when(i == 0)
        def _():
            def issue(r, carry):
                gather(cur_ref, r, 0).start()
                return carry
            lax.fori_loop(0, tm, issue, 0, unroll=8)
            acc_ref[1] = jnp.zeros((tm, D_MODEL), F32)
            clear = pltpu.make_async_copy(acc_ref.at[1], y_hbm.at[pl.ds(pad_code, tm)], sem_s)
            clear.start()
            clear.wait()

        wait_gather(slot)
        xb_ref[...] = xbuf_ref[slot].astype(BF16)
        acc_ref[slot] = jnp.zeros((tm, D_MODEL), F32)

    def tile_step(fs):
        if fs == 0:
            for r in range(tm):
                gather(nxt_ref, r, other).start()
        if fs == nf - 1:
            for r in range(tm):
                scatter(r, other, i == 0).start()
        xb = xb_ref[...]
        gate = jnp.dot(xb, wg_ref[0], preferred_element_type=F32)
        up = jnp.dot(xb, wu_ref[0], preferred_element_type=F32)
        hmid = (gate * jax.nn.sigmoid(gate) * up).astype(BF16)
        acc_ref[slot] += jnp.dot(hmid, wd_ref[0], preferred_element_type=F32)
        if fs == nf - 1:
            wait_scatter(other)

    for fs in range(nf):
        pl.when((i < n_used) & (f == fs))(functools.partial(tile_step, fs))

    @pl.when((i == n_used) & (f == 0))
    def _():
        wait_gather(slot)

        def issue(r, carry):
            scatter(r, other, False).start()
            return carry
        lax.fori_loop(0, tm, issue, 0, unroll=8)
        wait_scatter(other)


def _experts(x, src_of, dst_of, tile_expert, n_used, wg, wu, wd, tm, tf=1792):
    n_tok = x.shape[0]
    n_tiles = src_of.shape[0]
    nf = D_FF // tf
    fidx = lambda i, f, nu: jnp.where(i < nu[0], f, nf - 1)
    rows_of_tile = lambda shift: pl.BlockSpec(
        (1, 1, tm), lambda i, f, te, nu: (jnp.clip(i + shift, 0, n_tiles - 1), 0, 0),
        memory_space=pltpu.SMEM)
    grid_spec = pltpu.PrefetchScalarGridSpec(
        num_scalar_prefetch=2,
        grid=(n_tiles, nf),
        in_specs=[rows_of_tile(0), rows_of_tile(1), rows_of_tile(-1),
                  pl.BlockSpec(memory_space=pl.ANY),
                  pl.BlockSpec((1, D_MODEL, tf), lambda i, f, te, nu: (te[i], 0, fidx(i, f, nu))),
                  pl.BlockSpec((1, D_MODEL, tf), lambda i, f, te, nu: (te[i], 0, fidx(i, f, nu))),
                  pl.BlockSpec((1, tf, D_MODEL), lambda i, f, te, nu: (te[i], fidx(i, f, nu), 0))],
        out_specs=pl.BlockSpec(memory_space=pl.ANY),
        scratch_shapes=[pltpu.VMEM((2, tm, D_MODEL), F32), pltpu.VMEM((tm, D_MODEL), BF16),
                        pltpu.VMEM((2, tm, D_MODEL), F32),
                        pltpu.SemaphoreType.DMA((2,)), pltpu.SemaphoreType.DMA(())],
    )
    return pl.pallas_call(
        functools.partial(_expert_kernel, tm=tm, nf=nf, n_tok=n_tok),
        grid_spec=grid_spec,
        out_shape=jax.ShapeDtypeStruct((2 * n_tok + 2 * tm, D_MODEL), F32),
        compiler_params=_params("arbitrary", "arbitrary"),
        name="moe_experts",
    )(tile_expert, n_used, src_of, src_of, dst_of, x, wg, wu, wd)


def _combine_kernel(x_ref, y1_ref, y2_ref, gate_ref, g_ref, b_ref, o_ref):
    gates = gate_ref[...]
    y = gates[:, 0:1] * y1_ref[...] + gates[:, 1:2] * y2_ref[...]
    o_ref[...] = _layer_norm_rows(DN_ALPHA * x_ref[...] + y, g_ref[...], b_ref[...])


def _combine_ln(x, y, gates, g, b, tm=512):
    T = x.shape[0]
    nt = T // tm
    return pl.pallas_call(
        _combine_kernel,
        grid=(nt,),
        in_specs=[pl.BlockSpec((tm, D_MODEL), lambda i: (i, 0)),
                  pl.BlockSpec((tm, D_MODEL), lambda i: (i, 0)),
                  pl.BlockSpec((tm, D_MODEL), lambda i: (nt + i, 0)),
                  pl.BlockSpec((tm, ROUTER_LANES), lambda i: (i, 0)),
                  pl.BlockSpec((1, D_MODEL), lambda i: (0, 0)),
                  pl.BlockSpec((1, D_MODEL), lambda i: (0, 0))],
        out_specs=pl.BlockSpec((tm, D_MODEL), lambda i: (i, 0)),
        out_shape=jax.ShapeDtypeStruct((T, D_MODEL), F32),
        compiler_params=_params("parallel"),
        name="moe_combine_ln",
    )(x, y, y, gates, g.reshape(1, -1), b.reshape(1, -1))


def _moe_ln(x, router_w, wg, wu, wd, g, b, tm=512):
    router_pad = jnp.pad(router_w, ((0, 0), (0, ROUTER_LANES - N_EXPERTS)))
    meta, gates, cnt = _router(x, router_pad)
    src_of, dst_of, tile_expert, n_used = _dispatch_tables(meta, cnt, tm)
    y = _experts(x, src_of, dst_of, tile_expert, n_used, wg, wu, wd, tm)
    return _combine_ln(x, y, gates, g, b)


def _block_diag(w):
    nb, n, _ = w.shape
    eye = jnp.eye(nb, dtype=w.dtype)
    return (eye[:, None, :, None] * w[:, :, None, :]).reshape(nb * n, nb * n)


def kernel(x, w_in, w_out, hg_lb_logits, hg_norm_g, lru_conv_w, lru_conv_b, lru_wa, lru_ba, lru_wx, lru_bx, lru_lambda, da_lq1, da_lk1, da_lq2, da_lk2, da_norm_g, ln1_g, ln1_b, ln2_g, ln2_b, ffn_wg, ffn_wu, ffn_wd, router_w, moe_wg, moe_wu, moe_wd):
    batch, seq, _ = x.shape
    depth = w_in.shape[0]
    xf = x.reshape(batch * seq, D_MODEL)
    lb_p = jax.nn.softmax(hg_lb_logits.astype(F32), axis=0)
    lb_all = jnp.cumsum(lb_p, axis=0) - lb_p[0:1]
    for l in range(depth):
        rec, att = _in_proj(xf, w_in[l].astype(BF16))
        o_hg = _hgrn(rec, lb_all[l], hg_norm_g[l], batch, seq)
        o_lru = _lru(rec, lru_conv_w[l], lru_conv_b[l], _block_diag(lru_wa[l]).astype(BF16),
                     _block_diag(lru_wx[l]).astype(BF16), lru_ba[l], lru_bx[l], lru_lambda[l],
                     batch, seq)
        o_da = _attention(att, da_lq1[l], da_lk1[l], da_lq2[l], da_lk2[l], da_norm_g[l], l,
                          batch, seq)
        xf = _out_proj_ln(o_hg, o_lru, o_da, w_out[l].astype(BF16), xf, ln1_g[l], ln1_b[l])
        j = l // 2
        if l % 2 == 0:
            xf = _ffn_ln(xf, ffn_wg[j].astype(BF16), ffn_wu[j].astype(BF16), ffn_wd[j].astype(BF16),
                         ln2_g[l], ln2_b[l])
        else:
            xf = _moe_ln(xf, router_w[j], moe_wg[j].astype(BF16), moe_wu[j].astype(BF16),
                         moe_wd[j].astype(BF16), ln2_g[l], ln2_b[l])
    return xf.reshape(batch, seq, D_MODEL)
```

```python
import functools
import math

import jax
import jax.numpy as jnp
from jax import lax
from jax.experimental import pallas as pl
from jax.experimental.pallas import tpu as pltpu

D_MODEL = 1024
DEPTH = 2
HG_HEADS = 4
HG_KEY_DIM = 64
HG_WIDTH = 256
HG_CHUNK = 16
LRU_WIDTH = 256
LRU_BLOCKS = 4
CONV_WIDTH = 4
LRU_C = 8.0
DA_WIDTH = 512
DA_HEADS = 4
DA_HEAD_DIM = 64
D_FF = 3584
N_EXPERTS = 8
DN_ALPHA = (2.0 * DEPTH) ** 0.25
EPS = 1e-5
REC_WIDTH = 4 * HG_WIDTH + 2 * LRU_WIDTH
ATT_WIDTH = 3 * DA_WIDTH

V7X_VMEM_BYTES = 64 * 1024 * 1024
VMEM_LIMIT = V7X_VMEM_BYTES * 3 // 4

BF16 = jnp.bfloat16
F32 = jnp.float32


def _params(*semantics):
    return pltpu.CompilerParams(dimension_semantics=semantics, vmem_limit_bytes=VMEM_LIMIT)


def _layer_norm_rows(y, g, b):
    mu = jnp.mean(y, axis=-1, keepdims=True)
    yc = y - mu
    var = jnp.mean(yc * yc, axis=-1, keepdims=True)
    return yc * lax.rsqrt(var + EPS) * g + b


Q_SCALE = DA_HEAD_DIM ** -0.5 * math.log2(math.e)


def _in_proj_kernel(x_ref, w_ref, rec_ref, att_ref):
    xb = x_ref[...].astype(BF16)
    rec_ref[...] = jnp.dot(xb, w_ref[:, :REC_WIDTH], preferred_element_type=F32)
    q = jnp.dot(xb, w_ref[:, REC_WIDTH:REC_WIDTH + DA_WIDTH], preferred_element_type=F32)
    att_ref[:, :DA_WIDTH] = (q * Q_SCALE).astype(BF16)
    att_ref[:, DA_WIDTH:] = jnp.dot(xb, w_ref[:, REC_WIDTH + DA_WIDTH:],
                                    preferred_element_type=F32).astype(BF16)


def _in_proj(x, w_bf16, tm=512):
    T = x.shape[0]
    return pl.pallas_call(
        _in_proj_kernel,
        grid=(T // tm,),
        in_specs=[pl.BlockSpec((tm, D_MODEL), lambda i: (i, 0)),
                  pl.BlockSpec((D_MODEL, REC_WIDTH + ATT_WIDTH), lambda i: (0, 0))],
        out_specs=[pl.BlockSpec((tm, REC_WIDTH), lambda i: (i, 0)),
                   pl.BlockSpec((tm, ATT_WIDTH), lambda i: (i, 0))],
        out_shape=[jax.ShapeDtypeStruct((T, REC_WIDTH), F32),
                   jax.ShapeDtypeStruct((T, ATT_WIDTH), BF16)],
        compiler_params=_params("parallel"),
        name="in_proj",
    )(x, w_bf16)


def _hgrn_kernel(rec_ref, lb_ref, ng_ref, gmat_ref, sel_ref, bmask_ref, o_ref,
                 st_ref, q_s, kk_s, b_s, bl_s, qd_s, kd_s, o_s, *, rows):
    @pl.when(pl.program_id(1) == 0)
    def _():
        st_ref[...] = jnp.zeros_like(st_ref)

    C = HG_CHUNK
    W = HG_WIDTH
    nc = rows // C
    lb = lb_ref[...]
    gmat = gmat_ref[...]
    sel = sel_ref[...]

    qr = rec_ref[:, 0:W]
    z = rec_ref[:, W:2 * W]
    rin = lax.broadcasted_iota(jnp.int32, (rows, W), 0) % C
    logf = jnp.log(lb + (1.0 - lb) * jax.nn.sigmoid(z))
    kk = (1.0 - lb) * jax.nn.sigmoid(-z)
    q = qr * jax.nn.sigmoid(qr)
    b = logf
    for s in (1, 2, 4, 8):
        b = b + jnp.where(rin >= s, pltpu.roll(b, s, 0), 0.0)
    r = jnp.where(rin < C - 1, pltpu.roll(logf, rows - 1, 0), 0.0)
    for s in (1, 2, 4, 8):
        r = r + jnp.where(rin < C - s, pltpu.roll(r, rows - s, 0), 0.0)
    q_s[...] = q
    kk_s[...] = kk
    b_s[...] = b
    bl_s[...] = b + r
    qd_s[...] = (q * jnp.exp(b)).astype(BF16)
    kd_s[...] = (kk * jnp.exp(r)).astype(BF16)

    row = lax.broadcasted_iota(jnp.int32, (C, W), 0)

    def chunk(c, carry):
        r0 = pl.multiple_of(c * C, C)
        q = q_s[pl.ds(r0, C), :]
        kk = kk_s[pl.ds(r0, C), :]
        b = b_s[pl.ds(r0, C), :]
        v = rec_ref[pl.ds(r0, C), 2 * W:3 * W]
        w_rows = []
        for t in range(C):
            rel = b[t:t + 1, :] - b
            dec = jnp.exp(jnp.where(row <= t, rel, -jnp.inf))
            w_rows.append(((q[t:t + 1, :] * kk) * dec).astype(BF16))
        w2 = jnp.concatenate(w_rows, axis=0)
        a = jnp.dot(w2, gmat, preferred_element_type=F32)
        p = a * jnp.concatenate([v] * C, axis=0)
        o_intra = jnp.dot(sel, p.astype(BF16), preferred_element_type=F32)
        u_t = lax.dot_general(v.astype(BF16), kd_s[pl.ds(r0, C), :], (((0,), (0,)), ((), ())),
                              preferred_element_type=F32)
        st = st_ref[...]
        o_inter = lax.dot_general(qd_s[pl.ds(r0, C), :], st.astype(BF16), (((1,), (1,)), ((), ())),
                                  preferred_element_type=F32)
        o_s[pl.ds(r0, C), :] = o_intra + o_inter
        st_ref[...] = st * jnp.exp(bl_s[pl.ds(r0, 1), :]) + u_t * bmask_ref[...]
        return carry

    lax.fori_loop(0, nc, chunk, 0, unroll=8)

    o = o_s[...]
    g = rec_ref[:, 3 * W:4 * W]
    sq = o * o
    sq_hi = sq.astype(BF16)
    sq_lo = (sq - sq_hi.astype(F32)).astype(BF16)
    ms = (jnp.dot(sq_hi, gmat, preferred_element_type=F32)
          + jnp.dot(sq_lo, gmat, preferred_element_type=F32)) * (1.0 / HG_KEY_DIM)
    out = o * lax.rsqrt(ms + EPS) * ng_ref[...] * (g * jax.nn.sigmoid(g))
    o_ref[...] = out.astype(o_ref.dtype)


def _hgrn(rec, lb, norm_g, batch, seq, rows=512):
    T = rec.shape[0]
    nblk = seq // rows
    head = jnp.arange(HG_WIDTH) // HG_KEY_DIM
    same_head = head[:, None] == head[None, :]
    gmat = same_head.astype(BF16)
    bmask = same_head.astype(F32)
    sel = (jnp.arange(HG_CHUNK)[:, None] == (jnp.arange(HG_CHUNK * HG_CHUNK) // HG_CHUNK)[None, :]).astype(BF16)
    const = lambda shape: pl.BlockSpec(shape, lambda b, i: (0, 0))
    return pl.pallas_call(
        functools.partial(_hgrn_kernel, rows=rows),
        grid=(batch, nblk),
        in_specs=[pl.BlockSpec((rows, 4 * HG_WIDTH), lambda b, i: (b * nblk + i, 0)),
                  const((1, HG_WIDTH)), const((1, HG_WIDTH)),
                  const((HG_WIDTH, HG_WIDTH)), const((HG_CHUNK, HG_CHUNK * HG_CHUNK)),
                  const((HG_WIDTH, HG_WIDTH))],
        out_specs=pl.BlockSpec((rows, HG_WIDTH), lambda b, i: (b * nblk + i, 0)),
        out_shape=jax.ShapeDtypeStruct((T, HG_WIDTH), BF16),
        scratch_shapes=[pltpu.VMEM((HG_WIDTH, HG_WIDTH), F32)]
        + [pltpu.VMEM((rows, HG_WIDTH), F32)] * 4
        + [pltpu.VMEM((rows, HG_WIDTH), BF16)] * 2
        + [pltpu.VMEM((rows, HG_WIDTH), F32)],
        compiler_params=_params("parallel", "arbitrary"),
        name="hgrn2",
    )(rec, lb.reshape(1, -1), norm_g.reshape(1, -1), gmat, sel, bmask)


def _lru_kernel(rec_ref, cw_ref, cb_ref, wa_ref, wx_ref, ba_ref, bx_ref, lam_ref, o_ref,
                xprev_ref, h_ref, *, rows):
    @pl.when(pl.program_id(1) == 0)
    def _():
        xprev_ref[...] = jnp.zeros_like(xprev_ref)
        h_ref[...] = jnp.zeros_like(h_ref)

    W = LRU_WIDTH
    x = rec_ref[:, 0:W]
    gate = rec_ref[:, W:2 * W]
    row = lax.broadcasted_iota(jnp.int32, (rows, W), 0)
    xp = xprev_ref[...]
    tail = jnp.zeros((rows - 8, W), F32)
    xc = cb_ref[...] + cw_ref[CONV_WIDTH - 1:CONV_WIDTH, :] * x
    for j in range(1, CONV_WIDTH):
        prev = jnp.concatenate([pltpu.roll(xp, j, 0), tail], axis=0)
        xs = jnp.where(row >= j, pltpu.roll(x, j, 0), prev)
        xc = xc + cw_ref[CONV_WIDTH - 1 - j:CONV_WIDTH - j, :] * xs
    xprev_ref[...] = x[rows - 8:rows, :]

    xcb = xc.astype(BF16)
    r = jax.nn.sigmoid(jnp.dot(xcb, wa_ref[...], preferred_element_type=F32) + ba_ref[...])
    i = jax.nn.sigmoid(jnp.dot(xcb, wx_ref[...], preferred_element_type=F32) + bx_ref[...])
    lam = lam_ref[...]
    log_sig = jnp.minimum(lam, 0.0) - jnp.log1p(jnp.exp(-jnp.abs(lam)))
    log_a = LRU_C * r * log_sig
    a = jnp.exp(log_a)
    th = jnp.tanh(log_a)
    one_minus_a2 = -2.0 * th / (1.0 - th)
    u = jnp.sqrt(one_minus_a2) * i * xc

    for s in (1, 2, 4):
        a_s = jnp.where(row >= s, pltpu.roll(a, s, 0), 1.0)
        u_s = jnp.where(row >= s, pltpu.roll(u, s, 0), 0.0)
        u = a * u_s + u
        a = a * a_s
    s = 8
    while s < rows:
        a_s = jnp.concatenate([jnp.ones((s, W), F32), a[:rows - s, :]], axis=0)
        u_s = jnp.concatenate([jnp.zeros((s, W), F32), u[:rows - s, :]], axis=0)
        u = a * u_s + u
        a = a * a_s
        s *= 2
    h = a * h_ref[0:1, :] + u
    h_ref[...] = jnp.broadcast_to(h[rows - 1:rows, :], h_ref.shape)
    o_ref[...] = (h * jax.nn.gelu(gate)).astype(o_ref.dtype)


def _lru(rec, conv_w, conv_b, wa_bd, wx_bd, ba, bx, lam, batch, seq, rows=512):
    T = rec.shape[0]
    nblk = seq // rows
    W = LRU_WIDTH
    const = lambda shape: pl.BlockSpec(shape, lambda b, i: (0, 0))
    return pl.pallas_call(
        functools.partial(_lru_kernel, rows=rows),
        grid=(batch, nblk),
        in_specs=[pl.BlockSpec((rows, 2 * W), lambda b, i: (b * nblk + i, 2)),
                  const((CONV_WIDTH, W)), const((1, W)), const((W, W)), const((W, W)),
                  const((1, W)), const((1, W)), const((1, W))],
        out_specs=pl.BlockSpec((rows, W), lambda b, i: (b * nblk + i, 0)),
        out_shape=jax.ShapeDtypeStruct((T, W), BF16),
        scratch_shapes=[pltpu.VMEM((8, W), F32), pltpu.VMEM((8, W), F32)],
        compiler_params=_params("parallel", "arbitrary"),
        name="rglru",
    )(rec, conv_w, conv_b.reshape(1, -1), wa_bd, wx_bd, ba.reshape(1, -1), bx.reshape(1, -1),
      lam.reshape(1, -1))


def _attn_kernel(q_ref, k_ref, v_ref, lq1_ref, lk1_ref, lq2_ref, lk2_ref, ng_ref, o_ref,
                 vt_ref, qm_ref, s0_ref, s1_ref, m_ref, l_ref, acc_ref, *, tq, tk, qs, lam_init, seq):
    d = DA_HEAD_DIM
    qi = pl.program_id(2)

    @pl.when(qi == 0)
    def _():
        def transpose_values(j, carry):
            r0 = pl.multiple_of(j * tk, tk)
            vt_ref[j] = v_ref[pl.ds(r0, tk), :].astype(F32).T.astype(BF16)
            return carry
        lax.fori_loop(0, seq // tk, transpose_values, 0)

    q = q_ref[...]
    lane = lax.broadcasted_iota(jnp.int32, q.shape, 1)
    qm_ref[0] = jnp.where(lane < d, q, jnp.zeros_like(q))
    qm_ref[1] = jnp.where(lane >= d, q, jnp.zeros_like(q))
    m_ref[...] = jnp.full_like(m_ref, -jnp.inf)
    l_ref[...] = jnp.zeros_like(l_ref)
    acc_ref[...] = jnp.zeros_like(acc_ref)
    s_refs = (s0_ref, s1_ref)

    def scores(j, c):
        k0 = pl.multiple_of(j * tk, tk)
        kb = k_ref[pl.ds(k0, tk), :]
        for st in range(tq // qs):
            s_refs[c][st] = lax.dot_general(kb, qm_ref[c, st * qs:(st + 1) * qs, :],
                                            (((1,), (1,)), ((), ())),
                                            preferred_element_type=F32)

    def softmax_pv(j, c, masked):
        for st in range(tq // qs):
            cols = pl.ds(st * qs, qs)
            nk = min((st + 1) * qs, tk) if masked else tk
            vt = vt_ref[j, :, 0:nk]
            s = s_refs[c][st, 0:nk, :]
            if masked:
                kpos = lax.broadcasted_iota(jnp.int32, (nk, qs), 0)
                qpos = st * qs + lax.broadcasted_iota(jnp.int32, (nk, qs), 1)
                s = jnp.where(kpos <= qpos, s, -jnp.inf)
            m_old = m_ref[c, :, cols]
            m_new = jnp.maximum(m_old, jnp.max(s, axis=0, keepdims=True))
            p = jnp.exp2(s - m_new)
            alpha = jnp.exp2(m_old - m_new)
            l_ref[c, :, cols] = alpha * l_ref[c, :, cols] + jnp.sum(p, axis=0, keepdims=True)
            acc_ref[c, st] = (alpha * acc_ref[c, st]
                                   + jnp.dot(vt, p.astype(BF16), preferred_element_type=F32))
            m_ref[c, :, cols] = m_new

    def full_block(j, carry):
        scores(j, 1)
        softmax_pv(j, 0, False)
        scores(j + 1, 0)
        softmax_pv(j, 1, False)
        return carry

    def four_full_blocks(jj, carry):
        for u in range(4):
            full_block(4 * jj + u, carry)
        return carry

    def remaining_full_block(j, carry):
        return full_block(j, carry)

    scores(0, 0)
    lax.fori_loop(0, qi // 4, four_full_blocks, 0)
    lax.fori_loop((qi // 4) * 4, qi, remaining_full_block, 0)

    scores(qi, 1)
    softmax_pv(qi, 0, True)
    softmax_pv(qi, 1, True)

    lam =(jnp.exp(jnp.sum(lq1_ref[...] * lk1_ref[...], axis=-1, keepdims=True))
           - jnp.exp(jnp.sum(lq2_ref[...] * lk2_ref[...], axis=-1, keepdims=True)) + lam_init)
    acc = [jnp.concatenate([acc_ref[c, st] for st in range(tq // qs)], axis=1) for c in range(2)]
    o_t = acc[0] * (1.0 / l_ref[0]) - lam * (acc[1] * (1.0 / l_ref[1]))
    ms = jnp.mean(o_t * o_t, axis=0, keepdims=True)
    o_t = o_t * lax.rsqrt(ms + EPS) * ng_ref[...] * (1.0 - lam_init)
    o_ref[...] = o_t.T.astype(o_ref.dtype)


def _attention(att, lq1, lk1, lq2, lk2, norm_g, layer, batch, seq, tq=512, qs=256):
    T = att.shape[0]
    tk = tq
    nq = seq // tq
    hw = 2 * DA_HEAD_DIM
    lam_init = 0.8 - 0.6 * math.exp(-0.3 * layer)
    vec = lambda: pl.BlockSpec((1, DA_HEAD_DIM), lambda b, h, i: (0, 0))
    return pl.pallas_call(
        functools.partial(_attn_kernel, tq=tq, tk=tk, qs=qs, lam_init=lam_init, seq=seq),
        grid=(batch, DA_HEADS, nq),
        in_specs=[pl.BlockSpec((tq, hw), lambda b, h, i: (b * nq + i, h)),
                  pl.BlockSpec((seq, hw), lambda b, h, i: (b, DA_HEADS + h)),
                  pl.BlockSpec((seq, hw), lambda b, h, i: (b, 2 * DA_HEADS + h)),
                  vec(), vec(), vec(), vec(),
                  pl.BlockSpec((hw, 1), lambda b, h, i: (h, 0))],
        out_specs=pl.BlockSpec((tq, hw), lambda b, h, i: (b * nq + i, h)),
        out_shape=jax.ShapeDtypeStruct((T, DA_WIDTH), BF16),
        scratch_shapes=[pltpu.VMEM((seq // tk, hw, tk), BF16), pltpu.VMEM((2, tq, hw), BF16),
                        pltpu.VMEM((tq // qs, tk, qs), F32), pltpu.VMEM((tq // qs, tk, qs), F32),
                        pltpu.VMEM((2, 1, tq), F32), pltpu.VMEM((2, 1, tq), F32),
                        pltpu.VMEM((2, tq // qs, hw, qs), F32)],
        compiler_params=_params("arbitrary", "arbitrary", "arbitrary"),
        name="diff_attn",
    )(att, att, att, lq1.reshape(1, -1), lk1.reshape(1, -1), lq2.reshape(1, -1),
      lk2.reshape(1, -1), norm_g.reshape(-1, 1))


def _out_proj_kernel(hg_ref, lru_ref, da_ref, w_ref, x_ref, g_ref, b_ref, o_ref):
    h = jnp.dot(hg_ref[...], w_ref[0:HG_WIDTH, :], preferred_element_type=F32)
    h = h + jnp.dot(lru_ref[...], w_ref[HG_WIDTH:HG_WIDTH + LRU_WIDTH, :], preferred_element_type=F32)
    h = h + jnp.dot(da_ref[...], w_ref[HG_WIDTH + LRU_WIDTH:, :], preferred_element_type=F32)
    o_ref[...] = _layer_norm_rows(DN_ALPHA * x_ref[...] + h, g_ref[...], b_ref[...])


def _out_proj_ln(o_hg, o_lru, o_da, w_bf16, x, g, b, tm=1024):
    T = x.shape[0]
    rows = lambda w: pl.BlockSpec((tm, w), lambda i: (i, 0))
    const = lambda shape: pl.BlockSpec(shape, lambda i: (0, 0))
    return pl.pallas_call(
        _out_proj_kernel,
        grid=(T // tm,),
        in_specs=[rows(HG_WIDTH), rows(LRU_WIDTH), rows(DA_WIDTH), const((D_MODEL, D_MODEL)),
                  rows(D_MODEL), const((1, D_MODEL)), const((1, D_MODEL))],
        out_specs=rows(D_MODEL),
        out_shape=jax.ShapeDtypeStruct((T, D_MODEL), F32),
        compiler_params=_params("parallel"),
        name="out_proj_ln",
    )(o_hg, o_lru, o_da, w_bf16, x, g.reshape(1, -1), b.reshape(1, -1))


def _ffn_kernel(x_ref, wg_ref, wu_ref, wd_ref, g_ref, b_ref, o_ref, xb_ref, acc_ref):
    f = pl.program_id(1)

    @pl.when(f == 0)
    def _():
        xb_ref[...] = x_ref[...].astype(BF16)
        acc_ref[...] = jnp.zeros_like(acc_ref)

    xb = xb_ref[...]
    gate = jnp.dot(xb, wg_ref[...], preferred_element_type=F32)
    up = jnp.dot(xb, wu_ref[...], preferred_element_type=F32)
    hmid = (gate * jax.nn.sigmoid(gate) * up).astype(BF16)
    acc_ref[...] += jnp.dot(hmid, wd_ref[...], preferred_element_type=F32)

    @pl.when(f == pl.num_programs(1) - 1)
    def _():
        o_ref[...] = _layer_norm_rows(DN_ALPHA * x_ref[...] + acc_ref[...], g_ref[...], b_ref[...])


def _ffn_ln(x, wg, wu, wd, g, b, tm=1024, tf=512):
    T = x.shape[0]
    return pl.pallas_call(
        _ffn_kernel,
        grid=(T // tm, D_FF // tf),
        in_specs=[pl.BlockSpec((tm, D_MODEL), lambda i, f: (i, 0)),
                  pl.BlockSpec((D_MODEL, tf), lambda i, f: (0, f)),
                  pl.BlockSpec((D_MODEL, tf), lambda i, f: (0, f)),
                  pl.BlockSpec((tf, D_MODEL), lambda i, f: (f, 0)),
                  pl.BlockSpec((1, D_MODEL), lambda i, f: (0, 0)),
                  pl.BlockSpec((1, D_MODEL), lambda i, f: (0, 0))],
        out_specs=pl.BlockSpec((tm, D_MODEL), lambda i, f: (i, 0)),
        out_shape=jax.ShapeDtypeStruct((T, D_MODEL), F32),
        scratch_shapes=[pltpu.VMEM((tm, D_MODEL), BF16), pltpu.VMEM((tm, D_MODEL), F32)],
        compiler_params=_params("parallel", "arbitrary"),
        name="ffn_ln",
    )(x, wg, wu, wd, g.reshape(1, -1), b.reshape(1, -1))


ROUTER_LANES = 128
META_I1, META_I2, META_R1, META_R2 = 0, 1, 2, 3


def _router_kernel(x_ref, rwh_ref, rwl_ref, tri_ref, meta_ref, gate_ref, cnt_ref, carry_ref):
    @pl.when(pl.program_id(0) == 0)
    def _():
        carry_ref[...] = jnp.zeros_like(carry_ref)

    x = x_ref[...]
    x_hi = x.astype(BF16)
    x_lo = (x - x_hi.astype(F32)).astype(BF16)
    logits = (jnp.dot(x_hi, rwh_ref[...], preferred_element_type=F32)
              + jnp.dot(x_lo, rwh_ref[...], preferred_element_type=F32)
              + jnp.dot(x_hi, rwl_ref[...], preferred_element_type=F32))
    lane = lax.broadcasted_iota(jnp.int32, logits.shape, 1)
    neg = -jnp.inf
    l1 = jnp.where(lane < N_EXPERTS, logits, neg)
    v1 = jnp.max(l1, axis=-1, keepdims=True)
    i1 = jnp.min(jnp.where(l1 == v1, lane, ROUTER_LANES), axis=-1, keepdims=True)
    l2 = jnp.where(lane == i1, neg, l1)
    v2 = jnp.max(l2, axis=-1, keepdims=True)
    i2 = jnp.min(jnp.where(l2 == v2, lane, ROUTER_LANES), axis=-1, keepdims=True)
    e2 = jnp.exp(v2 - v1)
    g1 = 1.0 / (1.0 + e2)
    g2 = e2 / (1.0 + e2)
    member = jnp.where((lane == i1) | (lane == i2), 1.0, 0.0)
    rank = jnp.dot(tri_ref[...], member.astype(BF16), preferred_element_type=F32) + carry_ref[0:1, :]
    r1 = jnp.sum(jnp.where(lane == i1, rank, 0.0), axis=-1, keepdims=True).astype(jnp.int32)
    r2 = jnp.sum(jnp.where(lane == i2, rank, 0.0), axis=-1, keepdims=True).astype(jnp.int32)
    meta = jnp.where(lane == META_I1, i1, 0) + jnp.where(lane == META_I2, i2, 0)
    meta = meta + jnp.where(lane == META_R1, r1, 0) + jnp.where(lane == META_R2, r2, 0)
    meta_ref[...] = meta
    gate_ref[...] = jnp.where(lane == 0, g1, 0.0) + jnp.where(lane == 1, g2, 0.0)
    carry = carry_ref[...] + jnp.sum(member, axis=0, keepdims=True)
    carry_ref[...] = carry
    cnt_ref[...] = carry


def _router(x, router_pad, tm=1024):
    T = x.shape[0]
    tri = (jnp.arange(tm)[:, None] > jnp.arange(tm)[None, :]).astype(BF16)
    rw_hi = router_pad.astype(BF16)
    rw_lo = (router_pad - rw_hi.astype(F32)).astype(BF16)
    return pl.pallas_call(
        _router_kernel,
        grid=(T // tm,),
        in_specs=[pl.BlockSpec((tm, D_MODEL), lambda i: (i, 0)),
                  pl.BlockSpec((D_MODEL, ROUTER_LANES), lambda i: (0, 0)),
                  pl.BlockSpec((D_MODEL, ROUTER_LANES), lambda i: (0, 0)),
                  pl.BlockSpec((tm, tm), lambda i: (0, 0))],
        out_specs=[pl.BlockSpec((tm, ROUTER_LANES), lambda i: (i, 0)),
                   pl.BlockSpec((tm, ROUTER_LANES), lambda i: (i, 0)),
                   pl.BlockSpec((8, ROUTER_LANES), lambda i: (0, 0))],
        out_shape=[jax.ShapeDtypeStruct((T, ROUTER_LANES), jnp.int32),
                   jax.ShapeDtypeStruct((T, ROUTER_LANES), F32),
                   jax.ShapeDtypeStruct((8, ROUTER_LANES), F32)],
        scratch_shapes=[pltpu.VMEM((8, ROUTER_LANES), F32)],
        compiler_params=_params("arbitrary"),
        name="moe_router",
    )(x, rw_hi, rw_lo, tri)


def _dispatch_tables(meta, cnt, tm):
    T = meta.shape[0]
    counts = cnt[0, :N_EXPERTS].astype(jnp.int32)
    padded = ((counts + tm - 1) // tm) * tm
    ends = jnp.cumsum(padded)
    off = ends - padded
    experts = jnp.arange(N_EXPERTS, dtype=jnp.int32)
    group_start = lambda e: jnp.sum(jnp.where(e[:, None] == experts[None, :], off[None, :], 0), axis=1)
    pos1 = group_start(meta[:, META_I1]) + meta[:, META_R1]
    pos2 = group_start(meta[:, META_I2]) + meta[:, META_R2]
    n_tiles = 2 * T // tm + N_EXPERTS + 1
    tok = jnp.arange(T, dtype=jnp.int32)
    code_of = jnp.full((n_tiles * tm,), 2 * T, jnp.int32).at[jnp.concatenate([pos1, pos2])].set(
        jnp.concatenate([tok, tok + T]))
    tile_start = jnp.arange(n_tiles, dtype=jnp.int32) * tm
    tile_expert = jnp.minimum(jnp.sum((tile_start[:, None] >= ends[None, :]).astype(jnp.int32), axis=1),
                              N_EXPERTS - 1)
    n_used = (ends[-1] // tm).astype(jnp.int32).reshape(1)
    row = jnp.arange(n_tiles * tm, dtype=jnp.int32)
    is_pad = code_of >= 2 * T
    src_of = jnp.where(is_pad, 0, jnp.where(code_of >= T, code_of - T, code_of))
    dst_of = jnp.where(is_pad, 2 * T + ((row // tm) % 2) * tm + row % tm, code_of)
    return src_of.reshape(n_tiles, 1, tm), dst_of.reshape(n_tiles, 1, tm), tile_expert, n_used


def _expert_kernel(te_ref, nu_ref, cur_ref, nxt_ref, prv_ref, x_hbm, wg_ref, wu_ref, wd_ref, y_hbm,
                   xbuf_ref, xb_ref, acc_ref, sem_g, sem_s, *, tm, nf, n_tok):
    i = pl.program_id(0)
    f = pl.program_id(1)
    n_used = nu_ref[0]
    slot = i % 2
    other = 1 - slot
    pad_code = 2 * n_tok

    def gather(src_ref, r, buf):
        return pltpu.make_async_copy(x_hbm.at[pl.ds(src_ref[0, 0, r], 1)],
                                     xbuf_ref.at[buf, pl.ds(r, 1)], sem_g.at[buf])

    def scatter(r, buf, to_dump):
        dest = jnp.where(to_dump, pad_code + buf * tm + r, prv_ref[0, 0, r])
        return pltpu.make_async_copy(acc_ref.at[buf, pl.ds(r, 1)], y_hbm.at[pl.ds(dest, 1)], sem_s)

    def wait_gather(buf):
        pltpu.make_async_copy(x_hbm.at[pl.ds(0, tm)], xbuf_ref.at[buf], sem_g.at[buf]).wait()

    def wait_scatter(buf):
        pltpu.make_async_copy(acc_ref.at[buf], y_hbm.at[pl.ds(0, tm)], sem_s).wait()

    @pl.when((i < n_used) & (f == 0))
    def _():
        @pl.when(i == 0)
        def _():
            def issue(r, carry):
                gather(cur_ref, r, 0).start()
                return carry
            lax.fori_loop(0, tm, issue, 0, unroll=8)
            acc_ref[1] = jnp.zeros((tm, D_MODEL), F32)
            clear = pltpu.make_async_copy(acc_ref.at[1], y_hbm.at[pl.ds(pad_code, tm)], sem_s)
            clear.start()
            clear.wait()

        wait_gather(slot)
        xb_ref[...] = xbuf_ref[slot].astype(BF16)

    def tile_step(fs):
        if fs == 0:
            for r in range(tm):
                gather(nxt_ref, r, other).start()
        if fs == nf - 1:
            for r in range(tm):
                scatter(r, other, i == 0).start()
        xb = xb_ref[...]
        gate = jnp.dot(xb, wg_ref[0], preferred_element_type=F32)
        up = jnp.dot(xb, wu_ref[0], preferred_element_type=F32)
        hmid = (gate * jax.nn.sigmoid(gate) * up).astype(BF16)
        down = jnp.dot(hmid, wd_ref[0], preferred_element_type=F32)
        if fs == 0:
            acc_ref[slot] = down
        else:
            acc_ref[slot] += down
        if fs == nf - 1:
            wait_scatter(other)

    for fs in range(nf):
        pl.when((i < n_used) & (f == fs))(functools.partial(tile_step, fs))

    @pl.when((i == n_used) & (f == 0))
    def _():
        wait_gather(slot)

        def issue(r, carry):
            scatter(r, other, False).start()
            return carry
        lax.fori_loop(0, tm, issue, 0, unroll=8)
        wait_scatter(other)


def _experts(x, src_of, dst_of, tile_expert, n_used, wg, wu, wd, tm, tf=1792):
    n_tok = x.shape[0]
    n_tiles = src_of.shape[0]
    nf = D_FF // tf
    fidx = lambda i, f, nu: jnp.where(i < nu[0], f, nf - 1)
    rows_of_tile = lambda shift: pl.BlockSpec(
        (1, 1, tm), lambda i, f, te, nu: (jnp.clip(i + shift, 0, n_tiles - 1), 0, 0),
        memory_space=pltpu.SMEM)
    grid_spec = pltpu.PrefetchScalarGridSpec(
        num_scalar_prefetch=2,
        grid=(n_tiles, nf),
        in_specs=[rows_of_tile(0), rows_of_tile(1), rows_of_tile(-1),
                  pl.BlockSpec(memory_space=pl.ANY),
                  pl.BlockSpec((1, D_MODEL, tf), lambda i, f, te, nu: (te[i], 0, fidx(i, f, nu))),
                  pl.BlockSpec((1, D_MODEL, tf), lambda i, f, te, nu: (te[i], 0, fidx(i, f, nu))),
                  pl.BlockSpec((1, tf, D_MODEL), lambda i, f, te, nu: (te[i], fidx(i, f, nu), 0))],
        out_specs=pl.BlockSpec(memory_space=pl.ANY),
        scratch_shapes=[pltpu.VMEM((2, tm, D_MODEL), F32), pltpu.VMEM((tm, D_MODEL), BF16),
                        pltpu.VMEM((2, tm, D_MODEL), F32),
                        pltpu.SemaphoreType.DMA((2,)), pltpu.SemaphoreType.DMA(())],
    )
    return pl.pallas_call(
        functools.partial(_expert_kernel, tm=tm, nf=nf, n_tok=n_tok),
        grid_spec=grid_spec,
        out_shape=jax.ShapeDtypeStruct((2 * n_tok + 2 * tm, D_MODEL), F32),
        compiler_params=_params("arbitrary", "arbitrary"),
        name="moe_experts",
    )(tile_expert, n_used, src_of, src_of, dst_of, x, wg, wu, wd)


def _combine_kernel(x_ref, y1_ref, y2_ref, gate_ref, g_ref, b_ref, o_ref):
    gates = gate_ref[...]
    y = gates[:, 0:1] * y1_ref[...] + gates[:, 1:2] * y2_ref[...]
    o_ref[...] = _layer_norm_rows(DN_ALPHA * x_ref[...] + y, g_ref[...], b_ref[...])


def _combine_ln(x, y, gates, g, b, tm=1024):
    T = x.shape[0]
    nt = T // tm
    return pl.pallas_call(
        _combine_kernel,
        grid=(nt,),
        in_specs=[pl.BlockSpec((tm, D_MODEL), lambda i: (i, 0)),
                  pl.BlockSpec((tm, D_MODEL), lambda i: (i, 0)),
                  pl.BlockSpec((tm, D_MODEL), lambda i: (nt + i, 0)),
                  pl.BlockSpec((tm, ROUTER_LANES), lambda i: (i, 0)),
                  pl.BlockSpec((1, D_MODEL), lambda i: (0, 0)),
                  pl.BlockSpec((1, D_MODEL), lambda i: (0, 0))],
        out_specs=pl.BlockSpec((tm, D_MODEL), lambda i: (i, 0)),
        out_shape=jax.ShapeDtypeStruct((T, D_MODEL), F32),
        compiler_params=_params("parallel"),
        name="moe_combine_ln",
    )(x, y, y, gates, g.reshape(1, -1), b.reshape(1, -1))


def _moe_ln(x, router_w, wg, wu, wd, g, b, tm=512):
    router_pad = jnp.pad(router_w, ((0, 0), (0, ROUTER_LANES - N_EXPERTS)))
    meta, gates, cnt = _router(x, router_pad)
    src_of, dst_of, tile_expert, n_used = _dispatch_tables(meta, cnt, tm)
    y = _experts(x, src_of, dst_of, tile_expert, n_used, wg, wu, wd, tm)
    return _combine_ln(x, y, gates, g, b)


def _block_diag(w):
    nb, n, _ = w.shape
    eye = jnp.eye(nb, dtype=w.dtype)
    return (eye[:, None, :, None] * w[:, :, None, :]).reshape(nb * n, nb * n)


def kernel(x, w_in, w_out, hg_lb_logits, hg_norm_g, lru_conv_w, lru_conv_b, lru_wa, lru_ba, lru_wx, lru_bx, lru_lambda, da_lq1, da_lk1, da_lq2, da_lk2, da_norm_g, ln1_g, ln1_b, ln2_g, ln2_b, ffn_wg, ffn_wu, ffn_wd, router_w, moe_wg, moe_wu, moe_wd):
    batch, seq, _ = x.shape
    depth = w_in.shape[0]
    xf = x.reshape(batch * seq, D_MODEL)
    lb_p = jax.nn.softmax(hg_lb_logits.astype(F32), axis=0)
    lb_all = jnp.cumsum(lb_p, axis=0) - lb_p[0:1]
    for l in range(depth):
        rec, att = _in_proj(xf, w_in[l].astype(BF16))
        o_hg = _hgrn(rec, lb_all[l], hg_norm_g[l], batch, seq)
        o_lru = _lru(rec, lru_conv_w[l], lru_conv_b[l], _block_diag(lru_wa[l]).astype(BF16),
                     _block_diag(lru_wx[l]).astype(BF16), lru_ba[l], lru_bx[l], lru_lambda[l],
                     batch, seq)
        o_da = _attention(att, da_lq1[l], da_lk1[l], da_lq2[l], da_lk2[l], da_norm_g[l], l,
                          batch, seq)
        xf = _out_proj_ln(o_hg, o_lru, o_da, w_out[l].astype(BF16), xf, ln1_g[l], ln1_b[l])
        j = l // 2
        if l % 2 == 0:
            xf = _ffn_ln(xf, ffn_wg[j].astype(BF16), ffn_wu[j].astype(BF16), ffn_wd[j].astype(BF16),
                         ln2_g[l], ln2_b[l])
        else:
            xf = _moe_ln(xf, router_w[j], moe_wg[j].astype(BF16), moe_wu[j].astype(BF16),
                         moe_wd[j].astype(BF16), ln2_g[l], ln2_b[l])
    return xf.reshape(batch, seq, D_MODEL)
```

```python
import functools
import math

import jax
import jax.numpy as jnp
from jax import lax
from jax.experimental import pallas as pl
from jax.experimental.pallas import tpu as pltpu

D_MODEL = 1024
DEPTH = 2
HG_HEADS = 4
HG_KEY_DIM = 64
HG_WIDTH = 256
HG_CHUNK = 16
LRU_WIDTH = 256
LRU_BLOCKS = 4
CONV_WIDTH = 4
LRU_C = 8.0
DA_WIDTH = 512
DA_HEADS = 4
DA_HEAD_DIM = 64
D_FF = 3584
N_EXPERTS = 8
DN_ALPHA = (2.0 * DEPTH) ** 0.25
EPS = 1e-5
REC_WIDTH = 4 * HG_WIDTH + 2 * LRU_WIDTH
ATT_WIDTH = 3 * DA_WIDTH

V7X_VMEM_BYTES = 64 * 1024 * 1024
VMEM_LIMIT = V7X_VMEM_BYTES * 3 // 4

BF16 = jnp.bfloat16
F32 = jnp.float32


def _params(*semantics):
    return pltpu.CompilerParams(dimension_semantics=semantics, vmem_limit_bytes=VMEM_LIMIT)


def _layer_norm_rows(y, g, b):
    mu = jnp.mean(y, axis=-1, keepdims=True)
    yc = y - mu
    var = jnp.mean(yc * yc, axis=-1, keepdims=True)
    return yc * lax.rsqrt(var + EPS) * g + b


Q_SCALE = DA_HEAD_DIM ** -0.5 * math.log2(math.e)


def _in_proj_kernel(x_ref, w_ref, rec_ref, att_ref):
    xb = x_ref[...].astype(BF16)
    rec_ref[...] = jnp.dot(xb, w_ref[:, :REC_WIDTH], preferred_element_type=F32)
    q = jnp.dot(xb, w_ref[:, REC_WIDTH:REC_WIDTH + DA_WIDTH], preferred_element_type=F32)
    att_ref[:, :DA_WIDTH] = (q * Q_SCALE).astype(BF16)
    att_ref[:, DA_WIDTH:] = jnp.dot(xb, w_ref[:, REC_WIDTH + DA_WIDTH:],
                                    preferred_element_type=F32).astype(BF16)


def _in_proj(x, w_bf16, tm=512):
    T = x.shape[0]
    return pl.pallas_call(
        _in_proj_kernel,
        grid=(T // tm,),
        in_specs=[pl.BlockSpec((tm, D_MODEL), lambda i: (i, 0)),
                  pl.BlockSpec((D_MODEL, REC_WIDTH + ATT_WIDTH), lambda i: (0, 0))],
        out_specs=[pl.BlockSpec((tm, REC_WIDTH), lambda i: (i, 0)),
                   pl.BlockSpec((tm, ATT_WIDTH), lambda i: (i, 0))],
        out_shape=[jax.ShapeDtypeStruct((T, REC_WIDTH), F32),
                   jax.ShapeDtypeStruct((T, ATT_WIDTH), BF16)],
        compiler_params=_params("parallel"),
        name="in_proj",
    )(x, w_bf16)


def _hgrn_kernel(rec_ref, lb_ref, ng_ref, gmat_ref, sel_ref, bmask_ref, o_ref,
                 st_ref, q_s, kk_s, b_s, bl_s, qd_s, kd_s, o_s, *, rows):
    @pl.when(pl.program_id(1) == 0)
    def _():
        st_ref[...] = jnp.zeros_like(st_ref)

    C = HG_CHUNK
    W = HG_WIDTH
    nc = rows // C
    lb = lb_ref[...]
    gmat = gmat_ref[...]
    sel = sel_ref[...]

    qr = rec_ref[:, 0:W]
    z = rec_ref[:, W:2 * W]
    rin = lax.broadcasted_iota(jnp.int32, (rows, W), 0) % C
    logf = jnp.log(lb + (1.0 - lb) * jax.nn.sigmoid(z))
    kk = (1.0 - lb) * jax.nn.sigmoid(-z)
    q = qr * jax.nn.sigmoid(qr)
    b = logf
    for s in (1, 2, 4, 8):
        b = b + jnp.where(rin >= s, pltpu.roll(b, s, 0), 0.0)
    r = jnp.where(rin < C - 1, pltpu.roll(logf, rows - 1, 0), 0.0)
    for s in (1, 2, 4, 8):
        r = r + jnp.where(rin < C - s, pltpu.roll(r, rows - s, 0), 0.0)
    q_s[...] = q
    kk_s[...] = kk
    b_s[...] = b
    bl_s[...] = b + r
    qd_s[...] = (q * jnp.exp(b)).astype(BF16)
    kd_s[...] = (kk * jnp.exp(r)).astype(BF16)

    row = lax.broadcasted_iota(jnp.int32, (C, W), 0)

    def chunk(c, carry):
        r0 = pl.multiple_of(c * C, C)
        q = q_s[pl.ds(r0, C), :]
        kk = kk_s[pl.ds(r0, C), :]
        b = b_s[pl.ds(r0, C), :]
        v = rec_ref[pl.ds(r0, C), 2 * W:3 * W]
        w_rows = []
        for t in range(C):
            rel = b[t:t + 1, :] - b
            dec = jnp.exp(jnp.where(row <= t, rel, -jnp.inf))
            w_rows.append(((q[t:t + 1, :] * kk) * dec).astype(BF16))
        w2 = jnp.concatenate(w_rows, axis=0)
        a = jnp.dot(w2, gmat, preferred_element_type=F32)
        p = a * jnp.concatenate([v] * C, axis=0)
        o_intra = jnp.dot(sel, p.astype(BF16), preferred_element_type=F32)
        u_t = lax.dot_general(v.astype(BF16), kd_s[pl.ds(r0, C), :], (((0,), (0,)), ((), ())),
                              preferred_element_type=F32)
        st = st_ref[...]
        o_inter = lax.dot_general(qd_s[pl.ds(r0, C), :], st.astype(BF16), (((1,), (1,)), ((), ())),
                                  preferred_element_type=F32)
        o_s[pl.ds(r0, C), :] = o_intra + o_inter
        st_ref[...] = st * jnp.exp(bl_s[pl.ds(r0, 1), :]) + u_t * bmask_ref[...]
        return carry

    lax.fori_loop(0, nc, chunk, 0, unroll=16)

    o = o_s[...]
    g = rec_ref[:, 3 * W:4 * W]
    sq = o * o
    sq_hi = sq.astype(BF16)
    sq_lo = (sq - sq_hi.astype(F32)).astype(BF16)
    ms = (jnp.dot(sq_hi, gmat, preferred_element_type=F32)
          + jnp.dot(sq_lo, gmat, preferred_element_type=F32)) * (1.0 / HG_KEY_DIM)
    out = o * lax.rsqrt(ms + EPS) * ng_ref[...] * (g * jax.nn.sigmoid(g))
    o_ref[...] = out.astype(o_ref.dtype)


def _hgrn(rec, lb, norm_g, batch, seq, rows=512):
    T = rec.shape[0]
    nblk = seq // rows
    head = jnp.arange(HG_WIDTH) // HG_KEY_DIM
    same_head = head[:, None] == head[None, :]
    gmat = same_head.astype(BF16)
    bmask = same_head.astype(F32)
    sel = (jnp.arange(HG_CHUNK)[:, None] == (jnp.arange(HG_CHUNK * HG_CHUNK) // HG_CHUNK)[None, :]).astype(BF16)
    const = lambda shape: pl.BlockSpec(shape, lambda b, i: (0, 0))
    return pl.pallas_call(
        functools.partial(_hgrn_kernel, rows=rows),
        grid=(batch, nblk),
        in_specs=[pl.BlockSpec((rows, 4 * HG_WIDTH), lambda b, i: (b * nblk + i, 0)),
                  const((1, HG_WIDTH)), const((1, HG_WIDTH)),
                  const((HG_WIDTH, HG_WIDTH)), const((HG_CHUNK, HG_CHUNK * HG_CHUNK)),
                  const((HG_WIDTH, HG_WIDTH))],
        out_specs=pl.BlockSpec((rows, HG_WIDTH), lambda b, i: (b * nblk + i, 0)),
        out_shape=jax.ShapeDtypeStruct((T, HG_WIDTH), BF16),
        scratch_shapes=[pltpu.VMEM((HG_WIDTH, HG_WIDTH), F32)]
        + [pltpu.VMEM((rows, HG_WIDTH), F32)] * 4
        + [pltpu.VMEM((rows, HG_WIDTH), BF16)] * 2
        + [pltpu.VMEM((rows, HG_WIDTH), F32)],
        compiler_params=_params("parallel", "arbitrary"),
        name="hgrn2",
    )(rec, lb.reshape(1, -1), norm_g.reshape(1, -1), gmat, sel, bmask)


def _lru_kernel(rec_ref, cw_ref, cb_ref, wa_ref, wx_ref, ba_ref, bx_ref, lam_ref, o_ref,
                xprev_ref, h_ref, *, rows):
    @pl.when(pl.program_id(1) == 0)
    def _():
        xprev_ref[...] = jnp.zeros_like(xprev_ref)
        h_ref[...] = jnp.zeros_like(h_ref)

    W = LRU_WIDTH
    x = rec_ref[:, 0:W]
    gate = rec_ref[:, W:2 * W]
    row = lax.broadcasted_iota(jnp.int32, (rows, W), 0)
    xp = xprev_ref[...]
    tail = jnp.zeros((rows - 8, W), F32)
    xc = cb_ref[...] + cw_ref[CONV_WIDTH - 1:CONV_WIDTH, :] * x
    for j in range(1, CONV_WIDTH):
        prev = jnp.concatenate([pltpu.roll(xp, j, 0), tail], axis=0)
        xs = jnp.where(row >= j, pltpu.roll(x, j, 0), prev)
        xc = xc + cw_ref[CONV_WIDTH - 1 - j:CONV_WIDTH - j, :] * xs
    xprev_ref[...] = x[rows - 8:rows, :]

    xcb = xc.astype(BF16)
    r = jax.nn.sigmoid(jnp.dot(xcb, wa_ref[...], preferred_element_type=F32) + ba_ref[...])
    i = jax.nn.sigmoid(jnp.dot(xcb, wx_ref[...], preferred_element_type=F32) + bx_ref[...])
    lam = lam_ref[...]
    log_sig = jnp.minimum(lam, 0.0) - jnp.log1p(jnp.exp(-jnp.abs(lam)))
    log_a = LRU_C * r * log_sig
    a = jnp.exp(log_a)
    th = jnp.tanh(log_a)
    one_minus_a2 = -2.0 * th / (1.0 - th)
    u = jnp.sqrt(one_minus_a2) * i * xc

    for s in (1, 2, 4):
        a_s = jnp.where(row >= s, pltpu.roll(a, s, 0), 1.0)
        u_s = jnp.where(row >= s, pltpu.roll(u, s, 0), 0.0)
        u = a * u_s + u
        a = a * a_s
    s = 8
    while s < rows:
        a_s = jnp.concatenate([jnp.ones((s, W), F32), a[:rows - s, :]], axis=0)
        u_s = jnp.concatenate([jnp.zeros((s, W), F32), u[:rows - s, :]], axis=0)
        u = a * u_s + u
        a = a * a_s
        s *= 2
    h = a * h_ref[0:1, :] + u
    h_ref[...] = jnp.broadcast_to(h[rows - 1:rows, :], h_ref.shape)
    o_ref[...] = (h * jax.nn.gelu(gate)).astype(o_ref.dtype)


def _lru(rec, conv_w, conv_b, wa_bd, wx_bd, ba, bx, lam, batch, seq, rows=512):
    T = rec.shape[0]
    nblk = seq // rows
    W = LRU_WIDTH
    const = lambda shape: pl.BlockSpec(shape, lambda b, i: (0, 0))
    return pl.pallas_call(
        functools.partial(_lru_kernel, rows=rows),
        grid=(batch, nblk),
        in_specs=[pl.BlockSpec((rows, 2 * W), lambda b, i: (b * nblk + i, 2)),
                  const((CONV_WIDTH, W)), const((1, W)), const((W, W)), const((W, W)),
                  const((1, W)), const((1, W)), const((1, W))],
        out_specs=pl.BlockSpec((rows, W), lambda b, i: (b * nblk + i, 0)),
        out_shape=jax.ShapeDtypeStruct((T, W), BF16),
        scratch_shapes=[pltpu.VMEM((8, W), F32), pltpu.VMEM((8, W), F32)],
        compiler_params=_params("parallel", "arbitrary"),
        name="rglru",
    )(rec, conv_w, conv_b.reshape(1, -1), wa_bd, wx_bd, ba.reshape(1, -1), bx.reshape(1, -1),
      lam.reshape(1, -1))


def _attn_kernel(q_ref, qn_ref, k_ref, v_ref, lq1_ref, lk1_ref, lq2_ref, lk2_ref, ng_ref, o_ref,
                 vt_ref, qm_ref, s0_ref, s1_ref, m_ref, l_ref, acc_ref, *, tq, tk, qs, lam_init, seq):
    d = DA_HEAD_DIM
    qi = pl.program_id(2)

    @pl.when(qi == 0)
    def _():
        def transpose_values(j, carry):
            r0 = pl.multiple_of(j * tk, tk)
            vt_ref[j] = v_ref[pl.ds(r0, tk), :].astype(F32).T.astype(BF16)
            return carry
        lax.fori_loop(0, seq // tk, transpose_values, 0)

    q = q_ref[...]
    lane = lax.broadcasted_iota(jnp.int32, q.shape, 1)
    qm_ref[0] = jnp.where(lane < d, q, jnp.zeros_like(q))
    qm_ref[1] = jnp.where(lane >= d, q, jnp.zeros_like(q))
    m_ref[...] = jnp.full_like(m_ref, -jnp.inf)
    l_ref[...] = jnp.zeros_like(l_ref)
    acc_ref[...] = jnp.zeros_like(acc_ref)
    s_refs = (s0_ref, s1_ref)

    def scores(j, c, queries=None):
        k0 = pl.multiple_of(j * tk, tk)
        kb = k_ref[pl.ds(k0, tk), :]
        qsrc = c if queries is None else queries
        for st in range(tq // qs):
            s_refs[c][st] = lax.dot_general(kb, qm_ref[qsrc, st * qs:(st + 1) * qs, :],
                                            (((1,), (1,)), ((), ())),
                                            preferred_element_type=F32)

    def softmax_pv(j, c, masked):
        for st in range(tq // qs):
            cols = pl.ds(st * qs, qs)
            nk = min((st + 1) * qs, tk) if masked else tk
            vt = vt_ref[j, :, 0:nk]
            s = s_refs[c][st, 0:nk, :]
            if masked:
                kpos = lax.broadcasted_iota(jnp.int32, (nk, qs), 0)
                qpos = st * qs + lax.broadcasted_iota(jnp.int32, (nk, qs), 1)
                s = jnp.where(kpos <= qpos, s, -jnp.inf)
            m_old = m_ref[c, :, cols]
            m_new = jnp.maximum(m_old, jnp.max(s, axis=0, keepdims=True))
            p = jnp.exp2(s - m_new)
            alpha = jnp.exp2(m_old - m_new)
            l_ref[c, :, cols] = alpha * l_ref[c, :, cols] + jnp.sum(p, axis=0, keepdims=True)
            acc_ref[c, st] = (alpha * acc_ref[c, st]
                                   + jnp.dot(vt, p.astype(BF16), preferred_element_type=F32))
            m_ref[c, :, cols] = m_new

    def full_block(j, carry):
        scores(j, 1)
        softmax_pv(j, 0, False)
        scores(j + 1, 0)
        softmax_pv(j, 1, False)
        return carry

    def four_full_blocks(jj, carry):
        for u in range(4):
            full_block(4 * jj + u, carry)
        return carry

    def remaining_full_block(j, carry):
        return full_block(j, carry)

    @pl.when(qi == 0)
    def _():
        scores(0, 0)

    lax.fori_loop(0, qi // 4, four_full_blocks, 0)
    lax.fori_loop((qi // 4) * 4, qi, remaining_full_block, 0)

    scores(qi, 1)
    softmax_pv(qi, 0, True)
    qn = qn_ref[...]
    qm_ref[2] = jnp.where(lane < d, qn, jnp.zeros_like(qn))
    scores(0, 0, queries=2)
    softmax_pv(qi, 1, True)

    lam =(jnp.exp(jnp.sum(lq1_ref[...] * lk1_ref[...], axis=-1, keepdims=True))
           - jnp.exp(jnp.sum(lq2_ref[...] * lk2_ref[...], axis=-1, keepdims=True)) + lam_init)
    acc = [jnp.concatenate([acc_ref[c, st] for st in range(tq // qs)], axis=1) for c in range(2)]
    o_t = acc[0] * (1.0 / l_ref[0]) - lam * (acc[1] * (1.0 / l_ref[1]))
    ms = jnp.mean(o_t * o_t, axis=0, keepdims=True)
    o_t = o_t * lax.rsqrt(ms + EPS) * ng_ref[...] * (1.0 - lam_init)
    o_ref[...] = o_t.T.astype(o_ref.dtype)


def _attention(att, lq1, lk1, lq2, lk2, norm_g, layer, batch, seq, tq=512, qs=256):
    T = att.shape[0]
    tk = tq
    nq = seq // tq
    hw = 2 * DA_HEAD_DIM
    lam_init = 0.8 - 0.6 * math.exp(-0.3 * layer)
    vec = lambda: pl.BlockSpec((1, DA_HEAD_DIM), lambda b, h, i: (0, 0))
    return pl.pallas_call(
        functools.partial(_attn_kernel, tq=tq, tk=tk, qs=qs, lam_init=lam_init, seq=seq),
        grid=(batch, DA_HEADS, nq),
        in_specs=[pl.BlockSpec((tq, hw), lambda b, h, i: (b * nq + i, h)),
                  pl.BlockSpec((tq, hw), lambda b, h, i: (b * nq + jnp.minimum(i + 1, nq - 1), h)),
                  pl.BlockSpec((seq, hw), lambda b, h, i: (b, DA_HEADS + h)),
                  pl.BlockSpec((seq, hw), lambda b, h, i: (b, 2 * DA_HEADS + h)),
                  vec(), vec(), vec(), vec(),
                  pl.BlockSpec((hw, 1), lambda b, h, i: (h, 0))],
        out_specs=pl.BlockSpec((tq, hw), lambda b, h, i: (b * nq + i, h)),
        out_shape=jax.ShapeDtypeStruct((T, DA_WIDTH), BF16),
        scratch_shapes=[pltpu.VMEM((seq // tk, hw, tk), BF16), pltpu.VMEM((3, tq, hw), BF16),
                        pltpu.VMEM((tq // qs, tk, qs), F32), pltpu.VMEM((tq // qs, tk, qs), F32),
                        pltpu.VMEM((2, 1, tq), F32), pltpu.VMEM((2, 1, tq), F32),
                        pltpu.VMEM((2, tq // qs, hw, qs), F32)],
        compiler_params=_params("arbitrary", "arbitrary", "arbitrary"),
        name="diff_attn",
    )(att, att, att, att, lq1.reshape(1, -1), lk1.reshape(1, -1), lq2.reshape(1, -1),
      lk2.reshape(1, -1), norm_g.reshape(-1, 1))


def _out_proj_kernel(hg_ref, lru_ref, da_ref, w_ref, x_ref, g_ref, b_ref, o_ref):
    h = jnp.dot(hg_ref[...], w_ref[0:HG_WIDTH, :], preferred_element_type=F32)
    h = h + jnp.dot(lru_ref[...], w_ref[HG_WIDTH:HG_WIDTH + LRU_WIDTH, :], preferred_element_type=F32)
    h = h + jnp.dot(da_ref[...], w_ref[HG_WIDTH + LRU_WIDTH:, :], preferred_element_type=F32)
    o_ref[...] = _layer_norm_rows(DN_ALPHA * x_ref[...] + h, g_ref[...], b_ref[...])


def _out_proj_ln(o_hg, o_lru, o_da, w_bf16, x, g, b, tm=1024):
    T = x.shape[0]
    rows = lambda w: pl.BlockSpec((tm, w), lambda i: (i, 0))
    const = lambda shape: pl.BlockSpec(shape, lambda i: (0, 0))
    return pl.pallas_call(
        _out_proj_kernel,
        grid=(T // tm,),
        in_specs=[rows(HG_WIDTH), rows(LRU_WIDTH), rows(DA_WIDTH), const((D_MODEL, D_MODEL)),
                  rows(D_MODEL), const((1, D_MODEL)), const((1, D_MODEL))],
        out_specs=rows(D_MODEL),
        out_shape=jax.ShapeDtypeStruct((T, D_MODEL), F32),
        compiler_params=_params("parallel"),
        name="out_proj_ln",
    )(o_hg, o_lru, o_da, w_bf16, x, g.reshape(1, -1), b.reshape(1, -1))


def _ffn_kernel(x_ref, wg_ref, wu_ref, wd_ref, g_ref, b_ref, o_ref, xb_ref, acc_ref):
    f = pl.program_id(1)

    @pl.when(f == 0)
    def _():
        xb_ref[...] = x_ref[...].astype(BF16)
        acc_ref[...] = jnp.zeros_like(acc_ref)

    xb = xb_ref[...]
    gate = jnp.dot(xb, wg_ref[...], preferred_element_type=F32)
    up = jnp.dot(xb, wu_ref[...], preferred_element_type=F32)
    hmid = (gate * jax.nn.sigmoid(gate) * up).astype(BF16)
    acc_ref[...] += jnp.dot(hmid, wd_ref[...], preferred_element_type=F32)

    @pl.when(f == pl.num_programs(1) - 1)
    def _():
        o_ref[...] = _layer_norm_rows(DN_ALPHA * x_ref[...] + acc_ref[...], g_ref[...], b_ref[...])


def _ffn_ln(x, wg, wu, wd, g, b, tm=1024, tf=512):
    T = x.shape[0]
    return pl.pallas_call(
        _ffn_kernel,
        grid=(T // tm, D_FF // tf),
        in_specs=[pl.BlockSpec((tm, D_MODEL), lambda i, f: (i, 0)),
                  pl.BlockSpec((D_MODEL, tf), lambda i, f: (0, f)),
                  pl.BlockSpec((D_MODEL, tf), lambda i, f: (0, f)),
                  pl.BlockSpec((tf, D_MODEL), lambda i, f: (f, 0)),
                  pl.BlockSpec((1, D_MODEL), lambda i, f: (0, 0)),
                  pl.BlockSpec((1, D_MODEL), lambda i, f: (0, 0))],
        out_specs=pl.BlockSpec((tm, D_MODEL), lambda i, f: (i, 0)),
        out_shape=jax.ShapeDtypeStruct((T, D_MODEL), F32),
        scratch_shapes=[pltpu.VMEM((tm, D_MODEL), BF16), pltpu.VMEM((tm, D_MODEL), F32)],
        compiler_params=_params("parallel", "arbitrary"),
        name="ffn_ln",
    )(x, wg, wu, wd, g.reshape(1, -1), b.reshape(1, -1))


ROUTER_LANES = 128
META_I1, META_I2, META_R1, META_R2 = 0, 1, 2, 3


def _router_kernel(x_ref, rwh_ref, rwl_ref, tri_ref, meta_ref, gate_ref, cnt_ref, carry_ref):
    @pl.when(pl.program_id(0) == 0)
    def _():
        carry_ref[...] = jnp.zeros_like(carry_ref)

    x = x_ref[...]
    x_hi = x.astype(BF16)
    x_lo = (x - x_hi.astype(F32)).astype(BF16)
    logits = (jnp.dot(x_hi, rwh_ref[...], preferred_element_type=F32)
              + jnp.dot(x_lo, rwh_ref[...], preferred_element_type=F32)
              + jnp.dot(x_hi, rwl_ref[...], preferred_element_type=F32))
    lane = lax.broadcasted_iota(jnp.int32, logits.shape, 1)
    neg = -jnp.inf
    l1 = jnp.where(lane < N_EXPERTS, logits, neg)
    v1 = jnp.max(l1, axis=-1, keepdims=True)
    i1 = jnp.min(jnp.where(l1 == v1, lane, ROUTER_LANES), axis=-1, keepdims=True)
    l2 = jnp.where(lane == i1, neg, l1)
    v2 = jnp.max(l2, axis=-1, keepdims=True)
    i2 = jnp.min(jnp.where(l2 == v2, lane, ROUTER_LANES), axis=-1, keepdims=True)
    e2 = jnp.exp(v2 - v1)
    g1 = 1.0 / (1.0 + e2)
    g2 = e2 / (1.0 + e2)
    member = jnp.where((lane == i1) | (lane == i2), 1.0, 0.0)
    rank = jnp.dot(tri_ref[...], member.astype(BF16), preferred_element_type=F32) + carry_ref[0:1, :]
    r1 = jnp.sum(jnp.where(lane == i1, rank, 0.0), axis=-1, keepdims=True).astype(jnp.int32)
    r2 = jnp.sum(jnp.where(lane == i2, rank, 0.0), axis=-1, keepdims=True).astype(jnp.int32)
    meta = jnp.where(lane == META_I1, i1, 0) + jnp.where(lane == META_I2, i2, 0)
    meta = meta + jnp.where(lane == META_R1, r1, 0) + jnp.where(lane == META_R2, r2, 0)
    meta_ref[...] = meta
    gate_ref[...] = jnp.where(lane == 0, g1, 0.0) + jnp.where(lane == 1, g2, 0.0)
    carry = carry_ref[...] + jnp.sum(member, axis=0, keepdims=True)
    carry_ref[...] = carry
    cnt_ref[...] = carry


def _router(x, router_pad, tm=1024):
    T = x.shape[0]
    tri = (jnp.arange(tm)[:, None] > jnp.arange(tm)[None, :]).astype(BF16)
    rw_hi = router_pad.astype(BF16)
    rw_lo = (router_pad - rw_hi.astype(F32)).astype(BF16)
    return pl.pallas_call(
        _router_kernel,
        grid=(T // tm,),
        in_specs=[pl.BlockSpec((tm, D_MODEL), lambda i: (i, 0)),
                  pl.BlockSpec((D_MODEL, ROUTER_LANES), lambda i: (0, 0)),
                  pl.BlockSpec((D_MODEL, ROUTER_LANES), lambda i: (0, 0)),
                  pl.BlockSpec((tm, tm), lambda i: (0, 0))],
        out_specs=[pl.BlockSpec((tm, ROUTER_LANES), lambda i: (i, 0)),
                   pl.BlockSpec((tm, ROUTER_LANES), lambda i: (i, 0)),
                   pl.BlockSpec((8, ROUTER_LANES), lambda i: (0, 0))],
        out_shape=[jax.ShapeDtypeStruct((T, ROUTER_LANES), jnp.int32),
                   jax.ShapeDtypeStruct((T, ROUTER_LANES), F32),
                   jax.ShapeDtypeStruct((8, ROUTER_LANES), F32)],
        scratch_shapes=[pltpu.VMEM((8, ROUTER_LANES), F32)],
        compiler_params=_params("arbitrary"),
        name="moe_router",
    )(x, rw_hi, rw_lo, tri)


def _dispatch_tables(meta, cnt, tm):
    T = meta.shape[0]
    counts = cnt[0, :N_EXPERTS].astype(jnp.int32)
    padded = ((counts + tm - 1) // tm) * tm
    ends = jnp.cumsum(padded)
    off = ends - padded
    experts = jnp.arange(N_EXPERTS, dtype=jnp.int32)
    group_start = lambda e: jnp.sum(jnp.where(e[:, None] == experts[None, :], off[None, :], 0), axis=1)
    pos1 = group_start(meta[:, META_I1]) + meta[:, META_R1]
    pos2 = group_start(meta[:, META_I2]) + meta[:, META_R2]
    n_tiles = 2 * T // tm + N_EXPERTS + 1
    tok = jnp.arange(T, dtype=jnp.int32)
    code_of = jnp.full((n_tiles * tm,), 2 * T, jnp.int32).at[jnp.concatenate([pos1, pos2])].set(
        jnp.concatenate([tok, tok + T]))
    tile_start = jnp.arange(n_tiles, dtype=jnp.int32) * tm
    tile_expert = jnp.minimum(jnp.sum((tile_start[:, None] >= ends[None, :]).astype(jnp.int32), axis=1),
                              N_EXPERTS - 1)
    n_used = (ends[-1] // tm).astype(jnp.int32).reshape(1)
    row = jnp.arange(n_tiles * tm, dtype=jnp.int32)
    is_pad = code_of >= 2 * T
    src_of = jnp.where(is_pad, 0, jnp.where(code_of >= T, code_of - T, code_of))
    dst_of = jnp.where(is_pad, 2 * T + ((row // tm) % 2) * tm + row % tm, code_of)
    return src_of.reshape(n_tiles, 1, tm), dst_of.reshape(n_tiles, 1, tm), tile_expert, n_used


def _expert_kernel(te_ref, nu_ref, cur_ref, nxt_ref, prv_ref, x_hbm, wg_ref, wu_ref, wd_ref, y_hbm,
                   xbuf_ref, xb_ref, acc_ref, sem_g, sem_s, *, tm, nf, n_tok):
    i = pl.program_id(0)
    f = pl.program_id(1)
    n_used = nu_ref[0]
    slot = i % 2
    other = 1 - slot
    pad_code = 2 * n_tok

    def gather(src_ref, r, buf):
        return pltpu.make_async_copy(x_hbm.at[pl.ds(src_ref[0, 0, r], 1)],
                                     xbuf_ref.at[buf, pl.ds(r, 1)], sem_g.at[buf])

    def scatter(r, buf, to_dump):
        dest = jnp.where(to_dump, pad_code + buf * tm + r, prv_ref[0, 0, r])
        return pltpu.make_async_copy(acc_ref.at[buf, pl.ds(r, 1)], y_hbm.at[pl.ds(dest, 1)], sem_s)

    def wait_gather(buf):
        pltpu.make_async_copy(x_hbm.at[pl.ds(0, tm)], xbuf_ref.at[buf], sem_g.at[buf]).wait()

    def wait_scatter(buf):
        pltpu.make_async_copy(acc_ref.at[buf], y_hbm.at[pl.ds(0, tm)], sem_s).wait()

    @pl.when((i < n_used) & (f == 0))
    def _():
        @pl.when(i == 0)
        def _():
            def issue(r, carry):
                gather(cur_ref, r, 0).start()
                return carry
            lax.fori_loop(0, tm, issue, 0, unroll=8)
            acc_ref[1] = jnp.zeros((tm, D_MODEL), F32)
            clear = pltpu.make_async_copy(acc_ref.at[1], y_hbm.at[pl.ds(pad_code, tm)], sem_s)
            clear.start()
            clear.wait()

        wait_gather(slot)
        xb_ref[...] = xbuf_ref[slot].astype(BF16)

    def tile_step(fs):
        if fs == 0:
            for r in range(tm):
                gather(nxt_ref, r, other).start()
        if fs == nf - 1:
            for r in range(tm):
                scatter(r, other, i == 0).start()
        xb = xb_ref[...]
        gate = jnp.dot(xb, wg_ref[0], preferred_element_type=F32)
        up = jnp.dot(xb, wu_ref[0], preferred_element_type=F32)
        hmid = (gate * jax.nn.sigmoid(gate) * up).astype(BF16)
        down = jnp.dot(hmid, wd_ref[0], preferred_element_type=F32)
        if fs == 0:
            acc_ref[slot] = down
        else:
            acc_ref[slot] += down
        if fs == nf - 1:
            wait_scatter(other)

    for fs in range(nf):
        pl.when((i < n_used) & (f == fs))(functools.partial(tile_step, fs))

    @pl.when((i == n_used) & (f == 0))
    def _():
        wait_gather(slot)

        def issue(r, carry):
            scatter(r, other, False).start()
            return carry
        lax.fori_loop(0, tm, issue, 0, unroll=8)
        wait_scatter(other)


def _experts(x, src_of, dst_of, tile_expert, n_used, wg, wu, wd, tm, tf=1792):
    n_tok = x.shape[0]
    n_tiles = src_of.shape[0]
    nf = D_FF // tf
    fidx = lambda i, f, nu: jnp.where(i < nu[0], f, nf - 1)
    rows_of_tile = lambda shift: pl.BlockSpec(
        (1, 1, tm), lambda i, f, te, nu: (jnp.clip(i + shift, 0, n_tiles - 1), 0, 0),
        memory_space=pltpu.SMEM)
    grid_spec = pltpu.PrefetchScalarGridSpec(
        num_scalar_prefetch=2,
        grid=(n_tiles, nf),
        in_specs=[rows_of_tile(0), rows_of_tile(1), rows_of_tile(-1),
                  pl.BlockSpec(memory_space=pl.ANY),
                  pl.BlockSpec((1, D_MODEL, tf), lambda i, f, te, nu: (te[i], 0, fidx(i, f, nu))),
                  pl.BlockSpec((1, D_MODEL, tf), lambda i, f, te, nu: (te[i], 0, fidx(i, f, nu))),
                  pl.BlockSpec((1, tf, D_MODEL), lambda i, f, te, nu: (te[i], fidx(i, f, nu), 0))],
        out_specs=pl.BlockSpec(memory_space=pl.ANY),
        scratch_shapes=[pltpu.VMEM((2, tm, D_MODEL), F32), pltpu.VMEM((tm, D_MODEL), BF16),
                        pltpu.VMEM((2, tm, D_MODEL), F32),
                        pltpu.SemaphoreType.DMA((2,)), pltpu.SemaphoreType.DMA(())],
    )
    return pl.pallas_call(
        functools.partial(_expert_kernel, tm=tm, nf=nf, n_tok=n_tok),
        grid_spec=grid_spec,
        out_shape=jax.ShapeDtypeStruct((2 * n_tok + 2 * tm, D_MODEL), F32),
        compiler_params=_params("arbitrary", "arbitrary"),
        name="moe_experts",
    )(tile_expert, n_used, src_of, src_of, dst_of, x, wg, wu, wd)


def _combine_kernel(x_ref, y1_ref, y2_ref, gate_ref, g_ref, b_ref, o_ref):
    gates = gate_ref[...]
    y = gates[:, 0:1] * y1_ref[...] + gates[:, 1:2] * y2_ref[...]
    o_ref[...] = _layer_norm_rows(DN_ALPHA * x_ref[...] + y, g_ref[...], b_ref[...])


def _combine_ln(x, y, gates, g, b, tm=1024):
    T = x.shape[0]
    nt = T // tm
    return pl.pallas_call(
        _combine_kernel,
        grid=(nt,),
        in_specs=[pl.BlockSpec((tm, D_MODEL), lambda i: (i, 0)),
                  pl.BlockSpec((tm, D_MODEL), lambda i: (i, 0)),
                  pl.BlockSpec((tm, D_MODEL), lambda i: (nt + i, 0)),
                  pl.BlockSpec((tm, ROUTER_LANES), lambda i: (i, 0)),
                  pl.BlockSpec((1, D_MODEL), lambda i: (0, 0)),
                  pl.BlockSpec((1, D_MODEL), lambda i: (0, 0))],
        out_specs=pl.BlockSpec((tm, D_MODEL), lambda i: (i, 0)),
        out_shape=jax.ShapeDtypeStruct((T, D_MODEL), F32),
        compiler_params=_params("parallel"),
        name="moe_combine_ln",
    )(x, y, y, gates, g.reshape(1, -1), b.reshape(1, -1))


def _moe_ln(x, router_w, wg, wu, wd, g, b, tm=512):
    router_pad = jnp.pad(router_w, ((0, 0), (0, ROUTER_LANES - N_EXPERTS)))
    meta, gates, cnt = _router(x, router_pad)
    src_of, dst_of, tile_expert, n_used = _dispatch_tables(meta, cnt, tm)
    y = _experts(x, src_of, dst_of, tile_expert, n_used, wg, wu, wd, tm)
    return _combine_ln(x, y, gates, g, b)


def _block_diag(w):
    nb, n, _ = w.shape
    eye = jnp.eye(nb, dtype=w.dtype)
    return (eye[:, None, :, None] * w[:, :, None, :]).reshape(nb * n, nb * n)


def kernel(x, w_in, w_out, hg_lb_logits, hg_norm_g, lru_conv_w, lru_conv_b, lru_wa, lru_ba, lru_wx, lru_bx, lru_lambda, da_lq1, da_lk1, da_lq2, da_lk2, da_norm_g, ln1_g, ln1_b, ln2_g, ln2_b, ffn_wg, ffn_wu, ffn_wd, router_w, moe_wg, moe_wu, moe_wd):
    batch, seq, _ = x.shape
    depth = w_in.shape[0]
    xf = x.reshape(batch * seq, D_MODEL)
    lb_p = jax.nn.softmax(hg_lb_logits.astype(F32), axis=0)
    lb_all = jnp.cumsum(lb_p, axis=0) - lb_p[0:1]
    for l in range(depth):
        rec, att = _in_proj(xf, w_in[l].astype(BF16))
        o_hg = _hgrn(rec, lb_all[l], hg_norm_g[l], batch, seq)
        o_lru = _lru(rec, lru_conv_w[l], lru_conv_b[l], _block_diag(lru_wa[l]).astype(BF16),
                     _block_diag(lru_wx[l]).astype(BF16), lru_ba[l], lru_bx[l], lru_lambda[l],
                     batch, seq)
        o_da = _attention(att, da_lq1[l], da_lk1[l], da_lq2[l], da_lk2[l], da_norm_g[l], l,
                          batch, seq)
        xf = _out_proj_ln(o_hg, o_lru, o_da, w_out[l].astype(BF16), xf, ln1_g[l], ln1_b[l])
        j = l // 2
        if l % 2 == 0:
            xf = _ffn_ln(xf, ffn_wg[j].astype(BF16), ffn_wu[j].astype(BF16), ffn_wd[j].astype(BF16),
                         ln2_g[l], ln2_b[l])
        else:
            xf = _moe_ln(xf, router_w[j], moe_wg[j].astype(BF16), moe_wu[j].astype(BF16),
                         moe_wd[j].astype(BF16), ln2_g[l], ln2_b[l])
    return xf.reshape(batch, seq, D_MODEL)
```

```python
import functools
import math

import jax
import jax.numpy as jnp
from jax import lax
from jax.experimental import pallas as pl
from jax.experimental.pallas import tpu as pltpu

D_MODEL = 1024
DEPTH = 2
HG_HEADS = 4
HG_KEY_DIM = 64
HG_WIDTH = 256
HG_CHUNK = 16
LRU_WIDTH = 256
LRU_BLOCKS = 4
CONV_WIDTH = 4
LRU_C = 8.0
DA_WIDTH = 512
DA_HEADS = 4
DA_HEAD_DIM = 64
D_FF = 3584
N_EXPERTS = 8
DN_ALPHA = (2.0 * DEPTH) ** 0.25
EPS = 1e-5
REC_WIDTH = 4 * HG_WIDTH + 2 * LRU_WIDTH
ATT_WIDTH = 3 * DA_WIDTH

V7X_VMEM_BYTES = 64 * 1024 * 1024
VMEM_LIMIT = V7X_VMEM_BYTES * 3 // 4

BF16 = jnp.bfloat16
F32 = jnp.float32


def _params(*semantics):
    return pltpu.CompilerParams(dimension_semantics=semantics, vmem_limit_bytes=VMEM_LIMIT)


def _layer_norm_rows(y, g, b):
    mu = jnp.mean(y, axis=-1, keepdims=True)
    yc = y - mu
    var = jnp.mean(yc * yc, axis=-1, keepdims=True)
    return yc * lax.rsqrt(var + EPS) * g + b


Q_SCALE = DA_HEAD_DIM ** -0.5 * math.log2(math.e)


def _in_proj_kernel(x_ref, w_ref, rec_ref, att_ref):
    xb = x_ref[...].astype(BF16)
    rec_ref[...] = jnp.dot(xb, w_ref[:, :REC_WIDTH], preferred_element_type=F32)
    q = jnp.dot(xb, w_ref[:, REC_WIDTH:REC_WIDTH + DA_WIDTH], preferred_element_type=F32)
    att_ref[:, :DA_WIDTH] = (q * Q_SCALE).astype(BF16)
    att_ref[:, DA_WIDTH:] = jnp.dot(xb, w_ref[:, REC_WIDTH + DA_WIDTH:],
                                    preferred_element_type=F32).astype(BF16)


def _in_proj(x, w_bf16, tm=512):
    T = x.shape[0]
    return pl.pallas_call(
        _in_proj_kernel,
        grid=(T // tm,),
        in_specs=[pl.BlockSpec((tm, D_MODEL), lambda i: (i, 0)),
                  pl.BlockSpec((D_MODEL, REC_WIDTH + ATT_WIDTH), lambda i: (0, 0))],
        out_specs=[pl.BlockSpec((tm, REC_WIDTH), lambda i: (i, 0)),
                   pl.BlockSpec((tm, ATT_WIDTH), lambda i: (i, 0))],
        out_shape=[jax.ShapeDtypeStruct((T, REC_WIDTH), F32),
                   jax.ShapeDtypeStruct((T, ATT_WIDTH), BF16)],
        compiler_params=_params("parallel"),
        name="in_proj",
    )(x, w_bf16)


def _hgrn_kernel(rec_ref, lb_ref, ng_ref, gmat_ref, sel_ref, bmask_ref, o_ref,
                 st_ref, q_s, kk_s, b_s, bl_s, qd_s, kd_s, o_s, *, rows):
    @pl.when(pl.program_id(1) == 0)
    def _():
        st_ref[...] = jnp.zeros_like(st_ref)

    C = HG_CHUNK
    W = HG_WIDTH
    nc = rows // C
    lb = lb_ref[...]
    gmat = gmat_ref[...]
    sel = sel_ref[...]

    qr = rec_ref[:, 0:W]
    z = rec_ref[:, W:2 * W]
    rin = lax.broadcasted_iota(jnp.int32, (rows, W), 0) % C
    logf = jnp.log(lb + (1.0 - lb) * jax.nn.sigmoid(z))
    kk = (1.0 - lb) * jax.nn.sigmoid(-z)
    q = qr * jax.nn.sigmoid(qr)
    b = logf
    for s in (1, 2, 4, 8):
        b = b + jnp.where(rin >= s, pltpu.roll(b, s, 0), 0.0)
    r = jnp.where(rin < C - 1, pltpu.roll(logf, rows - 1, 0), 0.0)
    for s in (1, 2, 4, 8):
        r = r + jnp.where(rin < C - s, pltpu.roll(r, rows - s, 0), 0.0)
    q_s[...] = q
    kk_s[...] = kk
    b_s[...] = b
    bl_s[...] = b + r
    qd_s[...] = (q * jnp.exp(b)).astype(BF16)
    kd_s[...] = (kk * jnp.exp(r)).astype(BF16)

    row = lax.broadcasted_iota(jnp.int32, (C, W), 0)

    def chunk(c, carry):
        r0 = pl.multiple_of(c * C, C)
        q = q_s[pl.ds(r0, C), :]
        kk = kk_s[pl.ds(r0, C), :]
        b = b_s[pl.ds(r0, C), :]
        v = rec_ref[pl.ds(r0, C), 2 * W:3 * W]
        w_rows = []
        for t in range(C):
            rel = b[t:t + 1, :] - b
            dec = jnp.exp(jnp.where(row <= t, rel, -jnp.inf))
            w_rows.append(((q[t:t + 1, :] * kk) * dec).astype(BF16))
        w2 = jnp.concatenate(w_rows, axis=0)
        a = jnp.dot(w2, gmat, preferred_element_type=F32)
        p = a * jnp.concatenate([v] * C, axis=0)
        o_intra = jnp.dot(sel, p.astype(BF16), preferred_element_type=F32)
        u_t = lax.dot_general(v.astype(BF16), kd_s[pl.ds(r0, C), :], (((0,), (0,)), ((), ())),
                              preferred_element_type=F32)
        st = st_ref[...]
        o_inter = lax.dot_general(qd_s[pl.ds(r0, C), :], st.astype(BF16), (((1,), (1,)), ((), ())),
                                  preferred_element_type=F32)
        o_s[pl.ds(r0, C), :] = o_intra + o_inter
        st_ref[...] = st * jnp.exp(bl_s[pl.ds(r0, 1), :]) + u_t * bmask_ref[...]
        return carry

    lax.fori_loop(0, nc, chunk, 0, unroll=16)

    o = o_s[...]
    g = rec_ref[:, 3 * W:4 * W]
    sq = o * o
    sq_hi = sq.astype(BF16)
    sq_lo = (sq - sq_hi.astype(F32)).astype(BF16)
    ms = (jnp.dot(sq_hi, gmat, preferred_element_type=F32)
          + jnp.dot(sq_lo, gmat, preferred_element_type=F32)) * (1.0 / HG_KEY_DIM)
    out = o * lax.rsqrt(ms + EPS) * ng_ref[...] * (g * jax.nn.sigmoid(g))
    o_ref[...] = out.astype(o_ref.dtype)


def _hgrn(rec, lb, norm_g, batch, seq, rows=512):
    T = rec.shape[0]
    nblk = seq // rows
    head = jnp.arange(HG_WIDTH) // HG_KEY_DIM
    same_head = head[:, None] == head[None, :]
    gmat = same_head.astype(BF16)
    bmask = same_head.astype(F32)
    sel = (jnp.arange(HG_CHUNK)[:, None] == (jnp.arange(HG_CHUNK * HG_CHUNK) // HG_CHUNK)[None, :]).astype(BF16)
    const = lambda shape: pl.BlockSpec(shape, lambda b, i: (0, 0))
    return pl.pallas_call(
        functools.partial(_hgrn_kernel, rows=rows),
        grid=(batch, nblk),
        in_specs=[pl.BlockSpec((rows, 4 * HG_WIDTH), lambda b, i: (b * nblk + i, 0)),
                  const((1, HG_WIDTH)), const((1, HG_WIDTH)),
                  const((HG_WIDTH, HG_WIDTH)), const((HG_CHUNK, HG_CHUNK * HG_CHUNK)),
                  const((HG_WIDTH, HG_WIDTH))],
        out_specs=pl.BlockSpec((rows, HG_WIDTH), lambda b, i: (b * nblk + i, 0)),
        out_shape=jax.ShapeDtypeStruct((T, HG_WIDTH), BF16),
        scratch_shapes=[pltpu.VMEM((HG_WIDTH, HG_WIDTH), F32)]
        + [pltpu.VMEM((rows, HG_WIDTH), F32)] * 4
        + [pltpu.VMEM((rows, HG_WIDTH), BF16)] * 2
        + [pltpu.VMEM((rows, HG_WIDTH), F32)],
        compiler_params=_params("parallel", "arbitrary"),
        name="hgrn2",
    )(rec, lb.reshape(1, -1), norm_g.reshape(1, -1), gmat, sel, bmask)


def _lru_kernel(rec_ref, cw_ref, cb_ref, wa_ref, wx_ref, ba_ref, bx_ref, lam_ref, o_ref,
                xprev_ref, h_ref, *, rows):
    @pl.when(pl.program_id(1) == 0)
    def _():
        xprev_ref[...] = jnp.zeros_like(xprev_ref)
        h_ref[...] = jnp.zeros_like(h_ref)

    W = LRU_WIDTH
    x = rec_ref[:, 0:W]
    gate = rec_ref[:, W:2 * W]
    row = lax.broadcasted_iota(jnp.int32, (rows, W), 0)
    xp = xprev_ref[...]
    tail = jnp.zeros((rows - 8, W), F32)
    xc = cb_ref[...] + cw_ref[CONV_WIDTH - 1:CONV_WIDTH, :] * x
    for j in range(1, CONV_WIDTH):
        prev = jnp.concatenate([pltpu.roll(xp, j, 0), tail], axis=0)
        xs = jnp.where(row >= j, pltpu.roll(x, j, 0), prev)
        xc = xc + cw_ref[CONV_WIDTH - 1 - j:CONV_WIDTH - j, :] * xs
    xprev_ref[...] = x[rows - 8:rows, :]

    xcb = xc.astype(BF16)
    r = jax.nn.sigmoid(jnp.dot(xcb, wa_ref[...], preferred_element_type=F32) + ba_ref[...])
    i = jax.nn.sigmoid(jnp.dot(xcb, wx_ref[...], preferred_element_type=F32) + bx_ref[...])
    lam = lam_ref[...]
    log_sig = jnp.minimum(lam, 0.0) - jnp.log1p(jnp.exp(-jnp.abs(lam)))
    log_a = LRU_C * r * log_sig
    a = jnp.exp(log_a)
    th = jnp.tanh(log_a)
    one_minus_a2 = -2.0 * th / (1.0 - th)
    u = jnp.sqrt(one_minus_a2) * i * xc

    for s in (1, 2, 4):
        a_s = jnp.where(row >= s, pltpu.roll(a, s, 0), 1.0)
        u_s = jnp.where(row >= s, pltpu.roll(u, s, 0), 0.0)
        u = a * u_s + u
        a = a * a_s
    s = 8
    while s < rows:
        a_s = jnp.concatenate([jnp.ones((s, W), F32), a[:rows - s, :]], axis=0)
        u_s = jnp.concatenate([jnp.zeros((s, W), F32), u[:rows - s, :]], axis=0)
        u = a * u_s + u
        a = a * a_s
        s *= 2
    h = a * h_ref[0:1, :] + u
    h_ref[...] = jnp.broadcast_to(h[rows - 1:rows, :], h_ref.shape)
    o_ref[...] = (h * jax.nn.gelu(gate)).astype(o_ref.dtype)


def _lru(rec, conv_w, conv_b, wa_bd, wx_bd, ba, bx, lam, batch, seq, rows=512):
    T = rec.shape[0]
    nblk = seq // rows
    W = LRU_WIDTH
    const = lambda shape: pl.BlockSpec(shape, lambda b, i: (0, 0))
    return pl.pallas_call(
        functools.partial(_lru_kernel, rows=rows),
        grid=(batch, nblk),
        in_specs=[pl.BlockSpec((rows, 2 * W), lambda b, i: (b * nblk + i, 2)),
                  const((CONV_WIDTH, W)), const((1, W)), const((W, W)), const((W, W)),
                  const((1, W)), const((1, W)), const((1, W))],
        out_specs=pl.BlockSpec((rows, W), lambda b, i: (b * nblk + i, 0)),
        out_shape=jax.ShapeDtypeStruct((T, W), BF16),
        scratch_shapes=[pltpu.VMEM((8, W), F32), pltpu.VMEM((8, W), F32)],
        compiler_params=_params("parallel", "arbitrary"),
        name="rglru",
    )(rec, conv_w, conv_b.reshape(1, -1), wa_bd, wx_bd, ba.reshape(1, -1), bx.reshape(1, -1),
      lam.reshape(1, -1))


def _attn_kernel(q_ref, qn_ref, k_ref, v_ref, lq1_ref, lk1_ref, lq2_ref, lk2_ref, ng_ref, o_ref,
                 vt_ref, qm_ref, s0_ref, s1_ref, m_ref, l_ref, acc_ref, *, tq, tk, qs, lam_init, seq):
    d = DA_HEAD_DIM
    qi = pl.program_id(2)

    @pl.when(qi == 0)
    def _():
        def transpose_values(j, carry):
            r0 = pl.multiple_of(j * tk, tk)
            vt_ref[j] = v_ref[pl.ds(r0, tk), :].astype(F32).T.astype(BF16)
            return carry
        lax.fori_loop(0, seq // tk, transpose_values, 0)

    q = q_ref[...]
    lane = lax.broadcasted_iota(jnp.int32, q.shape, 1)
    qm_ref[0] = jnp.where(lane < d, q, jnp.zeros_like(q))
    qm_ref[1] = jnp.where(lane >= d, q, jnp.zeros_like(q))
    m_ref[...] = jnp.full_like(m_ref, -jnp.inf)
    l_ref[...] = jnp.zeros_like(l_ref)
    acc_ref[...] = jnp.zeros_like(acc_ref)
    s_refs = (s0_ref, s1_ref)

    def scores(j, c, queries=None):
        k0 = pl.multiple_of(j * tk, tk)
        kb = k_ref[pl.ds(k0, tk), :]
        qsrc = c if queries is None else queries
        for st in range(tq // qs):
            s_refs[c][st] = lax.dot_general(kb, qm_ref[qsrc, st * qs:(st + 1) * qs, :],
                                            (((1,), (1,)), ((), ())),
                                            preferred_element_type=F32)

    def softmax_pv(j, c, masked):
        for st in range(tq // qs):
            cols = pl.ds(st * qs, qs)
            nk = min((st + 1) * qs, tk) if masked else tk
            vt = vt_ref[j, :, 0:nk]
            s = s_refs[c][st, 0:nk, :]
            if masked:
                kpos = lax.broadcasted_iota(jnp.int32, (nk, qs), 0)
                qpos = st * qs + lax.broadcasted_iota(jnp.int32, (nk, qs), 1)
                s = jnp.where(kpos <= qpos, s, -jnp.inf)
            m_old = m_ref[c, :, cols]
            m_new = jnp.maximum(m_old, jnp.max(s, axis=0, keepdims=True))
            p = jnp.exp2(s - m_new)
            alpha = jnp.exp2(m_old - m_new)
            l_ref[c, :, cols] = alpha * l_ref[c, :, cols] + jnp.sum(p, axis=0, keepdims=True)
            acc_ref[c, st] = (alpha * acc_ref[c, st]
                                   + jnp.dot(vt, p.astype(BF16), preferred_element_type=F32))
            m_ref[c, :, cols] = m_new

    def full_block(j, carry):
        scores(j, 1)
        softmax_pv(j, 0, False)
        scores(j + 1, 0)
        softmax_pv(j, 1, False)
        return carry

    def four_full_blocks(jj, carry):
        for u in range(4):
            full_block(4 * jj + u, carry)
        return carry

    @pl.when(qi == 0)
    def _():
        scores(0, 0)

    lax.fori_loop(0, qi // 4, four_full_blocks, 0)

    @pl.when(qi % 4 >= 2)
    def _():
        full_block((qi // 4) * 4, 0)
        full_block((qi // 4) * 4 + 1, 0)

    @pl.when(qi % 2 == 1)
    def _():
        full_block(qi - 1, 0)

    scores(qi, 1)
    softmax_pv(qi, 0, True)
    qn = qn_ref[...]
    qm_ref[2] = jnp.where(lane < d, qn, jnp.zeros_like(qn))
    scores(0, 0, queries=2)
    softmax_pv(qi, 1, True)

    lam =(jnp.exp(jnp.sum(lq1_ref[...] * lk1_ref[...], axis=-1, keepdims=True))
           - jnp.exp(jnp.sum(lq2_ref[...] * lk2_ref[...], axis=-1, keepdims=True)) + lam_init)
    acc = [jnp.concatenate([acc_ref[c, st] for st in range(tq // qs)], axis=1) for c in range(2)]
    o_t = acc[0] * (1.0 / l_ref[0]) - lam * (acc[1] * (1.0 / l_ref[1]))
    ms = jnp.mean(o_t * o_t, axis=0, keepdims=True)
    o_t = o_t * lax.rsqrt(ms + EPS) * ng_ref[...] * (1.0 - lam_init)
    o_ref[...] = o_t.T.astype(o_ref.dtype)


def _attention(att, lq1, lk1, lq2, lk2, norm_g, layer, batch, seq, tq=512, qs=256):
    T = att.shape[0]
    tk = tq
    nq = seq // tq
    hw = 2 * DA_HEAD_DIM
    lam_init = 0.8 - 0.6 * math.exp(-0.3 * layer)
    vec = lambda: pl.BlockSpec((1, DA_HEAD_DIM), lambda b, h, i: (0, 0))
    return pl.pallas_call(
        functools.partial(_attn_kernel, tq=tq, tk=tk, qs=qs, lam_init=lam_init, seq=seq),
        grid=(batch, DA_HEADS, nq),
        in_specs=[pl.BlockSpec((tq, hw), lambda b, h, i: (b * nq + i, h)),
                  pl.BlockSpec((tq, hw), lambda b, h, i: (b * nq + jnp.minimum(i + 1, nq - 1), h)),
                  pl.BlockSpec((seq, hw), lambda b, h, i: (b, DA_HEADS + h)),
                  pl.BlockSpec((seq, hw), lambda b, h, i: (b, 2 * DA_HEADS + h)),
                  vec(), vec(), vec(), vec(),
                  pl.BlockSpec((hw, 1), lambda b, h, i: (h, 0))],
        out_specs=pl.BlockSpec((tq, hw), lambda b, h, i: (b * nq + i, h)),
        out_shape=jax.ShapeDtypeStruct((T, DA_WIDTH), BF16),
        scratch_shapes=[pltpu.VMEM((seq // tk, hw, tk), BF16), pltpu.VMEM((3, tq, hw), BF16),
                        pltpu.VMEM((tq // qs, tk, qs), F32), pltpu.VMEM((tq // qs, tk, qs), F32),
                        pltpu.VMEM((2, 1, tq), F32), pltpu.VMEM((2, 1, tq), F32),
                        pltpu.VMEM((2, tq // qs, hw, qs), F32)],
        compiler_params=_params("arbitrary", "arbitrary", "arbitrary"),
        name="diff_attn",
    )(att, att, att, att, lq1.reshape(1, -1), lk1.reshape(1, -1), lq2.reshape(1, -1),
      lk2.reshape(1, -1), norm_g.reshape(-1, 1))


def _out_proj_kernel(hg_ref, lru_ref, da_ref, w_ref, x_ref, g_ref, b_ref, o_ref):
    h = jnp.dot(hg_ref[...], w_ref[0:HG_WIDTH, :], preferred_element_type=F32)
    h = h + jnp.dot(lru_ref[...], w_ref[HG_WIDTH:HG_WIDTH + LRU_WIDTH, :], preferred_element_type=F32)
    h = h + jnp.dot(da_ref[...], w_ref[HG_WIDTH + LRU_WIDTH:, :], preferred_element_type=F32)
    o_ref[...] = _layer_norm_rows(DN_ALPHA * x_ref[...] + h, g_ref[...], b_ref[...])


def _out_proj_ln(o_hg, o_lru, o_da, w_bf16, x, g, b, tm=1024):
    T = x.shape[0]
    rows = lambda w: pl.BlockSpec((tm, w), lambda i: (i, 0))
    const = lambda shape: pl.BlockSpec(shape, lambda i: (0, 0))
    return pl.pallas_call(
        _out_proj_kernel,
        grid=(T // tm,),
        in_specs=[rows(HG_WIDTH), rows(LRU_WIDTH), rows(DA_WIDTH), const((D_MODEL, D_MODEL)),
                  rows(D_MODEL), const((1, D_MODEL)), const((1, D_MODEL))],
        out_specs=rows(D_MODEL),
        out_shape=jax.ShapeDtypeStruct((T, D_MODEL), F32),
        compiler_params=_params("parallel"),
        name="out_proj_ln",
    )(o_hg, o_lru, o_da, w_bf16, x, g.reshape(1, -1), b.reshape(1, -1))


def _ffn_kernel(x_ref, wg_ref, wu_ref, wd_ref, g_ref, b_ref, o_ref, xb_ref, acc_ref):
    f = pl.program_id(1)
    nf = pl.num_programs(1)

    def step(first, last):
        if first:
            xb_ref[...] = x_ref[...].astype(BF16)
        xb = xb_ref[...]
        gate = jnp.dot(xb, wg_ref[...], preferred_element_type=F32)
        up = jnp.dot(xb, wu_ref[...], preferred_element_type=F32)
        hmid = (gate * jax.nn.sigmoid(gate) * up).astype(BF16)
        down = jnp.dot(hmid, wd_ref[...], preferred_element_type=F32)
        if first:
            acc_ref[...] = down
        elif last:
            o_ref[...] = _layer_norm_rows(DN_ALPHA * x_ref[...] + (acc_ref[...] + down),
                                          g_ref[...], b_ref[...])
        else:
            acc_ref[...] += down

    pl.when(f == 0)(functools.partial(step, True, False))
    pl.when((f > 0) & (f < nf - 1))(functools.partial(step, False, False))
    pl.when(f == nf - 1)(functools.partial(step, False, True))


def _ffn_ln(x, wg, wu, wd, g, b, tm=1024, tf=512):
    T = x.shape[0]
    return pl.pallas_call(
        _ffn_kernel,
        grid=(T // tm, D_FF // tf),
        in_specs=[pl.BlockSpec((tm, D_MODEL), lambda i, f: (i, 0)),
                  pl.BlockSpec((D_MODEL, tf), lambda i, f: (0, f)),
                  pl.BlockSpec((D_MODEL, tf), lambda i, f: (0, f)),
                  pl.BlockSpec((tf, D_MODEL), lambda i, f: (f, 0)),
                  pl.BlockSpec((1, D_MODEL), lambda i, f: (0, 0)),
                  pl.BlockSpec((1, D_MODEL), lambda i, f: (0, 0))],
        out_specs=pl.BlockSpec((tm, D_MODEL), lambda i, f: (i, 0)),
        out_shape=jax.ShapeDtypeStruct((T, D_MODEL), F32),
        scratch_shapes=[pltpu.VMEM((tm, D_MODEL), BF16), pltpu.VMEM((tm, D_MODEL), F32)],
        compiler_params=_params("parallel", "arbitrary"),
        name="ffn_ln",
    )(x, wg, wu, wd, g.reshape(1, -1), b.reshape(1, -1))


ROUTER_LANES = 128
META_I1, META_I2, META_R1, META_R2 = 0, 1, 2, 3


def _router_kernel(x_ref, rwh_ref, rwl_ref, tri_ref, meta_ref, gate_ref, cnt_ref, carry_ref):
    @pl.when(pl.program_id(0) == 0)
    def _():
        carry_ref[...] = jnp.zeros_like(carry_ref)

    x = x_ref[...]
    x_hi = x.astype(BF16)
    x_lo = (x - x_hi.astype(F32)).astype(BF16)
    logits = (jnp.dot(x_hi, rwh_ref[...], preferred_element_type=F32)
              + jnp.dot(x_lo, rwh_ref[...], preferred_element_type=F32)
              + jnp.dot(x_hi, rwl_ref[...], preferred_element_type=F32))
    lane = lax.broadcasted_iota(jnp.int32, logits.shape, 1)
    neg = -jnp.inf
    l1 = jnp.where(lane < N_EXPERTS, logits, neg)
    v1 = jnp.max(l1, axis=-1, keepdims=True)
    i1 = jnp.min(jnp.where(l1 == v1, lane, ROUTER_LANES), axis=-1, keepdims=True)
    l2 = jnp.where(lane == i1, neg, l1)
    v2 = jnp.max(l2, axis=-1, keepdims=True)
    i2 = jnp.min(jnp.where(l2 == v2, lane, ROUTER_LANES), axis=-1, keepdims=True)
    e2 = jnp.exp(v2 - v1)
    g1 = 1.0 / (1.0 + e2)
    g2 = e2 / (1.0 + e2)
    member = jnp.where((lane == i1) | (lane == i2), 1.0, 0.0)
    rank = jnp.dot(tri_ref[...], member.astype(BF16), preferred_element_type=F32) + carry_ref[0:1, :]
    r1 = jnp.sum(jnp.where(lane == i1, rank, 0.0), axis=-1, keepdims=True).astype(jnp.int32)
    r2 = jnp.sum(jnp.where(lane == i2, rank, 0.0), axis=-1, keepdims=True).astype(jnp.int32)
    meta = jnp.where(lane == META_I1, i1, 0) + jnp.where(lane == META_I2, i2, 0)
    meta = meta + jnp.where(lane == META_R1, r1, 0) + jnp.where(lane == META_R2, r2, 0)
    meta_ref[...] = meta
    gate_ref[...] = jnp.where(lane == 0, g1, 0.0) + jnp.where(lane == 1, g2, 0.0)
    carry = carry_ref[...] + jnp.sum(member, axis=0, keepdims=True)
    carry_ref[...] = carry
    cnt_ref[...] = carry


def _router(x, router_pad, tm=1024):
    T = x.shape[0]
    tri = (jnp.arange(tm)[:, None] > jnp.arange(tm)[None, :]).astype(BF16)
    rw_hi = router_pad.astype(BF16)
    rw_lo = (router_pad - rw_hi.astype(F32)).astype(BF16)
    return pl.pallas_call(
        _router_kernel,
        grid=(T // tm,),
        in_specs=[pl.BlockSpec((tm, D_MODEL), lambda i: (i, 0)),
                  pl.BlockSpec((D_MODEL, ROUTER_LANES), lambda i: (0, 0)),
                  pl.BlockSpec((D_MODEL, ROUTER_LANES), lambda i: (0, 0)),
                  pl.BlockSpec((tm, tm), lambda i: (0, 0))],
        out_specs=[pl.BlockSpec((tm, ROUTER_LANES), lambda i: (i, 0)),
                   pl.BlockSpec((tm, ROUTER_LANES), lambda i: (i, 0)),
                   pl.BlockSpec((8, ROUTER_LANES), lambda i: (0, 0))],
        out_shape=[jax.ShapeDtypeStruct((T, ROUTER_LANES), jnp.int32),
                   jax.ShapeDtypeStruct((T, ROUTER_LANES), F32),
                   jax.ShapeDtypeStruct((8, ROUTER_LANES), F32)],
        scratch_shapes=[pltpu.VMEM((8, ROUTER_LANES), F32)],
        compiler_params=_params("arbitrary"),
        name="moe_router",
    )(x, rw_hi, rw_lo, tri)


def _dispatch_tables(meta, cnt, tm):
    T = meta.shape[0]
    counts = cnt[0, :N_EXPERTS].astype(jnp.int32)
    padded = ((counts + tm - 1) // tm) * tm
    ends = jnp.cumsum(padded)
    off = ends - padded
    experts = jnp.arange(N_EXPERTS, dtype=jnp.int32)
    group_start = lambda e: jnp.sum(jnp.where(e[:, None] == experts[None, :], off[None, :], 0), axis=1)
    pos1 = group_start(meta[:, META_I1]) + meta[:, META_R1]
    pos2 = group_start(meta[:, META_I2]) + meta[:, META_R2]
    n_tiles = 2 * T // tm + N_EXPERTS + 1
    tok = jnp.arange(T, dtype=jnp.int32)
    code_of = jnp.full((n_tiles * tm,), 2 * T, jnp.int32).at[jnp.concatenate([pos1, pos2])].set(
        jnp.concatenate([tok, tok + T]))
    tile_start = jnp.arange(n_tiles, dtype=jnp.int32) * tm
    tile_expert = jnp.minimum(jnp.sum((tile_start[:, None] >= ends[None, :]).astype(jnp.int32), axis=1),
                              N_EXPERTS - 1)
    n_used = (ends[-1] // tm).astype(jnp.int32).reshape(1)
    row = jnp.arange(n_tiles * tm, dtype=jnp.int32)
    is_pad = code_of >= 2 * T
    src_of = jnp.where(is_pad, 0, jnp.where(code_of >= T, code_of - T, code_of))
    dst_of = jnp.where(is_pad, 2 * T + ((row // tm) % 2) * tm + row % tm, code_of)
    return src_of.reshape(n_tiles, 1, tm), dst_of.reshape(n_tiles, 1, tm), tile_expert, n_used


def _expert_kernel(te_ref, nu_ref, cur_ref, nxt_ref, prv_ref, x_hbm, wg_ref, wu_ref, wd_ref, y_hbm,
                   xbuf_ref, xb_ref, acc_ref, sem_g, sem_s, *, tm, nf, n_tok):
    i = pl.program_id(0)
    f = pl.program_id(1)
    n_used = nu_ref[0]
    slot = i % 2
    other = 1 - slot
    pad_code = 2 * n_tok

    def gather(src_ref, r, buf):
        return pltpu.make_async_copy(x_hbm.at[pl.ds(src_ref[0, 0, r], 1)],
                                     xbuf_ref.at[buf, pl.ds(r, 1)], sem_g.at[buf])

    def scatter(r, buf, to_dump):
        dest = jnp.where(to_dump, pad_code + buf * tm + r, prv_ref[0, 0, r])
        return pltpu.make_async_copy(acc_ref.at[buf, pl.ds(r, 1)], y_hbm.at[pl.ds(dest, 1)], sem_s)

    def wait_gather(buf):
        pltpu.make_async_copy(x_hbm.at[pl.ds(0, tm)], xbuf_ref.at[buf], sem_g.at[buf]).wait()

    def wait_scatter(buf):
        pltpu.make_async_copy(acc_ref.at[buf], y_hbm.at[pl.ds(0, tm)], sem_s).wait()

    @pl.when((i < n_used) & (f == 0))
    def _():
        @pl.when(i == 0)
        def _():
            def issue(r, carry):
                gather(cur_ref, r, 0).start()
                return carry
            lax.fori_loop(0, tm, issue, 0, unroll=8)
            acc_ref[1] = jnp.zeros((tm, D_MODEL), F32)
            clear = pltpu.make_async_copy(acc_ref.at[1], y_hbm.at[pl.ds(pad_code, tm)], sem_s)
            clear.start()
            clear.wait()

        wait_gather(slot)
        xb_ref[...] = xbuf_ref[slot].astype(BF16)

    def tile_step(fs):
        if fs == 0:
            for r in range(tm):
                gather(nxt_ref, r, other).start()
        if fs == nf - 1:
            for r in range(tm):
                scatter(r, other, i == 0).start()
        xb = xb_ref[...]
        gate = jnp.dot(xb, wg_ref[0], preferred_element_type=F32)
        up = jnp.dot(xb, wu_ref[0], preferred_element_type=F32)
        hmid = (gate * jax.nn.sigmoid(gate) * up).astype(BF16)
        down = jnp.dot(hmid, wd_ref[0], preferred_element_type=F32)
        if fs == 0:
            acc_ref[slot] = down
        else:
            acc_ref[slot] += down
        if fs == nf - 1:
            wait_scatter(other)

    for fs in range(nf):
        pl.when((i < n_used) & (f == fs))(functools.partial(tile_step, fs))

    @pl.when((i == n_used) & (f == 0))
    def _():
        wait_gather(slot)

        def issue(r, carry):
            scatter(r, other, False).start()
            return carry
        lax.fori_loop(0, tm, issue, 0, unroll=8)
        wait_scatter(other)


def _experts(x, src_of, dst_of, tile_expert, n_used, wg, wu, wd, tm, tf=1792):
    n_tok = x.shape[0]
    n_tiles = src_of.shape[0]
    nf = D_FF // tf
    fidx = lambda i, f, nu: jnp.where(i < nu[0], f, nf - 1)
    rows_of_tile = lambda shift: pl.BlockSpec(
        (1, 1, tm), lambda i, f, te, nu: (jnp.clip(i + shift, 0, n_tiles - 1), 0, 0),
        memory_space=pltpu.SMEM)
    grid_spec = pltpu.PrefetchScalarGridSpec(
        num_scalar_prefetch=2,
        grid=(n_tiles, nf),
        in_specs=[rows_of_tile(0), rows_of_tile(1), rows_of_tile(-1),
                  pl.BlockSpec(memory_space=pl.ANY),
                  pl.BlockSpec((1, D_MODEL, tf), lambda i, f, te, nu: (te[i], 0, fidx(i, f, nu))),
                  pl.BlockSpec((1, D_MODEL, tf), lambda i, f, te, nu: (te[i], 0, fidx(i, f, nu))),
                  pl.BlockSpec((1, tf, D_MODEL), lambda i, f, te, nu: (te[i], fidx(i, f, nu), 0))],
        out_specs=pl.BlockSpec(memory_space=pl.ANY),
        scratch_shapes=[pltpu.VMEM((2, tm, D_MODEL), F32), pltpu.VMEM((tm, D_MODEL), BF16),
                        pltpu.VMEM((2, tm, D_MODEL), F32),
                        pltpu.SemaphoreType.DMA((2,)), pltpu.SemaphoreType.DMA(())],
    )
    return pl.pallas_call(
        functools.partial(_expert_kernel, tm=tm, nf=nf, n_tok=n_tok),
        grid_spec=grid_spec,
        out_shape=jax.ShapeDtypeStruct((2 * n_tok + 2 * tm, D_MODEL), F32),
        compiler_params=_params("arbitrary", "arbitrary"),
        name="moe_experts",
    )(tile_expert, n_used, src_of, src_of, dst_of, x, wg, wu, wd)


def _combine_kernel(x_ref, y1_ref, y2_ref, gate_ref, g_ref, b_ref, o_ref):
    gates = gate_ref[...]
    y = gates[:, 0:1] * y1_ref[...] + gates[:, 1:2] * y2_ref[...]
    o_ref[...] = _layer_norm_rows(DN_ALPHA * x_ref[...] + y, g_ref[...], b_ref[...])


def _combine_ln(x, y, gates, g, b, tm=1024):
    T = x.shape[0]
    nt = T // tm
    return pl.pallas_call(
        _combine_kernel,
        grid=(nt,),
        in_specs=[pl.BlockSpec((tm, D_MODEL), lambda i: (i, 0)),
                  pl.BlockSpec((tm, D_MODEL), lambda i: (i, 0)),
                  pl.BlockSpec((tm, D_MODEL), lambda i: (nt + i, 0)),
                  pl.BlockSpec((tm, ROUTER_LANES), lambda i: (i, 0)),
                  pl.BlockSpec((1, D_MODEL), lambda i: (0, 0)),
                  pl.BlockSpec((1, D_MODEL), lambda i: (0, 0))],
        out_specs=pl.BlockSpec((tm, D_MODEL), lambda i: (i, 0)),
        out_shape=jax.ShapeDtypeStruct((T, D_MODEL), F32),
        compiler_params=_params("parallel"),
        name="moe_combine_ln",
    )(x, y, y, gates, g.reshape(1, -1), b.reshape(1, -1))


def _moe_ln(x, router_w, wg, wu, wd, g, b, tm=512):
    router_pad = jnp.pad(router_w, ((0, 0), (0, ROUTER_LANES - N_EXPERTS)))
    meta, gates, cnt = _router(x, router_pad)
    src_of, dst_of, tile_expert, n_used = _dispatch_tables(meta, cnt, tm)
    y = _experts(x, src_of, dst_of, tile_expert, n_used, wg, wu, wd, tm)
    return _combine_ln(x, y, gates, g, b)


def _block_diag(w):
    nb, n, _ = w.shape
    eye = jnp.eye(nb, dtype=w.dtype)
    return (eye[:, None, :, None] * w[:, :, None, :]).reshape(nb * n, nb * n)


def kernel(x, w_in, w_out, hg_lb_logits, hg_norm_g, lru_conv_w, lru_conv_b, lru_wa, lru_ba, lru_wx, lru_bx, lru_lambda, da_lq1, da_lk1, da_lq2, da_lk2, da_norm_g, ln1_g, ln1_b, ln2_g, ln2_b, ffn_wg, ffn_wu, ffn_wd, router_w, moe_wg, moe_wu, moe_wd):
    batch, seq, _ = x.shape
    depth = w_in.shape[0]
    xf = x.reshape(batch * seq, D_MODEL)
    lb_p = jax.nn.softmax(hg_lb_logits.astype(F32), axis=0)
    lb_all = jnp.cumsum(lb_p, axis=0) - lb_p[0:1]
    for l in range(depth):
        rec, att = _in_proj(xf, w_in[l].astype(BF16))
        o_hg = _hgrn(rec, lb_all[l], hg_norm_g[l], batch, seq)
        o_lru = _lru(rec, lru_conv_w[l], lru_conv_b[l], _block_diag(lru_wa[l]).astype(BF16),
                     _block_diag(lru_wx[l]).astype(BF16), lru_ba[l], lru_bx[l], lru_lambda[l],
                     batch, seq)
        o_da = _attention(att, da_lq1[l], da_lk1[l], da_lq2[l], da_lk2[l], da_norm_g[l], l,
                          batch, seq)
        xf = _out_proj_ln(o_hg, o_lru, o_da, w_out[l].astype(BF16), xf, ln1_g[l], ln1_b[l])
        j = l // 2
        if l % 2 == 0:
            xf = _ffn_ln(xf, ffn_wg[j].astype(BF16), ffn_wu[j].astype(BF16), ffn_wd[j].astype(BF16),
                         ln2_g[l], ln2_b[l])
        else:
            xf = _moe_ln(xf, router_w[j], moe_wg[j].astype(BF16), moe_wu[j].astype(BF16),
                         moe_wd[j].astype(BF16), ln2_g[l], ln2_b[l])
    return xf.reshape(batch, seq, D_MODEL)
```

```python
import functools
import math

import jax
import jax.numpy as jnp
from jax import lax
from jax.experimental import pallas as pl
from jax.experimental.pallas import tpu as pltpu

D_MODEL = 1024
DEPTH = 2
HG_HEADS = 4
HG_KEY_DIM = 64
HG_WIDTH = 256
HG_CHUNK = 16
LRU_WIDTH = 256
LRU_BLOCKS = 4
CONV_WIDTH = 4
LRU_C = 8.0
DA_WIDTH = 512
DA_HEADS = 4
DA_HEAD_DIM = 64
D_FF = 3584
N_EXPERTS = 8
DN_ALPHA = (2.0 * DEPTH) ** 0.25
EPS = 1e-5
REC_WIDTH = 4 * HG_WIDTH + 2 * LRU_WIDTH
ATT_WIDTH = 3 * DA_WIDTH

V7X_VMEM_BYTES = 64 * 1024 * 1024
VMEM_LIMIT = V7X_VMEM_BYTES * 3 // 4

BF16 = jnp.bfloat16
F32 = jnp.float32


def _params(*semantics):
    return pltpu.CompilerParams(dimension_semantics=semantics, vmem_limit_bytes=VMEM_LIMIT)


def _layer_norm_rows(y, g, b):
    mu = jnp.mean(y, axis=-1, keepdims=True)
    yc = y - mu
    var = jnp.mean(yc * yc, axis=-1, keepdims=True)
    return yc * lax.rsqrt(var + EPS) * g + b


Q_SCALE = DA_HEAD_DIM ** -0.5 * math.log2(math.e)


def _in_proj_kernel(x_ref, w_ref, rec_ref, att_ref):
    xb = x_ref[...].astype(BF16)
    rec_ref[...] = jnp.dot(xb, w_ref[:, :REC_WIDTH], preferred_element_type=F32)
    q = jnp.dot(xb, w_ref[:, REC_WIDTH:REC_WIDTH + DA_WIDTH], preferred_element_type=F32)
    att_ref[:, :DA_WIDTH] = (q * Q_SCALE).astype(BF16)
    att_ref[:, DA_WIDTH:] = jnp.dot(xb, w_ref[:, REC_WIDTH + DA_WIDTH:],
                                    preferred_element_type=F32).astype(BF16)


def _in_proj(x, w_bf16, tm=512):
    T = x.shape[0]
    return pl.pallas_call(
        _in_proj_kernel,
        grid=(T // tm,),
        in_specs=[pl.BlockSpec((tm, D_MODEL), lambda i: (i, 0)),
                  pl.BlockSpec((D_MODEL, REC_WIDTH + ATT_WIDTH), lambda i: (0, 0))],
        out_specs=[pl.BlockSpec((tm, REC_WIDTH), lambda i: (i, 0)),
                   pl.BlockSpec((tm, ATT_WIDTH), lambda i: (i, 0))],
        out_shape=[jax.ShapeDtypeStruct((T, REC_WIDTH), F32),
                   jax.ShapeDtypeStruct((T, ATT_WIDTH), BF16)],
        compiler_params=_params("parallel"),
        name="in_proj",
    )(x, w_bf16)


def _hgrn_kernel(rec_ref, lb_ref, ng_ref, gmat_ref, sel_ref, bmask_ref, o_ref,
                 st_ref, q_s, kk_s, b_s, bl_s, qd_s, kd_s, o_s, *, rows):
    @pl.when(pl.program_id(1) == 0)
    def _():
        st_ref[...] = jnp.zeros_like(st_ref)

    C = HG_CHUNK
    W = HG_WIDTH
    nc = rows // C
    lb = lb_ref[...]
    gmat = gmat_ref[...]
    sel = sel_ref[...]

    qr = rec_ref[:, 0:W]
    z = rec_ref[:, W:2 * W]
    rin = lax.broadcasted_iota(jnp.int32, (rows, W), 0) % C
    logf = jnp.log(lb + (1.0 - lb) * jax.nn.sigmoid(z))
    kk = (1.0 - lb) * jax.nn.sigmoid(-z)
    q = qr * jax.nn.sigmoid(qr)
    b = logf
    for s in (1, 2, 4, 8):
        b = b + jnp.where(rin >= s, pltpu.roll(b, s, 0), 0.0)
    r = jnp.where(rin < C - 1, pltpu.roll(logf, rows - 1, 0), 0.0)
    for s in (1, 2, 4, 8):
        r = r + jnp.where(rin < C - s, pltpu.roll(r, rows - s, 0), 0.0)
    q_s[...] = q
    kk_s[...] = kk
    b_s[...] = b
    bl_s[...] = b + r
    qd_s[...] = (q * jnp.exp(b)).astype(BF16)
    kd_s[...] = (kk * jnp.exp(r)).astype(BF16)

    row = lax.broadcasted_iota(jnp.int32, (C, W), 0)

    def chunk(c, carry):
        r0 = pl.multiple_of(c * C, C)
        q = q_s[pl.ds(r0, C), :]
        kk = kk_s[pl.ds(r0, C), :]
        b = b_s[pl.ds(r0, C), :]
        v = rec_ref[pl.ds(r0, C), 2 * W:3 * W]
        w_rows = []
        for t in range(C):
            rel = b[t:t + 1, :] - b
            dec = jnp.exp(jnp.where(row <= t, rel, -jnp.inf))
            w_rows.append(((q[t:t + 1, :] * kk) * dec).astype(BF16))
        w2 = jnp.concatenate(w_rows, axis=0)
        a = jnp.dot(w2, gmat, preferred_element_type=F32)
        p = a * jnp.concatenate([v] * C, axis=0)
        o_intra = jnp.dot(sel, p.astype(BF16), preferred_element_type=F32)
        u_t = lax.dot_general(v.astype(BF16), kd_s[pl.ds(r0, C), :], (((0,), (0,)), ((), ())),
                              preferred_element_type=F32)
        st = st_ref[...]
        o_inter = lax.dot_general(qd_s[pl.ds(r0, C), :], st.astype(BF16), (((1,), (1,)), ((), ())),
                                  preferred_element_type=F32)
        o_s[pl.ds(r0, C), :] = o_intra + o_inter
        st_ref[...] = st * jnp.exp(bl_s[pl.ds(r0, 1), :]) + u_t * bmask_ref[...]
        return carry

    lax.fori_loop(0, nc, chunk, 0, unroll=16)

    o = o_s[...]
    g = rec_ref[:, 3 * W:4 * W]
    sq = o * o
    sq_hi = sq.astype(BF16)
    sq_lo = (sq - sq_hi.astype(F32)).astype(BF16)
    ms = (jnp.dot(sq_hi, gmat, preferred_element_type=F32)
          + jnp.dot(sq_lo, gmat, preferred_element_type=F32)) * (1.0 / HG_KEY_DIM)
    out = o * lax.rsqrt(ms + EPS) * ng_ref[...] * (g * jax.nn.sigmoid(g))
    o_ref[...] = out.astype(o_ref.dtype)


def _hgrn(rec, lb, norm_g, batch, seq, rows=512):
    T = rec.shape[0]
    nblk = seq // rows
    head = jnp.arange(HG_WIDTH) // HG_KEY_DIM
    same_head = head[:, None] == head[None, :]
    gmat = same_head.astype(BF16)
    bmask = same_head.astype(F32)
    sel = (jnp.arange(HG_CHUNK)[:, None] == (jnp.arange(HG_CHUNK * HG_CHUNK) // HG_CHUNK)[None, :]).astype(BF16)
    const = lambda shape: pl.BlockSpec(shape, lambda b, i: (0, 0))
    return pl.pallas_call(
        functools.partial(_hgrn_kernel, rows=rows),
        grid=(batch, nblk),
        in_specs=[pl.BlockSpec((rows, 4 * HG_WIDTH), lambda b, i: (b * nblk + i, 0)),
                  const((1, HG_WIDTH)), const((1, HG_WIDTH)),
                  const((HG_WIDTH, HG_WIDTH)), const((HG_CHUNK, HG_CHUNK * HG_CHUNK)),
                  const((HG_WIDTH, HG_WIDTH))],
        out_specs=pl.BlockSpec((rows, HG_WIDTH), lambda b, i: (b * nblk + i, 0)),
        out_shape=jax.ShapeDtypeStruct((T, HG_WIDTH), BF16),
        scratch_shapes=[pltpu.VMEM((HG_WIDTH, HG_WIDTH), F32)]
        + [pltpu.VMEM((rows, HG_WIDTH), F32)] * 4
        + [pltpu.VMEM((rows, HG_WIDTH), BF16)] * 2
        + [pltpu.VMEM((rows, HG_WIDTH), F32)],
        compiler_params=_params("parallel", "arbitrary"),
        name="hgrn2",
    )(rec, lb.reshape(1, -1), norm_g.reshape(1, -1), gmat, sel, bmask)


def _lru_kernel(rec_ref, cw_ref, cb_ref, wa_ref, wx_ref, ba_ref, bx_ref, lam_ref, o_ref,
                xprev_ref, h_ref, *, rows):
    @pl.when(pl.program_id(1) == 0)
    def _():
        xprev_ref[...] = jnp.zeros_like(xprev_ref)
        h_ref[...] = jnp.zeros_like(h_ref)

    W = LRU_WIDTH
    x = rec_ref[:, 0:W]
    gate = rec_ref[:, W:2 * W]
    row = lax.broadcasted_iota(jnp.int32, (rows, W), 0)
    xp = xprev_ref[...]
    tail = jnp.zeros((rows - 8, W), F32)
    xc = cb_ref[...] + cw_ref[CONV_WIDTH - 1:CONV_WIDTH, :] * x
    for j in range(1, CONV_WIDTH):
        prev = jnp.concatenate([pltpu.roll(xp, j, 0), tail], axis=0)
        xs = jnp.where(row >= j, pltpu.roll(x, j, 0), prev)
        xc = xc + cw_ref[CONV_WIDTH - 1 - j:CONV_WIDTH - j, :] * xs
    xprev_ref[...] = x[rows - 8:rows, :]

    xcb = xc.astype(BF16)
    r = jax.nn.sigmoid(jnp.dot(xcb, wa_ref[...], preferred_element_type=F32) + ba_ref[...])
    i = jax.nn.sigmoid(jnp.dot(xcb, wx_ref[...], preferred_element_type=F32) + bx_ref[...])
    lam = lam_ref[...]
    log_sig = jnp.minimum(lam, 0.0) - jnp.log1p(jnp.exp(-jnp.abs(lam)))
    log_a = LRU_C * r * log_sig
    a = jnp.exp(log_a)
    th = jnp.tanh(log_a)
    one_minus_a2 = -2.0 * th / (1.0 - th)
    u = jnp.sqrt(one_minus_a2) * i * xc

    for s in (1, 2, 4):
        a_s = jnp.where(row >= s, pltpu.roll(a, s, 0), 1.0)
        u_s = jnp.where(row >= s, pltpu.roll(u, s, 0), 0.0)
        u = a * u_s + u
        a = a * a_s
    s = 8
    while s < rows:
        a_s = jnp.concatenate([jnp.ones((s, W), F32), a[:rows - s, :]], axis=0)
        u_s = jnp.concatenate([jnp.zeros((s, W), F32), u[:rows - s, :]], axis=0)
        u = a * u_s + u
        a = a * a_s
        s *= 2
    h = a * h_ref[0:1, :] + u
    h_ref[...] = jnp.broadcast_to(h[rows - 1:rows, :], h_ref.shape)
    o_ref[...] = (h * jax.nn.gelu(gate)).astype(o_ref.dtype)


def _lru(rec, conv_w, conv_b, wa_bd, wx_bd, ba, bx, lam, batch, seq, rows=512):
    T = rec.shape[0]
    nblk = seq // rows
    W = LRU_WIDTH
    const = lambda shape: pl.BlockSpec(shape, lambda b, i: (0, 0))
    return pl.pallas_call(
        functools.partial(_lru_kernel, rows=rows),
        grid=(batch, nblk),
        in_specs=[pl.BlockSpec((rows, 2 * W), lambda b, i: (b * nblk + i, 2)),
                  const((CONV_WIDTH, W)), const((1, W)), const((W, W)), const((W, W)),
                  const((1, W)), const((1, W)), const((1, W))],
        out_specs=pl.BlockSpec((rows, W), lambda b, i: (b * nblk + i, 0)),
        out_shape=jax.ShapeDtypeStruct((T, W), BF16),
        scratch_shapes=[pltpu.VMEM((8, W), F32), pltpu.VMEM((8, W), F32)],
        compiler_params=_params("parallel", "arbitrary"),
        name="rglru",
    )(rec, conv_w, conv_b.reshape(1, -1), wa_bd, wx_bd, ba.reshape(1, -1), bx.reshape(1, -1),
      lam.reshape(1, -1))


def _attn_kernel(q_ref, qn_ref, k_ref, v_ref, lq1_ref, lk1_ref, lq2_ref, lk2_ref, ng_ref, o_ref,
                 vt_ref, qm_ref, s0_ref, s1_ref, m_ref, l_ref, acc_ref, *, tq, tk, qs, lam_init, seq):
    d = DA_HEAD_DIM
    qi = pl.program_id(2)

    @pl.when(qi == 0)
    def _():
        def transpose_values(j, carry):
            r0 = pl.multiple_of(j * tk, tk)
            vt_ref[j] = v_ref[pl.ds(r0, tk), :].astype(F32).T.astype(BF16)
            return carry
        lax.fori_loop(0, seq // tk, transpose_values, 0)

    q = q_ref[...]
    lane = lax.broadcasted_iota(jnp.int32, q.shape, 1)
    qm_ref[0] = jnp.where(lane < d, q, jnp.zeros_like(q))
    qm_ref[1] = jnp.where(lane >= d, q, jnp.zeros_like(q))
    m_ref[...] = jnp.full_like(m_ref, -jnp.inf)
    l_ref[...] = jnp.zeros_like(l_ref)
    acc_ref[...] = jnp.zeros_like(acc_ref)
    s_refs = (s0_ref, s1_ref)

    def scores(j, c, queries=None):
        k0 = pl.multiple_of(j * tk, tk)
        kb = k_ref[pl.ds(k0, tk), :]
        qsrc = c if queries is None else queries
        for st in range(tq // qs):
            s_refs[c][st] = lax.dot_general(kb, qm_ref[qsrc, st * qs:(st + 1) * qs, :],
                                            (((1,), (1,)), ((), ())),
                                            preferred_element_type=F32)

    def softmax_pv(j, c, masked):
        for st in range(tq // qs):
            cols = pl.ds(st * qs, qs)
            nk = min((st + 1) * qs, tk) if masked else tk
            vt = vt_ref[j, :, 0:nk]
            s = s_refs[c][st, 0:nk, :]
            if masked:
                kpos = lax.broadcasted_iota(jnp.int32, (nk, qs), 0)
                qpos = st * qs + lax.broadcasted_iota(jnp.int32, (nk, qs), 1)
                s = jnp.where(kpos <= qpos, s, -jnp.inf)
            m_old = m_ref[c, :, cols]
            m_new = jnp.maximum(m_old, jnp.max(s, axis=0, keepdims=True))
            p = jnp.exp2(s - m_new)
            alpha = jnp.exp2(m_old - m_new)
            l_ref[c, :, cols] = alpha * l_ref[c, :, cols] + jnp.sum(p, axis=0, keepdims=True)
            acc_ref[c, st] = (alpha * acc_ref[c, st]
                                   + jnp.dot(vt, p.astype(BF16), preferred_element_type=F32))
            m_ref[c, :, cols] = m_new

    def full_block(j, carry):
        scores(j, 1)
        softmax_pv(j, 0, False)
        scores(j + 1, 0)
        softmax_pv(j, 1, False)
        return carry

    def four_full_blocks(jj, carry):
        for u in range(4):
            full_block(4 * jj + u, carry)
        return carry

    @pl.when(qi == 0)
    def _():
        scores(0, 0)

    lax.fori_loop(0, qi // 4, four_full_blocks, 0)

    @pl.when(qi % 4 >= 2)
    def _():
        full_block((qi // 4) * 4, 0)
        full_block((qi // 4) * 4 + 1, 0)

    @pl.when(qi % 2 == 1)
    def _():
        full_block(qi - 1, 0)

    scores(qi, 1)
    softmax_pv(qi, 0, True)
    qn = qn_ref[...]
    qm_ref[2] = jnp.where(lane < d, qn, jnp.zeros_like(qn))
    scores(0, 0, queries=2)
    softmax_pv(qi, 1, True)

    lam =(jnp.exp(jnp.sum(lq1_ref[...] * lk1_ref[...], axis=-1, keepdims=True))
           - jnp.exp(jnp.sum(lq2_ref[...] * lk2_ref[...], axis=-1, keepdims=True)) + lam_init)
    acc = [jnp.concatenate([acc_ref[c, st] for st in range(tq // qs)], axis=1) for c in range(2)]
    o_t = acc[0] * (1.0 / l_ref[0]) - lam * (acc[1] * (1.0 / l_ref[1]))
    ms = jnp.mean(o_t * o_t, axis=0, keepdims=True)
    o_t = o_t * lax.rsqrt(ms + EPS) * ng_ref[...] * (1.0 - lam_init)
    o_ref[...] = o_t.T.astype(o_ref.dtype)


def _attention(att, lq1, lk1, lq2, lk2, norm_g, layer, batch, seq, tq=512, qs=256):
    T = att.shape[0]
    tk = tq
    nq = seq // tq
    hw = 2 * DA_HEAD_DIM
    lam_init = 0.8 - 0.6 * math.exp(-0.3 * layer)
    vec = lambda: pl.BlockSpec((1, DA_HEAD_DIM), lambda b, h, i: (0, 0))
    return pl.pallas_call(
        functools.partial(_attn_kernel, tq=tq, tk=tk, qs=qs, lam_init=lam_init, seq=seq),
        grid=(batch, DA_HEADS, nq),
        in_specs=[pl.BlockSpec((tq, hw), lambda b, h, i: (b * nq + i, h)),
                  pl.BlockSpec((tq, hw), lambda b, h, i: (b * nq + jnp.minimum(i + 1, nq - 1), h)),
                  pl.BlockSpec((seq, hw), lambda b, h, i: (b, DA_HEADS + h)),
                  pl.BlockSpec((seq, hw), lambda b, h, i: (b, 2 * DA_HEADS + h)),
                  vec(), vec(), vec(), vec(),
                  pl.BlockSpec((hw, 1), lambda b, h, i: (h, 0))],
        out_specs=pl.BlockSpec((tq, hw), lambda b, h, i: (b * nq + i, h)),
        out_shape=jax.ShapeDtypeStruct((T, DA_WIDTH), BF16),
        scratch_shapes=[pltpu.VMEM((seq // tk, hw, tk), BF16), pltpu.VMEM((3, tq, hw), BF16),
                        pltpu.VMEM((tq // qs, tk, qs), F32), pltpu.VMEM((tq // qs, tk, qs), F32),
                        pltpu.VMEM((2, 1, tq), F32), pltpu.VMEM((2, 1, tq), F32),
                        pltpu.VMEM((2, tq // qs, hw, qs), F32)],
        compiler_params=_params("arbitrary", "arbitrary", "arbitrary"),
        name="diff_attn",
    )(att, att, att, att, lq1.reshape(1, -1), lk1.reshape(1, -1), lq2.reshape(1, -1),
      lk2.reshape(1, -1), norm_g.reshape(-1, 1))


def _out_proj_kernel(hg_ref, lru_ref, da_ref, w_ref, x_ref, g_ref, b_ref, o_ref):
    h = jnp.dot(hg_ref[...], w_ref[0:HG_WIDTH, :], preferred_element_type=F32)
    h = h + jnp.dot(lru_ref[...], w_ref[HG_WIDTH:HG_WIDTH + LRU_WIDTH, :], preferred_element_type=F32)
    h = h + jnp.dot(da_ref[...], w_ref[HG_WIDTH + LRU_WIDTH:, :], preferred_element_type=F32)
    o_ref[...] = _layer_norm_rows(DN_ALPHA * x_ref[...] + h, g_ref[...], b_ref[...])


def _out_proj_ln(o_hg, o_lru, o_da, w_bf16, x, g, b, tm=1024):
    T = x.shape[0]
    rows = lambda w: pl.BlockSpec((tm, w), lambda i: (i, 0))
    const = lambda shape: pl.BlockSpec(shape, lambda i: (0, 0))
    return pl.pallas_call(
        _out_proj_kernel,
        grid=(T // tm,),
        in_specs=[rows(HG_WIDTH), rows(LRU_WIDTH), rows(DA_WIDTH), const((D_MODEL, D_MODEL)),
                  rows(D_MODEL), const((1, D_MODEL)), const((1, D_MODEL))],
        out_specs=rows(D_MODEL),
        out_shape=jax.ShapeDtypeStruct((T, D_MODEL), F32),
        compiler_params=_params("parallel"),
        name="out_proj_ln",
    )(o_hg, o_lru, o_da, w_bf16, x, g.reshape(1, -1), b.reshape(1, -1))


def _ffn_kernel(x_ref, wg_ref, wu_ref, wd_ref, g_ref, b_ref, o_ref, xb_ref, acc_ref):
    f = pl.program_id(1)
    nf = pl.num_programs(1)

    def step(first, last):
        if first:
            xb_ref[...] = x_ref[...].astype(BF16)
        xb = xb_ref[...]
        gate = jnp.dot(xb, wg_ref[...], preferred_element_type=F32)
        up = jnp.dot(xb, wu_ref[...], preferred_element_type=F32)
        hmid = (gate * jax.nn.sigmoid(gate) * up).astype(BF16)
        down = jnp.dot(hmid, wd_ref[...], preferred_element_type=F32)
        if first:
            acc_ref[...] = down
        elif last:
            o_ref[...] = _layer_norm_rows(DN_ALPHA * x_ref[...] + (acc_ref[...] + down),
                                          g_ref[...], b_ref[...])
        else:
            acc_ref[...] += down

    pl.when(f == 0)(functools.partial(step, True, False))
    pl.when((f > 0) & (f < nf - 1))(functools.partial(step, False, False))
    pl.when(f == nf - 1)(functools.partial(step, False, True))


def _ffn_ln(x, wg, wu, wd, g, b, tm=1024, tf=512):
    T = x.shape[0]
    return pl.pallas_call(
        _ffn_kernel,
        grid=(T // tm, D_FF // tf),
        in_specs=[pl.BlockSpec((tm, D_MODEL), lambda i, f: (i, 0)),
                  pl.BlockSpec((D_MODEL, tf), lambda i, f: (0, f)),
                  pl.BlockSpec((D_MODEL, tf), lambda i, f: (0, f)),
                  pl.BlockSpec((tf, D_MODEL), lambda i, f: (f, 0)),
                  pl.BlockSpec((1, D_MODEL), lambda i, f: (0, 0)),
                  pl.BlockSpec((1, D_MODEL), lambda i, f: (0, 0))],
        out_specs=pl.BlockSpec((tm, D_MODEL), lambda i, f: (i, 0)),
        out_shape=jax.ShapeDtypeStruct((T, D_MODEL), F32),
        scratch_shapes=[pltpu.VMEM((tm, D_MODEL), BF16), pltpu.VMEM((tm, D_MODEL), F32)],
        compiler_params=_params("parallel", "arbitrary"),
        name="ffn_ln",
    )(x, wg, wu, wd, g.reshape(1, -1), b.reshape(1, -1))


ROUTER_LANES = 128
META_I1, META_I2, META_R1, META_R2 = 0, 1, 2, 3


def _router_kernel(x_ref, rwh_ref, rwl_ref, tri_ref, meta_ref, gate_ref, cnt_ref, carry_ref):
    @pl.when(pl.program_id(0) == 0)
    def _():
        carry_ref[...] = jnp.zeros_like(carry_ref)

    x = x_ref[...]
    x_hi = x.astype(BF16)
    x_lo = (x - x_hi.astype(F32)).astype(BF16)
    logits = (jnp.dot(x_hi, rwh_ref[...], preferred_element_type=F32)
              + jnp.dot(x_lo, rwh_ref[...], preferred_element_type=F32)
              + jnp.dot(x_hi, rwl_ref[...], preferred_element_type=F32))
    lane = lax.broadcasted_iota(jnp.int32, logits.shape, 1)
    neg = -jnp.inf
    l1 = jnp.where(lane < N_EXPERTS, logits, neg)
    v1 = jnp.max(l1, axis=-1, keepdims=True)
    i1 = jnp.min(jnp.where(l1 == v1, lane, ROUTER_LANES), axis=-1, keepdims=True)
    l2 = jnp.where(lane == i1, neg, l1)
    v2 = jnp.max(l2, axis=-1, keepdims=True)
    i2 = jnp.min(jnp.where(l2 == v2, lane, ROUTER_LANES), axis=-1, keepdims=True)
    e2 = jnp.exp(v2 - v1)
    g1 = 1.0 / (1.0 + e2)
    g2 = e2 / (1.0 + e2)
    member = jnp.where((lane == i1) | (lane == i2), 1.0, 0.0)
    rank = jnp.dot(tri_ref[...], member.astype(BF16), preferred_element_type=F32) + carry_ref[0:1, :]
    r1 = jnp.sum(jnp.where(lane == i1, rank, 0.0), axis=-1, keepdims=True).astype(jnp.int32)
    r2 = jnp.sum(jnp.where(lane == i2, rank, 0.0), axis=-1, keepdims=True).astype(jnp.int32)
    meta = jnp.where(lane == META_I1, i1, 0) + jnp.where(lane == META_I2, i2, 0)
    meta = meta + jnp.where(lane == META_R1, r1, 0) + jnp.where(lane == META_R2, r2, 0)
    meta_ref[...] = meta
    gate_ref[...] = jnp.where(lane == 0, g1, 0.0) + jnp.where(lane == 1, g2, 0.0)
    carry = carry_ref[...] + jnp.sum(member, axis=0, keepdims=True)
    carry_ref[...] = carry
    cnt_ref[...] = carry


def _router(x, router_pad, tm=1024):
    T = x.shape[0]
    tri = (jnp.arange(tm)[:, None] > jnp.arange(tm)[None, :]).astype(BF16)
    rw_hi = router_pad.astype(BF16)
    rw_lo = (router_pad - rw_hi.astype(F32)).astype(BF16)
    return pl.pallas_call(
        _router_kernel,
        grid=(T // tm,),
        in_specs=[pl.BlockSpec((tm, D_MODEL), lambda i: (i, 0)),
                  pl.BlockSpec((D_MODEL, ROUTER_LANES), lambda i: (0, 0)),
                  pl.BlockSpec((D_MODEL, ROUTER_LANES), lambda i: (0, 0)),
                  pl.BlockSpec((tm, tm), lambda i: (0, 0))],
        out_specs=[pl.BlockSpec((tm, ROUTER_LANES), lambda i: (i, 0)),
                   pl.BlockSpec((tm, ROUTER_LANES), lambda i: (i, 0)),
                   pl.BlockSpec((8, ROUTER_LANES), lambda i: (0, 0))],
        out_shape=[jax.ShapeDtypeStruct((T, ROUTER_LANES), jnp.int32),
                   jax.ShapeDtypeStruct((T, ROUTER_LANES), F32),
                   jax.ShapeDtypeStruct((8, ROUTER_LANES), F32)],
        scratch_shapes=[pltpu.VMEM((8, ROUTER_LANES), F32)],
        compiler_params=_params("arbitrary"),
        name="moe_router",
    )(x, rw_hi, rw_lo, tri)


def _dispatch_tables(meta, cnt, tm):
    T = meta.shape[0]
    counts = cnt[0, :N_EXPERTS].astype(jnp.int32)
    padded = ((counts + tm - 1) // tm) * tm
    ends = jnp.cumsum(padded)
    off = ends - padded
    experts = jnp.arange(N_EXPERTS, dtype=jnp.int32)
    group_start = lambda e: jnp.sum(jnp.where(e[:, None] == experts[None, :], off[None, :], 0), axis=1)
    pos1 = group_start(meta[:, META_I1]) + meta[:, META_R1]
    pos2 = group_start(meta[:, META_I2]) + meta[:, META_R2]
    n_tiles = 2 * T // tm + N_EXPERTS + 1
    tok = jnp.arange(T, dtype=jnp.int32)
    code_of = jnp.full((n_tiles * tm,), 2 * T, jnp.int32).at[jnp.concatenate([pos1, pos2])].set(
        jnp.concatenate([tok, tok + T]))
    tile_start = jnp.arange(n_tiles, dtype=jnp.int32) * tm
    tile_expert = jnp.minimum(jnp.sum((tile_start[:, None] >= ends[None, :]).astype(jnp.int32), axis=1),
                              N_EXPERTS - 1)
    n_used = (ends[-1] // tm).astype(jnp.int32).reshape(1)
    row = jnp.arange(n_tiles * tm, dtype=jnp.int32)
    is_pad = code_of >= 2 * T
    src_of = jnp.where(is_pad, 0, jnp.where(code_of >= T, code_of - T, code_of))
    dst_of = jnp.where(is_pad, 2 * T + ((row // tm) % 2) * tm + row % tm, code_of)
    return src_of.reshape(n_tiles, 1, tm), dst_of.reshape(n_tiles, 1, tm), tile_expert, n_used


def _expert_kernel(te_ref, nu_ref, cur_ref, nxt_ref, prv_ref, x_hbm, wg_ref, wu_ref, wd_ref, y_hbm,
                   xbuf_ref, xb_ref, acc_ref, sem_g, sem_s, *, tm, nf, n_tok):
    i = pl.program_id(0)
    f = pl.program_id(1)
    n_used = nu_ref[0]
    slot = i % 2
    other = 1 - slot
    pad_code = 2 * n_tok

    def gather(src_ref, r, buf):
        return pltpu.make_async_copy(x_hbm.at[pl.ds(src_ref[0, 0, r], 1)],
                                     xbuf_ref.at[buf, pl.ds(r, 1)], sem_g.at[buf])

    def scatter(r, buf, to_dump):
        dest = jnp.where(to_dump, pad_code + buf * tm + r, prv_ref[0, 0, r])
        return pltpu.make_async_copy(acc_ref.at[buf, pl.ds(r, 1)], y_hbm.at[pl.ds(dest, 1)], sem_s)

    def wait_gather(buf):
        pltpu.make_async_copy(x_hbm.at[pl.ds(0, tm)], xbuf_ref.at[buf], sem_g.at[buf]).wait()

    def wait_scatter(buf):
        pltpu.make_async_copy(acc_ref.at[buf], y_hbm.at[pl.ds(0, tm)], sem_s).wait()

    @pl.when((i < n_used) & (f == 0))
    def _():
        @pl.when(i == 0)
        def _():
            def issue(r, carry):
                gather(cur_ref, r, 0).start()
                return carry
            lax.fori_loop(0, tm, issue, 0, unroll=8)
            acc_ref[1] = jnp.zeros((tm, D_MODEL), F32)
            clear = pltpu.make_async_copy(acc_ref.at[1], y_hbm.at[pl.ds(pad_code, tm)], sem_s)
            clear.start()
            clear.wait()

        wait_gather(slot)
        xb_ref[...] = xbuf_ref[slot].astype(BF16)

    def tile_step(fs):
        if fs == 0:
            for r in range(tm):
                gather(nxt_ref, r, other).start()
        if fs == nf - 1:
            for r in range(tm):
                scatter(r, other, i == 0).start(priority=r % 2)
        xb = xb_ref[...]
        gate = jnp.dot(xb, wg_ref[0], preferred_element_type=F32)
        up = jnp.dot(xb, wu_ref[0], preferred_element_type=F32)
        hmid = (gate * jax.nn.sigmoid(gate) * up).astype(BF16)
        down = jnp.dot(hmid, wd_ref[0], preferred_element_type=F32)
        if fs == 0:
            acc_ref[slot] = down
        else:
            acc_ref[slot] += down
        if fs == nf - 1:
            wait_scatter(other)

    for fs in range(nf):
        pl.when((i < n_used) & (f == fs))(functools.partial(tile_step, fs))

    @pl.when((i == n_used) & (f == 0))
    def _():
        wait_gather(slot)

        def issue(r, carry):
            scatter(r, other, False).start()
            return carry
        lax.fori_loop(0, tm, issue, 0, unroll=8)
        wait_scatter(other)


def _experts(x, src_of, dst_of, tile_expert, n_used, wg, wu, wd, tm, tf=1792):
    n_tok = x.shape[0]
    n_tiles = src_of.shape[0]
    nf = D_FF // tf
    fidx = lambda i, f, nu: jnp.where(i < nu[0], f, nf - 1)
    rows_of_tile = lambda shift: pl.BlockSpec(
        (1, 1, tm), lambda i, f, te, nu: (jnp.clip(i + shift, 0, n_tiles - 1), 0, 0),
        memory_space=pltpu.SMEM)
    grid_spec = pltpu.PrefetchScalarGridSpec(
        num_scalar_prefetch=2,
        grid=(n_tiles, nf),
        in_specs=[rows_of_tile(0), rows_of_tile(1), rows_of_tile(-1),
                  pl.BlockSpec(memory_space=pl.ANY),
                  pl.BlockSpec((1, D_MODEL, tf), lambda i, f, te, nu: (te[i], 0, fidx(i, f, nu))),
                  pl.BlockSpec((1, D_MODEL, tf), lambda i, f, te, nu: (te[i], 0, fidx(i, f, nu))),
                  pl.BlockSpec((1, tf, D_MODEL), lambda i, f, te, nu: (te[i], fidx(i, f, nu), 0))],
        out_specs=pl.BlockSpec(memory_space=pl.ANY),
        scratch_shapes=[pltpu.VMEM((2, tm, D_MODEL), F32), pltpu.VMEM((tm, D_MODEL), BF16),
                        pltpu.VMEM((2, tm, D_MODEL), F32),
                        pltpu.SemaphoreType.DMA((2,)), pltpu.SemaphoreType.DMA(())],
    )
    return pl.pallas_call(
        functools.partial(_expert_kernel, tm=tm, nf=nf, n_tok=n_tok),
        grid_spec=grid_spec,
        out_shape=jax.ShapeDtypeStruct((2 * n_tok + 2 * tm, D_MODEL), F32),
        compiler_params=_params("arbitrary", "arbitrary"),
        name="moe_experts",
    )(tile_expert, n_used, src_of, src_of, dst_of, x, wg, wu, wd)


def _combine_kernel(x_ref, y1_ref, y2_ref, gate_ref, g_ref, b_ref, o_ref):
    gates = gate_ref[...]
    y = gates[:, 0:1] * y1_ref[...] + gates[:, 1:2] * y2_ref[...]
    o_ref[...] = _layer_norm_rows(DN_ALPHA * x_ref[...] + y, g_ref[...], b_ref[...])


def _combine_ln(x, y, gates, g, b, tm=1024):
    T = x.shape[0]
    nt = T // tm
    return pl.pallas_call(
        _combine_kernel,
        grid=(nt,),
        in_specs=[pl.BlockSpec((tm, D_MODEL), lambda i: (i, 0)),
                  pl.BlockSpec((tm, D_MODEL), lambda i: (i, 0)),
                  pl.BlockSpec((tm, D_MODEL), lambda i: (nt + i, 0)),
                  pl.BlockSpec((tm, ROUTER_LANES), lambda i: (i, 0)),
                  pl.BlockSpec((1, D_MODEL), lambda i: (0, 0)),
                  pl.BlockSpec((1, D_MODEL), lambda i: (0, 0))],
        out_specs=pl.BlockSpec((tm, D_MODEL), lambda i: (i, 0)),
        out_shape=jax.ShapeDtypeStruct((T, D_MODEL), F32),
        compiler_params=_params("parallel"),
        name="moe_combine_ln",
    )(x, y, y, gates, g.reshape(1, -1), b.reshape(1, -1))


def _moe_ln(x, router_w, wg, wu, wd, g, b, tm=512):
    router_pad = jnp.pad(router_w, ((0, 0), (0, ROUTER_LANES - N_EXPERTS)))
    meta, gates, cnt = _router(x, router_pad)
    src_of, dst_of, tile_expert, n_used = _dispatch_tables(meta, cnt, tm)
    y = _experts(x, src_of, dst_of, tile_expert, n_used, wg, wu, wd, tm)
    return _combine_ln(x, y, gates, g, b)


def _block_diag(w):
    nb, n, _ = w.shape
    eye = jnp.eye(nb, dtype=w.dtype)
    return (eye[:, None, :, None] * w[:, :, None, :]).reshape(nb * n, nb * n)


def kernel(x, w_in, w_out, hg_lb_logits, hg_norm_g, lru_conv_w, lru_conv_b, lru_wa, lru_ba, lru_wx, lru_bx, lru_lambda, da_lq1, da_lk1, da_lq2, da_lk2, da_norm_g, ln1_g, ln1_b, ln2_g, ln2_b, ffn_wg, ffn_wu, ffn_wd, router_w, moe_wg, moe_wu, moe_wd):
    batch, seq, _ = x.shape
    depth = w_in.shape[0]
    xf = x.reshape(batch * seq, D_MODEL)
    lb_p = jax.nn.softmax(hg_lb_logits.astype(F32), axis=0)
    lb_all = jnp.cumsum(lb_p, axis=0) - lb_p[0:1]
    for l in range(depth):
        rec, att = _in_proj(xf, w_in[l].astype(BF16))
        o_hg = _hgrn(rec, lb_all[l], hg_norm_g[l], batch, seq)
        o_lru = _lru(rec, lru_conv_w[l], lru_conv_b[l], _block_diag(lru_wa[l]).astype(BF16),
                     _block_diag(lru_wx[l]).astype(BF16), lru_ba[l], lru_bx[l], lru_lambda[l],
                     batch, seq)
        o_da = _attention(att, da_lq1[l], da_lk1[l], da_lq2[l], da_lk2[l], da_norm_g[l], l,
                          batch, seq)
        xf = _out_proj_ln(o_hg, o_lru, o_da, w_out[l].astype(BF16), xf, ln1_g[l], ln1_b[l])
        j = l // 2
        if l % 2 == 0:
            xf = _ffn_ln(xf, ffn_wg[j].astype(BF16), ffn_wu[j].astype(BF16), ffn_wd[j].astype(BF16),
                         ln2_g[l], ln2_b[l])
        else:
            xf = _moe_ln(xf, router_w[j], moe_wg[j].astype(BF16), moe_wu[j].astype(BF16),
                         moe_wd[j].astype(BF16), ln2_g[l], ln2_b[l])
    return xf.reshape(batch, seq, D_MODEL)
```
